```python
import jax, jax.numpy as jnp
from jax import lax
import numpy as np

D_MODEL = 1024
BATCH = 16
SEQ = 256
DEPTH = 4
DEC_BATCH = 8
DEC_SEQ = 1024
PAST_LEN = 256

GRID_W = 64
N_EVEN = (DEPTH + 1) // 2
N_ODD = DEPTH // 2
EPS = 1e-6
ROPE_BASE = 10000.0
NEG_INF = -1e30
Q_BLOCK = 128
A_HEADS = 16
A_KV_HEADS = 4
A_HEAD_DIM = 64
A_WIDTH = A_HEADS * A_HEAD_DIM
A_KV_WIDTH = A_KV_HEADS * A_HEAD_DIM
WINDOW = 128
BLOCK = 128
B_HEADS = 4
B_HEAD_DIM = 256
B_WIDTH = B_HEADS * B_HEAD_DIM
B_CHUNK = 64
B_CONV = 3
C_HEADS = 16
C_NOPE = 64
C_ROPE = 32
C_VDIM = 64
C_Q_RANK = 384
C_KV_RANK = 256
C_WIDTH = C_HEADS * C_VDIM
AB_SIZES = [A_WIDTH, A_KV_WIDTH, A_KV_WIDTH, A_WIDTH, B_WIDTH, B_WIDTH, B_WIDTH, B_WIDTH, B_WIDTH, 4 * B_HEADS]
AB_IN = sum(AB_SIZES)
AB_SPLITS = [sum(AB_SIZES[:i + 1]) for i in range(len(AB_SIZES) - 1)]
C_SIZES = [C_Q_RANK, C_KV_RANK, C_ROPE, C_WIDTH]
C_IN = sum(C_SIZES)
C_SPLITS = [sum(C_SIZES[:i + 1]) for i in range(len(C_SIZES) - 1)]

kernel_name = "hybrid_dit_swa_mlstm_mla_step"

F32 = jnp.float32


def _rmsnorm(x, g):
    xf = x.astype(F32)
    y = xf * lax.rsqrt(jnp.mean(xf * xf, axis=-1, keepdims=True) + EPS)
    return (y * g.astype(F32)).astype(x.dtype)


def _adaln(cond, w, b):
    m = jax.nn.silu(cond) @ w + b
    shift, scale, gate = jnp.split(m[:, None, :], 3, axis=-1)
    return shift, scale, gate


def _grid_positions(n_tokens):
    n_rows = n_tokens // GRID_W
    rows = jnp.repeat(jnp.arange(n_rows, dtype=jnp.int32), GRID_W)
    cols = jnp.tile(jnp.arange(GRID_W, dtype=jnp.int32), n_rows)
    return rows, cols


def _rotate(seg, pos):
    half = seg.shape[-1] // 2
    freqs = jnp.power(ROPE_BASE, -jnp.arange(half, dtype=F32) / half)
    ang = pos.astype(F32)[:, None] * freqs[None, :]
    cos = jnp.cos(ang)[None, :, None, :]
    sin = jnp.sin(ang)[None, :, None, :]
    s1, s2 = seg[..., :half], seg[..., half:]
    return jnp.concatenate([s1 * cos - s2 * sin, s2 * cos + s1 * sin], axis=-1)


def _axial_rope(x, rows, cols):
    d_axis = x.shape[-1] // 2
    xf = x.astype(F32)
    out = jnp.concatenate([_rotate(xf[..., :d_axis], rows), _rotate(xf[..., d_axis:], cols)], axis=-1)
    return out.astype(x.dtype)


def _blocked_attn(q, k, v, sink, scale):
    B, Tq, KV, G, dq = q.shape
    nb = Tq // Q_BLOCK
    qb = jnp.moveaxis(q.reshape(B, nb, Q_BLOCK, KV, G, dq), 1, 0)

    def one(qi):
        s = jnp.einsum('bqhgd,bkhd->bhgqk', qi, k, preferred_element_type=F32) * scale
        if sink is not None:
            sk = jnp.broadcast_to(sink.astype(F32)[None, :, :, None, None], s.shape[:-1] + (1,))
            s = jnp.concatenate([s, sk], axis=-1)
        p = jax.nn.softmax(s, axis=-1)
        if sink is not None:
            p = p[..., :-1]
        o = jnp.einsum('bhgqk,bkhd->bqhgd', p.astype(v.dtype), v, preferred_element_type=F32)
        return o.astype(v.dtype)

    out = lax.map(one, qb)
    return jnp.moveaxis(out, 0, 1).reshape(B, Tq, -1)


def _window_attn_latent(q, k, v, ck, cv, sink):
    B, T, H, d = q.shape
    KV = k.shape[2]
    G = H // KV
    L = ck.shape[1]
    nb = T // BLOCK
    scale = d ** -0.5
    pad = ((0, 0), (BLOCK, BLOCK), (0, 0), (0, 0))
    kp = jnp.pad(k, pad)
    vp = jnp.pad(v, pad)
    qb = jnp.moveaxis(q.reshape(B, nb, BLOCK, KV, G, d), 1, 0)
    sink_b = jnp.broadcast_to(sink.astype(F32).reshape(1, KV, G, 1, 1), (B, KV, G, BLOCK, 1))

    def one_block(args):
        qi, i = args
        start = i * BLOCK
        ki = lax.dynamic_slice_in_dim(kp, start, 3 * BLOCK, axis=1)
        vi = lax.dynamic_slice_in_dim(vp, start, 3 * BLOCK, axis=1)
        qpos = start + jnp.arange(BLOCK)
        kpos = start - BLOCK + jnp.arange(3 * BLOCK)
        valid = (jnp.abs(qpos[:, None] - kpos[None, :]) <= WINDOW) & (kpos >= 0)[None, :] & (kpos < T)[None, :]
        s_loc = jnp.einsum('bqhgd,bkhd->bhgqk', qi, ki, preferred_element_type=F32) * scale
        s_loc = jnp.where(valid, s_loc, NEG_INF)
        s_ctx = jnp.einsum('bqhgd,bkhd->bhgqk', qi, ck, preferred_element_type=F32) * scale
        p = jax.nn.softmax(jnp.concatenate([s_loc, s_ctx, sink_b], axis=-1), axis=-1)
        p_loc = p[..., :3 * BLOCK].astype(v.dtype)
        p_ctx = p[..., 3 * BLOCK:3 * BLOCK + L].astype(v.dtype)
        o = (jnp.einsum('bhgqk,bkhd->bqhgd', p_loc, vi, preferred_element_type=F32)
             + jnp.einsum('bhgqk,bkhd->bqhgd', p_ctx, cv, preferred_element_type=F32))
        return o.astype(v.dtype).reshape(B, BLOCK, H * d)

    out = lax.map(one_block, (qb, jnp.arange(nb)))
    return jnp.moveaxis(out, 0, 1).reshape(B, T, H * d)


def _dwconv_centred(x, w):
    K = w.shape[0]
    return lax.conv_general_dilated(x, w[:, None, :].astype(x.dtype), window_strides=(1,),
                                    padding=[(K // 2, K // 2)], dimension_numbers=('NWC', 'WIO', 'NWC'),
                                    feature_group_count=x.shape[-1])


def _mlstm_chunkwise(q, k, v, li, lf, C0, n0, m0):
    B, T, NH, DK = q.shape
    DV = v.shape[-1]
    nc = T // B_CHUNK

    def to_chunks(a):
        return a.astype(F32).reshape(B, nc, B_CHUNK, NH, -1).transpose(1, 0, 3, 2, 4)

    def gate_chunks(a):
        return a.astype(F32).reshape(B, nc, B_CHUNK, NH).transpose(1, 0, 3, 2)

    causal = jnp.tril(jnp.ones((B_CHUNK, B_CHUNK), dtype=bool))

    def step(carry, xs):
        C, n, m = carry
        qc, kc, vc, lic, lfc = xs
        b = jnp.cumsum(lfc, axis=-1)
        log_d = jnp.where(causal, b[..., :, None] - b[..., None, :] + lic[..., None, :], NEG_INF)
        log_init = b + m[..., None]
        m_t = jnp.maximum(log_init, jnp.max(log_d, axis=-1))
        d_mat = jnp.exp(log_d - m_t[..., None])
        w_init = jnp.exp(log_init - m_t)
        s = jnp.einsum('bhtd,bhsd->bhts', qc, kc) * d_mat
        num = w_init[..., None] * jnp.einsum('bhtd,bhde->bhte', qc, C) + jnp.einsum('bhts,bhse->bhte', s, vc)
        den = w_init * jnp.einsum('bhtd,bhd->bht', qc, n) + jnp.sum(s, axis=-1)
        h = num / jnp.maximum(jnp.abs(den), jnp.exp(-m_t))[..., None]
        b_last = b[..., -1]
        log_w = b_last[..., None] - b + lic
        m_new = jnp.maximum(b_last + m, jnp.max(log_w, axis=-1))
        w_s = jnp.exp(log_w - m_new[..., None])
        w_0 = jnp.exp(b_last + m - m_new)
        kw = kc * w_s[..., None]
        C_new = w_0[..., None, None] * C + jnp.einsum('bhsd,bhse->bhde', kw, vc)
        n_new = w_0[..., None] * n + jnp.sum(kw, axis=-2)
        return (C_new, n_new, m_new), h

    init = (C0.astype(F32), n0.astype(F32), m0.astype(F32))
    xs = (to_chunks(q), to_chunks(k), to_chunks(v), gate_chunks(li), gate_chunks(lf))
    (C, n, m), h = lax.scan(step, init, xs)
    h = h.transpose(1, 0, 3, 2, 4).reshape(B, T, NH, DV)
    return h, (C, n, m)


def _mlstm_bidir(q, k, v, o, g, conv_w, norm_w, init):
    B, T, _ = q.shape
    qk = jax.nn.silu(_dwconv_centred(jnp.concatenate([q, k], axis=-1), conv_w))
    q, k = jnp.split(qk, 2, axis=-1)
    q = q.reshape(B, T, B_HEADS, B_HEAD_DIM)
    k = k.reshape(B, T, B_HEADS, B_HEAD_DIM) * (B_HEAD_DIM ** -0.5)
    v = v.reshape(B, T, B_HEADS, B_HEAD_DIM)
    i_f, f_f, i_b, f_b = jnp.split(g.astype(F32), 4, axis=-1)
    if init is None:
        C0 = jnp.zeros((B, 2, B_HEADS, B_HEAD_DIM, B_HEAD_DIM), F32)
        n0 = jnp.zeros((B, 2, B_HEADS, B_HEAD_DIM), F32)
        m0 = jnp.zeros((B, 2, B_HEADS), F32)
    else:
        C0, n0, m0 = init
    h_f, (Cf, nf, mf) = _mlstm_chunkwise(q, k, v, i_f, jax.nn.log_sigmoid(f_f), C0[:, 0], n0[:, 0], m0[:, 0])
    rev = lambda a: jnp.flip(a, axis=1)
    h_b, (Cb, nbk, mb) = _mlstm_chunkwise(rev(q), rev(k), rev(v), rev(i_b), rev(jax.nn.log_sigmoid(f_b)),
                                          C0[:, 1], n0[:, 1], m0[:, 1])
    og = jax.nn.sigmoid(o.astype(F32)).reshape(B, T, B_HEADS, B_HEAD_DIM)
    h = og * (h_f + rev(h_b))
    h = h * lax.rsqrt(jnp.mean(h * h, axis=-1, keepdims=True) + EPS)
    h = (h.reshape(B, T, B_WIDTH) * norm_w.astype(F32)).astype(v.dtype)
    states = (jnp.stack([Cf, Cb], axis=1), jnp.stack([nf, nbk], axis=1), jnp.stack([mf, mb], axis=1))
    return h, states


def _mixer_ab(h, w_in, sink, conv_w, gate_bias, norm_w, w_out, ctx, pos):
    B, T, _ = h.shape
    G = A_HEADS // A_KV_HEADS
    qa, ka, va, za, qb, kb, vb, ob, zb, gb = jnp.split(h @ w_in, AB_SPLITS, axis=-1)
    qa = qa.reshape(B, T, A_HEADS, A_HEAD_DIM)
    ka = ka.reshape(B, T, A_KV_HEADS, A_HEAD_DIM)
    va = va.reshape(B, T, A_KV_HEADS, A_HEAD_DIM)
    if ctx is None:
        attn = _blocked_attn(qa.reshape(B, T, A_KV_HEADS, G, A_HEAD_DIM), ka, va,
                             sink.reshape(A_KV_HEADS, G), A_HEAD_DIM ** -0.5)
        init = None
    else:
        ck, cv, C0, n0, m0 = ctx
        rows, cols = pos
        attn = _window_attn_latent(_axial_rope(qa, rows, cols), _axial_rope(ka, rows, cols), va, ck, cv, sink)
        init = (C0, n0, m0)
    a_out = attn * jax.nn.silu(za)
    hb, states = _mlstm_bidir(qb, kb, vb, ob, gb + gate_bias, conv_w, norm_w, init)
    b_out = hb * jax.nn.silu(zb)
    out = jnp.concatenate([a_out, b_out], axis=-1) @ w_out
    return out, (ka, va) + states


def _mixer_c(h, w_in, q_norm, w_qb, kv_norm, w_kvb, w_out, ctx, pos):
    B, T, _ = h.shape
    qa, kva, kr, z = jnp.split(h @ w_in, C_SPLITS, axis=-1)
    q = (_rmsnorm(qa, q_norm) @ w_qb).reshape(B, T, C_HEADS, C_NOPE + C_ROPE)
    q_nope, q_rope = q[..., :C_NOPE], q[..., C_NOPE:]
    c_kv = _rmsnorm(kva, kv_norm)
    kr = kr[:, :, None, :]
    if ctx is None:
        keys_ckv, keys_kr = c_kv, kr
    else:
        rows, cols = pos
        cc, ckr = ctx
        q_rope = _axial_rope(q_rope, rows, cols)
        keys_ckv = jnp.concatenate([c_kv, cc], axis=1)
        keys_kr = jnp.concatenate([_axial_rope(kr, rows, cols), ckr[:, :, None, :]], axis=1)
    Tk = keys_ckv.shape[1]
    kv = (keys_ckv @ w_kvb).reshape(B, Tk, C_HEADS, C_NOPE + C_VDIM)
    k_nope, v = kv[..., :C_NOPE], kv[..., C_NOPE:]
    k = jnp.concatenate([k_nope, jnp.broadcast_to(keys_kr, (B, Tk, C_HEADS, C_ROPE))], axis=-1)
    qf = jnp.concatenate([q_nope, q_rope], axis=-1)[:, :, :, None, :]
    attn = _blocked_attn(qf, k, v, None, (C_NOPE + C_ROPE) ** -0.5)
    out = (attn * jax.nn.silu(z)) @ w_out
    return out, (c_kv, kr[:, :, 0, :])


def setup_inputs(seed: int = 0) -> dict:
    key = jax.random.key(seed)
    ks = iter(jax.random.split(key, 40))

    def nrm(shape, scale):
        return jax.random.normal(next(ks), shape, F32) * scale

    d_inv = D_MODEL ** -0.5
    gate_offset = jnp.repeat(jnp.array([0.0, 3.0, 0.0, 3.0], F32), B_HEADS)
    return {
        'x_prompt': nrm((BATCH, SEQ, D_MODEL), 1.0),
        'x_sample': nrm((DEC_BATCH, DEC_SEQ, D_MODEL), 1.0),
        'cache_a_k': nrm((DEC_BATCH, N_EVEN, PAST_LEN, A_KV_HEADS, A_HEAD_DIM), 1.0),
        'cache_a_v': nrm((DEC_BATCH, N_EVEN, PAST_LEN, A_KV_HEADS, A_HEAD_DIM), 1.0),
        'state_b_mem': nrm((DEC_BATCH, N_EVEN, 2, B_HEADS, B_HEAD_DIM, B_HEAD_DIM), 0.05),
        'state_b_norm': nrm((DEC_BATCH, N_EVEN, 2, B_HEADS, B_HEAD_DIM), 0.05),
        'state_b_max': nrm((DEC_BATCH, N_EVEN, 2, B_HEADS), 1.0),
        'cache_c_kv': nrm((DEC_BATCH, N_ODD, PAST_LEN, C_KV_RANK), 1.0),
        'cache_c_krope': nrm((DEC_BATCH, N_ODD, PAST_LEN, C_ROPE), 1.0),
        'c': nrm((DEC_BATCH, D_MODEL), 1.0),
        'c_ctx': nrm((D_MODEL,), 1.0),
        'norm_g': 1.0 + nrm((DEPTH, D_MODEL), 0.02),
        'w_mod': nrm((DEPTH, D_MODEL, 3 * D_MODEL), 0.5 * d_inv),
        'b_mod': nrm((DEPTH, 3 * D_MODEL), 0.02),
        'w_in_ab': nrm((N_EVEN, D_MODEL, AB_IN), d_inv),
        'sink_a': nrm((N_EVEN, A_HEADS), 0.5),
        'conv_b': nrm((N_EVEN, B_CONV, 2 * B_WIDTH), B_CONV ** -0.5),
        'gate_bias_b': gate_offset[None, :] + nrm((N_EVEN, 4 * B_HEADS), 0.1),
        'norm_b': 1.0 + nrm((N_EVEN, B_WIDTH), 0.02),
        'w_out_ab': nrm((N_EVEN, A_WIDTH + B_WIDTH, D_MODEL), (A_WIDTH + B_WIDTH) ** -0.5),
        'w_in_c': nrm((N_ODD, D_MODEL, C_IN), d_inv),
        'q_norm_c': 1.0 + nrm((N_ODD, C_Q_RANK), 0.02),
        'w_qb_c': nrm((N_ODD, C_Q_RANK, C_HEADS * (C_NOPE + C_ROPE)), C_Q_RANK ** -0.5),
        'kv_norm_c': 1.0 + nrm((N_ODD, C_KV_RANK), 0.02),
        'w_kvb_c': nrm((N_ODD, C_KV_RANK, C_HEADS * (C_NOPE + C_VDIM)), C_KV_RANK ** -0.5),
        'w_out_c': nrm((N_ODD, C_WIDTH, D_MODEL), C_WIDTH ** -0.5),
        'final_norm': 1.0 + nrm((D_MODEL,), 0.02),
    }


def reference(x_prompt, x_sample, cache_a_k, cache_a_v, state_b_mem, state_b_norm, state_b_max,
              cache_c_kv, cache_c_krope, c, c_ctx, norm_g, w_mod, b_mod, w_in_ab, sink_a, conv_b,
              gate_bias_b, norm_b, w_out_ab, w_in_c, q_norm_c, w_qb_c, kv_norm_c, w_kvb_c, w_out_c,
              final_norm):
    pos = _grid_positions(x_sample.shape[1])
    yp, ys = x_prompt, x_sample
    a_k, a_v, b_mem, b_nrm, b_max, c_kvs, c_krs = [], [], [], [], [], [], []
    for l in range(DEPTH):
        sh_p, sc_p, gt_p = _adaln(c_ctx[None, :], w_mod[l], b_mod[l])
        sh_s, sc_s, gt_s = _adaln(c, w_mod[l], b_mod[l])
        hp = _rmsnorm(yp, norm_g[l]) * (1.0 + sc_p) + sh_p
        hs = _rmsnorm(ys, norm_g[l]) * (1.0 + sc_s) + sh_s
        j = l // 2
        if l % 2 == 0:
            wts = (w_in_ab[j], sink_a[j], conv_b[j], gate_bias_b[j], norm_b[j], w_out_ab[j])
            out_p, (k_ctx, v_ctx, mem, nrm_, mx) = _mixer_ab(hp, *wts, None, None)
            ctx = (cache_a_k[:, j], cache_a_v[:, j], state_b_mem[:, j], state_b_norm[:, j], state_b_max[:, j])
            out_s = _mixer_ab(hs, *wts, ctx, pos)[0]
            a_k.append(k_ctx)
            a_v.append(v_ctx)
            b_mem.append(mem)
            b_nrm.append(nrm_)
            b_max.append(mx)
        else:
            wts = (w_in_c[j], q_norm_c[j], w_qb_c[j], kv_norm_c[j], w_kvb_c[j], w_out_c[j])
            out_p, (ckv, ckr) = _mixer_c(hp, *wts, None, None)
            out_s = _mixer_c(hs, *wts, (cache_c_kv[:, j], cache_c_krope[:, j]), pos)[0]
            c_kvs.append(ckv)
            c_krs.append(ckr)
        yp = yp + gt_p * out_p
        ys = ys + gt_s * out_s
    y_prompt = _rmsnorm(yp, final_norm)
    y_sample = _rmsnorm(ys, final_norm)
    return (y_prompt, y_sample, jnp.stack(a_k, axis=1), jnp.stack(a_v, axis=1), jnp.stack(b_mem, axis=1),
            jnp.stack(b_nrm, axis=1), jnp.stack(b_max, axis=1), jnp.stack(c_kvs, axis=1), jnp.stack(c_krs, axis=1))
```

```python
import functools

import numpy as np
import jax
import jax.numpy as jnp
from jax import lax
from jax.experimental import pallas as pl
from jax.experimental.pallas import tpu as pltpu

F32 = jnp.float32
BF16 = jnp.bfloat16

D_MODEL = 1024
DEPTH = 4
EPS = 1e-6
ROPE_BASE = 10000.0
NEG_INF = -1e30
GRID_W = 64
A_HEADS = 16
A_KV_HEADS = 4
A_GROUP = A_HEADS // A_KV_HEADS
A_HEAD_DIM = 64
A_WIDTH = A_HEADS * A_HEAD_DIM
A_KV_WIDTH = A_KV_HEADS * A_HEAD_DIM
WINDOW = 128
BLOCK = 128
B_HEADS = 4
B_HEAD_DIM = 256
B_WIDTH = B_HEADS * B_HEAD_DIM
B_CHUNK = 256
C_HEADS = 16
C_NOPE = 64
C_ROPE = 32
C_VDIM = 64
C_Q_RANK = 384
C_KV_RANK = 256
C_WIDTH = C_HEADS * C_VDIM
C_HEAD_PAD = 128

LANES = 128
MAIN_QA, MAIN_ZA, MAIN_QB, MAIN_KB, MAIN_VB, MAIN_OB, MAIN_ZB, MAIN_KA, MAIN_VA = (
    0, 1024, 2048, 3072, 4096, 5120, 6144, 7168, 7424)
MAIN_WIDTH = 7680
CIN_QA, CIN_KVA, CIN_Z, CIN_KR, CIN_WIDTH = 0, 384, 640, 1664, 1792

VMEM_LIMIT = 48 * 1024 * 1024

_NT = (((1,), (1,)), ((), ()))
_TN = (((0,), (0,)), ((), ()))


def _cparams(n_axes):
    return pltpu.CompilerParams(dimension_semantics=("arbitrary",) * n_axes,
                                vmem_limit_bytes=VMEM_LIMIT)


def _silu(x):
    return x * jax.nn.sigmoid(x)


def _log_sigmoid(x):
    return jnp.minimum(x, 0.0) - jnp.log1p(jnp.exp(-jnp.abs(x)))


def _rms(x):
    return x * lax.rsqrt(jnp.mean(x * x, axis=-1, keepdims=True) + EPS)


def _norm_modulate(x, g, mod_ref):
    y = _rms(x) * g
    return y * (1.0 + mod_ref[0, 1:2, :]) + mod_ref[0, 0:1, :]


def _swap_halves(x, half):
    lane = lax.broadcasted_iota(jnp.int32, x.shape, 1)
    first = (lane % (2 * half)) < half
    return jnp.where(first, pltpu.roll(x, LANES - half, 1), pltpu.roll(x, half, 1))


def _rope(x, cos, sin, half):
    return x * cos + _swap_halves(x, half) * sin


def _adaln_kernel(c_ref, w_ref, b_ref, o_ref):
    a = _silu(c_ref[...]).astype(BF16)
    w = w_ref[0].astype(BF16)
    o_ref[0] = jnp.dot(a, w, preferred_element_type=F32) + b_ref[0]


def _adaln(cond, w_mod, b_mod):
    tn = 1024
    n = 3 * D_MODEL
    return pl.pallas_call(
        _adaln_kernel,
        grid=(DEPTH, n // tn),
        in_specs=[pl.BlockSpec((16, D_MODEL), lambda l, j: (0, 0)),
                  pl.BlockSpec((1, D_MODEL, tn), lambda l, j: (l, 0, j)),
                  pl.BlockSpec((1, 1, tn), lambda l, j: (l, 0, j))],
        out_specs=pl.BlockSpec((1, 16, tn), lambda l, j: (l, 0, j)),
        out_shape=jax.ShapeDtypeStruct((DEPTH, 16, n), F32),
        compiler_params=_cparams(2),
        name="adaln",
    )(cond, w_mod, b_mod.reshape(DEPTH, 1, n))


def _inproj_ab_kernel(x_ref, mod_ref, g_ref, w_ref, waux_ref, main_ref, aux_ref, hn_ref):
    @pl.when(pl.program_id(1) == 0)
    def _():
        hn = _norm_modulate(x_ref[...], g_ref[...], mod_ref).astype(BF16)
        hn_ref[...] = hn
        aux_ref[...] = jnp.dot(hn, waux_ref[...], preferred_element_type=F32)

    main_ref[...] = jnp.dot(hn_ref[...], w_ref[...], preferred_element_type=F32).astype(BF16)


def _inproj_ab(x, mod3, g, w_main, w_aux, rows_per_mod):
    m = x.shape[0]
    tm, tn = 1024, 1536
    aux_w = w_aux.shape[1]
    if rows_per_mod is None:
        mod_map = lambda i, j: (0, 0, 0)
    else:
        mod_map = lambda i, j: (1 + (i * tm) // rows_per_mod, 0, 0)
    return pl.pallas_call(
        _inproj_ab_kernel,
        grid=(m // tm, MAIN_WIDTH // tn),
        in_specs=[pl.BlockSpec((tm, D_MODEL), lambda i, j: (i, 0)),
                  pl.BlockSpec((1, 3, D_MODEL), mod_map),
                  pl.BlockSpec((1, D_MODEL), lambda i, j: (0, 0)),
                  pl.BlockSpec((D_MODEL, tn), lambda i, j: (0, j)),
                  pl.BlockSpec((D_MODEL, aux_w), lambda i, j: (0, 0))],
        out_specs=[pl.BlockSpec((tm, tn), lambda i, j: (i, j)),
                   pl.BlockSpec((tm, aux_w), lambda i, j: (i, 0))],
        out_shape=[jax.ShapeDtypeStruct((m, MAIN_WIDTH), BF16),
                   jax.ShapeDtypeStruct((m, aux_w), F32)],
        scratch_shapes=[pltpu.VMEM((tm, D_MODEL), BF16)],
        compiler_params=_cparams(2),
        name="inproj_ab",
    )(x, mod3, g, w_main, w_aux)


def _softmax_pv(scores, values, sink):
    m = functools.reduce(jnp.maximum, [jnp.max(s, axis=-1, keepdims=True) for s in scores])
    if sink is not None:
        m = jnp.maximum(m, sink)
    es = [jnp.exp(s - m) for s in scores]
    denom = functools.reduce(jnp.add, [jnp.sum(e, axis=-1, keepdims=True) for e in es])
    if sink is not None:
        denom = denom + jnp.exp(sink - m)
    o = functools.reduce(jnp.add, [jnp.dot(e.astype(BF16), v, preferred_element_type=F32)
                                   for e, v in zip(es, values)])
    return o / denom


def _attn_a_prompt_kernel(sink_ref, q_ref, k_ref, v_ref, o_ref):
    scale = A_HEAD_DIM ** -0.5
    for h in range(A_HEADS):
        g = h // A_GROUP
        ks = slice(g * A_HEAD_DIM, (g + 1) * A_HEAD_DIM)
        hs = slice(h * A_HEAD_DIM, (h + 1) * A_HEAD_DIM)
        qh = q_ref[:, hs] * scale
        s = lax.dot_general(qh, k_ref[:, ks], _NT, preferred_element_type=F32)
        o = _softmax_pv([s], [v_ref[:, ks]], sink_ref[h])
        o_ref[:, hs] = o.astype(o_ref.dtype)


def _attn_a_prompt(main, sink, batch, seq):
    kb = MAIN_KA // A_KV_WIDTH
    vb = MAIN_VA // A_KV_WIDTH
    return pl.pallas_call(
        _attn_a_prompt_kernel,
        grid=(batch,),
        in_specs=[pl.BlockSpec(memory_space=pltpu.SMEM),
                  pl.BlockSpec((seq, A_WIDTH), lambda b: (b, MAIN_QA // A_WIDTH)),
                  pl.BlockSpec((seq, A_KV_WIDTH), lambda b: (b, kb)),
                  pl.BlockSpec((seq, A_KV_WIDTH), lambda b: (b, vb))],
        out_specs=pl.BlockSpec((seq, A_WIDTH), lambda b: (b, 0)),
        out_shape=jax.ShapeDtypeStruct((batch * seq, A_WIDTH), BF16),
        compiler_params=_cparams(1),
        name="attn_a_prompt",
    )(sink, main, main, main)


def _attn_a_sample_kernel(sink_ref, q_ref, k_ref, v_ref, ck_ref, cv_ref, cosq_ref, sinq_ref,
                          cosk_ref, sink_k_ref, o_ref, krope_ref):
    i = pl.program_id(1)
    seq = k_ref.shape[0]
    half = A_HEAD_DIM // 4
    span = 3 * BLOCK

    @pl.when(i == 0)
    def _():
        for c in range(A_KV_WIDTH // LANES):
            cs = slice(c * LANES, (c + 1) * LANES)
            x = k_ref[:, cs].astype(F32)
            krope_ref[:, cs] = _rope(x, cosk_ref[...], sink_k_ref[...], half).astype(BF16)

    start = pl.multiple_of(jnp.clip((i - 1) * BLOCK, 0, seq - span), BLOCK)
    kw = krope_ref[pl.ds(start, span), :]
    vw = v_ref[pl.ds(start, span), :]
    ck = ck_ref[0]
    cv = cv_ref[0]
    rows = A_GROUP * BLOCK
    r = lax.broadcasted_iota(jnp.int32, (rows, span), 0) % BLOCK
    c = lax.broadcasted_iota(jnp.int32, (rows, span), 1)
    valid = jnp.abs(i * BLOCK - start + r - c) <= WINDOW
    scale = A_HEAD_DIM ** -0.5
    cosq = cosq_ref[...]
    sinq = sinq_ref[...]
    head_of_row = lax.broadcasted_iota(jnp.int32, (rows, 1), 0) // BLOCK
    for g in range(A_KV_HEADS):
        ks = slice(g * A_HEAD_DIM, (g + 1) * A_HEAD_DIM)
        parts = []
        sink = jnp.zeros((rows, 1), F32)
        for hh in range(A_GROUP):
            h = g * A_GROUP + hh
            if h % 2 == 0:
                cs = slice((h // 2) * LANES, (h // 2 + 1) * LANES)
                pair = _rope(q_ref[:, cs].astype(F32), cosq, sinq, half) * scale
            parts.append(pair[:, (h % 2) * A_HEAD_DIM:(h % 2 + 1) * A_HEAD_DIM].astype(BF16))
            sink = jnp.where(head_of_row == hh, sink_ref[h], sink)
        qg = jnp.concatenate(parts, axis=0)
        s_loc = lax.dot_general(qg, kw[:, ks], _NT, preferred_element_type=F32)
        s_loc = jnp.where(valid, s_loc, NEG_INF)
        s_ctx = lax.dot_general(qg, ck[:, ks], _NT, preferred_element_type=F32)
        o = _softmax_pv([s_loc, s_ctx], [vw[:, ks], cv[:, ks]], sink)
        for hh in range(A_GROUP):
            h = g * A_GROUP + hh
            o_ref[:, h * A_HEAD_DIM:(h + 1) * A_HEAD_DIM] = (
                o[hh * BLOCK:(hh + 1) * BLOCK].astype(o_ref.dtype))


def _attn_a_sample(main, sink, ck, cv, cos, sin, batch, seq):
    nb = seq // BLOCK
    kb = MAIN_KA // A_KV_WIDTH
    vb = MAIN_VA // A_KV_WIDTH
    ctx = ck.shape[1]
    return pl.pallas_call(
        _attn_a_sample_kernel,
        grid=(batch, nb),
        in_specs=[pl.BlockSpec(memory_space=pltpu.SMEM),
                  pl.BlockSpec((BLOCK, A_WIDTH), lambda b, i: (b * nb + i, MAIN_QA // A_WIDTH)),
                  pl.BlockSpec((seq, A_KV_WIDTH), lambda b, i: (b, kb)),
                  pl.BlockSpec((seq, A_KV_WIDTH), lambda b, i: (b, vb)),
                  pl.BlockSpec((1, ctx, A_KV_WIDTH), lambda b, i: (b, 0, 0)),
                  pl.BlockSpec((1, ctx, A_KV_WIDTH), lambda b, i: (b, 0, 0)),
                  pl.BlockSpec((BLOCK, LANES), lambda b, i: (i, 0)),
                  pl.BlockSpec((BLOCK, LANES), lambda b, i: (i, 0)),
                  pl.BlockSpec((seq, LANES), lambda b, i: (0, 0)),
                  pl.BlockSpec((seq, LANES), lambda b, i: (0, 0))],
        out_specs=pl.BlockSpec((BLOCK, A_WIDTH), lambda b, i: (b * nb + i, 0)),
        out_shape=jax.ShapeDtypeStruct((batch * seq, A_WIDTH), BF16),
        scratch_shapes=[pltpu.VMEM((seq, A_KV_WIDTH), BF16)],
        compiler_params=_cparams(2),
        name="attn_a_sample",
    )(sink, main, main, main, ck, cv, cos, sin, cos, sin)


def _mlstm_kernel(*refs, seq, chunk, has_init, emit_state):
    it = iter(refs)
    bias_ref = next(it)
    m0_ref = next(it) if has_init else None
    q_ref, k_ref, v_ref, o_ref, gcol_ref, grow_ref, cwq_ref, cwk_ref, nw_ref = (next(it) for _ in range(9))
    c0_ref, n0_ref = (next(it), next(it)) if has_init else (None, None)
    h_ref = next(it)
    cst_out, nst_out, mst_out = (next(it), next(it), next(it)) if emit_state else (None, None, None)
    qs_ref, ks_ref, hf_ref, hb_ref, cst_ref, nst_ref = (next(it) for _ in range(6))

    b = pl.program_id(0)
    h = pl.program_id(1)
    nc = seq // chunk

    def conv_silu(x_ref, w_ref, scale):
        x = x_ref[...].astype(F32)
        row = lax.broadcasted_iota(jnp.int32, x.shape, 0)
        prev = jnp.where(row == 0, 0.0, pltpu.roll(x, 1, 0))
        nxt = jnp.where(row == seq - 1, 0.0, pltpu.roll(x, seq - 1, 0))
        y = prev * w_ref[0:1, :] + x * w_ref[1:2, :] + nxt * w_ref[2:3, :]
        return (_silu(y) * scale).astype(BF16)

    qs_ref[...] = conv_silu(q_ref, cwq_ref, 1.0)
    ks_ref[...] = conv_silu(k_ref, cwk_ref, B_HEAD_DIM ** -0.5)

    rr = lax.broadcasted_iota(jnp.int32, (chunk, chunk), 0)
    cc = lax.broadcasted_iota(jnp.int32, (chunk, chunk), 1)

    def chunk_step(c, d, m_prev, first):
        rows = pl.ds(c * chunk, chunk)
        qc = qs_ref[rows, :]
        kc = ks_ref[rows, :]
        vc = v_ref[rows, :]
        gcol = gcol_ref[0, 0, rows, :]
        grow = grow_ref[0, 0, c]
        bi = bias_ref[h, 2 * d]
        bf = bias_ref[h, 2 * d + 1]
        li_col = gcol[:, 2 * d:2 * d + 1] + bi
        lf_col = _log_sigmoid(gcol[:, 2 * d + 1:2 * d + 2] + bf)
        li_row = grow[2 * d:2 * d + 1, :] + bi
        lf_row = _log_sigmoid(grow[2 * d + 1:2 * d + 2, :] + bf)
        causal = (cc <= rr) if d == 0 else (cc >= rr)
        causal_t = (rr <= cc) if d == 0 else (rr >= cc)
        b_col = jnp.sum(jnp.where(causal, lf_row, 0.0), axis=1, keepdims=True)
        b_row = jnp.sum(jnp.where(causal_t, lf_col, 0.0), axis=0, keepdims=True)
        total = jnp.sum(lf_col, axis=0, keepdims=True)
        log_d = jnp.where(causal, b_col - b_row + li_row, NEG_INF)
        log_init = b_col + m_prev
        m_t = jnp.maximum(log_init, jnp.max(log_d, axis=1, keepdims=True))
        d_mat = jnp.exp(log_d - m_t)
        s = lax.dot_general(qc, kc, _NT, preferred_element_type=F32) * d_mat
        num = jnp.dot(s.astype(BF16), vc, preferred_element_type=F32)
        den = jnp.sum(s, axis=1, keepdims=True)
        if not first:
            w_init = jnp.exp(log_init - m_t)
            num = num + w_init * jnp.dot(qc, cst_ref[d].astype(BF16), preferred_element_type=F32)
            den = den + w_init * jnp.sum(qc.astype(F32) * nst_ref[d], axis=1, keepdims=True)
        hc = num / jnp.maximum(jnp.abs(den), jnp.exp(-m_t))
        (hf_ref if d == 0 else hb_ref)[rows, :] = hc
        log_w = total - b_col + li_col
        m_new = jnp.maximum(total + m_prev, jnp.max(log_w, axis=0, keepdims=True))
        kw = kc.astype(F32) * jnp.exp(log_w - m_new)
        c_add = lax.dot_general(kw.astype(BF16), vc, _TN, preferred_element_type=F32)
        n_add = jnp.sum(kw, axis=0, keepdims=True)
        if first:
            cst_ref[d] = c_add
            nst_ref[d] = n_add
        else:
            w_0 = jnp.exp(total + m_prev - m_new)
            cst_ref[d] = w_0 * cst_ref[d] + c_add
            nst_ref[d] = w_0 * nst_ref[d] + n_add
        return m_new

    if has_init:
        for d in range(2):
            cst_ref[d] = c0_ref[0, d, 0]
            nst_ref[d] = n0_ref[0, d, 0]
        m = [jnp.full((1, 1), m0_ref[b, d, h], F32) for d in range(2)]
    else:
        m = [jnp.zeros((1, 1), F32) for _ in range(2)]
    for step in range(nc):
        first = (step == 0) and not has_init
        m[0] = chunk_step(step, 0, m[0], first)
        m[1] = chunk_step(nc - 1 - step, 1, m[1], first)

    og = jax.nn.sigmoid(o_ref[...].astype(F32))
    hh = _rms(og * (hf_ref[...] + hb_ref[...]))
    h_ref[...] = (hh * nw_ref[...]).astype(h_ref.dtype)

    if emit_state:
        for d in range(2):
            cst_out[0, d, 0] = cst_ref[d]
            nst_out[0, d, 0] = nst_ref[d]
        row = lax.broadcasted_iota(jnp.int32, (8, LANES), 0)
        mst_out[0, 0] = jnp.where(row == 0, m[0], m[1])


def _mlstm(main, gates, conv_w, gate_bias, norm_w, init, batch, seq, emit_state):
    chunk = B_CHUNK
    nc = seq // chunk
    hd = B_HEAD_DIM
    g4 = gates.reshape(batch, seq, 4, B_HEADS)
    gcol = g4.transpose(0, 3, 1, 2)
    grow = g4.reshape(batch, nc, chunk, 4, B_HEADS).transpose(0, 4, 1, 3, 2)
    bias = gate_bias.reshape(4, B_HEADS).T
    cw = conv_w
    has_init = init is not None

    def col(off):
        return lambda b, h: (b, off // hd + h)

    in_specs = [pl.BlockSpec(memory_space=pltpu.SMEM)]
    args = [bias]
    if has_init:
        c0, n0, m0 = init
        in_specs.append(pl.BlockSpec(memory_space=pltpu.SMEM))
        args.append(m0)
    in_specs += [pl.BlockSpec((seq, hd), col(MAIN_QB)),
                 pl.BlockSpec((seq, hd), col(MAIN_KB)),
                 pl.BlockSpec((seq, hd), col(MAIN_VB)),
                 pl.BlockSpec((seq, hd), col(MAIN_OB)),
                 pl.BlockSpec((1, 1, seq, 4), lambda b, h: (b, h, 0, 0)),
                 pl.BlockSpec((1, 1, nc, 4, chunk), lambda b, h: (b, h, 0, 0, 0)),
                 pl.BlockSpec((3, hd), lambda b, h: (0, h)),
                 pl.BlockSpec((3, hd), lambda b, h: (0, B_HEADS + h)),
                 pl.BlockSpec((1, hd), lambda b, h: (0, h))]
    args += [main, main, main, main, gcol, grow, cw, cw, norm_w.reshape(1, B_WIDTH)]
    if has_init:
        in_specs += [pl.BlockSpec((1, 2, 1, hd, hd), lambda b, h: (b, 0, h, 0, 0)),
                     pl.BlockSpec((1, 2, 1, 1, hd), lambda b, h: (b, 0, h, 0, 0))]
        args += [c0, n0.reshape(batch, 2, B_HEADS, 1, hd)]
    out_specs = [pl.BlockSpec((seq, hd), lambda b, h: (b, h))]
    out_shape = [jax.ShapeDtypeStruct((batch * seq, B_WIDTH), BF16)]
    if emit_state:
        out_specs += [pl.BlockSpec((1, 2, 1, hd, hd), lambda b, h: (b, 0, h, 0, 0)),
                      pl.BlockSpec((1, 2, 1, 1, hd), lambda b, h: (b, 0, h, 0, 0)),
                      pl.BlockSpec((1, 1, 8, LANES), lambda b, h: (b, h, 0, 0))]
        out_shape += [jax.ShapeDtypeStruct((batch, 2, B_HEADS, hd, hd), F32),
                      jax.ShapeDtypeStruct((batch, 2, B_HEADS, 1, hd), F32),
                      jax.ShapeDtypeStruct((batch, B_HEADS, 8, LANES), F32)]
    outs = pl.pallas_call(
        functools.partial(_mlstm_kernel, seq=seq, chunk=chunk, has_init=has_init, emit_state=emit_state),
        grid=(batch, B_HEADS),
        in_specs=in_specs,
        out_specs=out_specs,
        out_shape=out_shape,
        scratch_shapes=[pltpu.VMEM((seq, hd), BF16), pltpu.VMEM((seq, hd), BF16),
                        pltpu.VMEM((seq, hd), F32), pltpu.VMEM((seq, hd), F32),
                        pltpu.VMEM((2, hd, hd), F32), pltpu.VMEM((2, 1, hd), F32)],
        compiler_params=_cparams(2),
        name="mlstm",
    )(*args)
    if not emit_state:
        return outs[0], None
    hb, cst, nst, mst = outs
    mstate = mst[:, :, 0:2, 0].transpose(0, 2, 1)
    return hb, (cst, nst.reshape(batch, 2, B_HEADS, hd), mstate)


def _outproj_kernel(*refs, n_branch, final):
    it = iter(refs)
    branches = [(next(it), next(it)) for _ in range(n_branch)]
    w_ref, x_ref, mod_ref = next(it), next(it), next(it)
    fn_ref = next(it) if final else None
    y_ref = next(it)
    width = branches[0][0].shape[1]
    out = None
    for n, (a_ref, z_ref) in enumerate(branches):
        gated = (a_ref[...].astype(F32) * _silu(z_ref[...].astype(F32))).astype(BF16)
        part = jnp.dot(gated, w_ref[n * width:(n + 1) * width, :], preferred_element_type=F32)
        out = part if out is None else out + part
    y = x_ref[...] + mod_ref[0, 2:3, :] * out
    if final:
        y = _rms(y) * fn_ref[...]
    y_ref[...] = y


def _outproj(branches, w_out, x, mod3, rows_per_mod, final_norm):
    m = x.shape[0]
    tm = 512
    width = 1024
    if rows_per_mod is None:
        mod_map = lambda i: (0, 0, 0)
    else:
        mod_map = lambda i: (1 + (i * tm) // rows_per_mod, 0, 0)
    in_specs, args = [], []
    for a, a_blk, z, z_blk in branches:
        in_specs += [pl.BlockSpec((tm, width), lambda i, a_blk=a_blk: (i, a_blk)),
                     pl.BlockSpec((tm, width), lambda i, z_blk=z_blk: (i, z_blk))]
        args += [a, z]
    in_specs += [pl.BlockSpec(w_out.shape, lambda i: (0, 0)),
                 pl.BlockSpec((tm, D_MODEL), lambda i: (i, 0)),
                 pl.BlockSpec((1, 3, D_MODEL), mod_map)]
    args += [w_out, x, mod3]
    final = final_norm is not None
    if final:
        in_specs.append(pl.BlockSpec((1, D_MODEL), lambda i: (0, 0)))
        args.append(final_norm.reshape(1, D_MODEL))
    return pl.pallas_call(
        functools.partial(_outproj_kernel, n_branch=len(branches), final=final),
        grid=(m // tm,),
        in_specs=in_specs,
        out_specs=pl.BlockSpec((tm, D_MODEL), lambda i: (i, 0)),
        out_shape=jax.ShapeDtypeStruct((m, D_MODEL), F32),
        compiler_params=_cparams(1),
        name="outproj",
    )(*args)


def _inproj_c_kernel(*refs, rope):
    it = iter(refs)
    x_ref, mod_ref, g_ref, w_ref, qn_ref, wqb_ref, kvn_ref, wkvb_ref = (next(it) for _ in range(8))
    cos_ref, sin_ref = (next(it), next(it)) if rope else (None, None)
    q_ref, kv_ref, z_ref, ckv_ref, kr_ref, krb_ref = (next(it) for _ in range(6))

    hn = _norm_modulate(x_ref[...], g_ref[...], mod_ref).astype(BF16)
    r = jnp.dot(hn, w_ref[...], preferred_element_type=F32)
    qa = _rms(r[:, CIN_QA:CIN_KVA]) * qn_ref[...]
    ckv = _rms(r[:, CIN_KVA:CIN_Z]) * kvn_ref[...]
    z_ref[...] = r[:, CIN_Z:CIN_KR].astype(BF16)
    kr = r[:, CIN_KR:CIN_WIDTH]
    ckv_ref[...] = ckv
    kr_ref[...] = kr
    kv_ref[...] = jnp.dot(ckv.astype(BF16), wkvb_ref[...], preferred_element_type=F32).astype(BF16)
    scale = (C_NOPE + C_ROPE) ** -0.5
    q = jnp.dot(qa.astype(BF16), wqb_ref[...], preferred_element_type=F32) * scale
    half = C_ROPE // 4
    if rope:
        cos = cos_ref[...]
        sin = sin_ref[...]
        kr = _rope(kr, cos, sin, half)
        for hd in range(C_HEADS):
            hs = slice(hd * C_HEAD_PAD, (hd + 1) * C_HEAD_PAD)
            q_ref[:, hs] = _rope(q[:, hs], cos, sin, half).astype(BF16)
    else:
        q_ref[...] = q.astype(BF16)
    krb_ref[...] = kr.astype(BF16)


def _inproj_c(x, mod3, g, w_in, q_norm, w_qb, kv_norm, w_kvb, rows_per_mod, tables):
    m = x.shape[0]
    tm = 512
    rope = tables is not None
    if rows_per_mod is None:
        mod_map = lambda i: (0, 0, 0)
    else:
        mod_map = lambda i: (1 + (i * tm) // rows_per_mod, 0, 0)
    const = lambda i: (0, 0)
    qw = C_HEADS * C_HEAD_PAD
    kvw = C_HEADS * (C_NOPE + C_VDIM)
    in_specs = [pl.BlockSpec((tm, D_MODEL), lambda i: (i, 0)),
                pl.BlockSpec((1, 3, D_MODEL), mod_map),
                pl.BlockSpec((1, D_MODEL), const),
                pl.BlockSpec((D_MODEL, CIN_WIDTH), const),
                pl.BlockSpec((1, C_Q_RANK), const),
                pl.BlockSpec((C_Q_RANK, qw), const),
                pl.BlockSpec((1, C_KV_RANK), const),
                pl.BlockSpec((C_KV_RANK, kvw), const)]
    args = [x, mod3, g, w_in, q_norm, w_qb, kv_norm, w_kvb]
    if rope:
        tpb = rows_per_mod // tm
        in_specs += [pl.BlockSpec((tm, LANES), lambda i: (i % tpb, 0))] * 2
        args += list(tables)
    return pl.pallas_call(
        functools.partial(_inproj_c_kernel, rope=rope),
        grid=(m // tm,),
        in_specs=in_specs,
        out_specs=[pl.BlockSpec((tm, qw), lambda i: (i, 0)),
                   pl.BlockSpec((tm, kvw), lambda i: (i, 0)),
                   pl.BlockSpec((tm, C_WIDTH), lambda i: (i, 0)),
                   pl.BlockSpec((tm, C_KV_RANK), lambda i: (i, 0)),
                   pl.BlockSpec((tm, LANES), lambda i: (i, 0)),
                   pl.BlockSpec((tm, LANES), lambda i: (i, 0))],
        out_shape=[jax.ShapeDtypeStruct((m, qw), BF16),
                   jax.ShapeDtypeStruct((m, kvw), BF16),
                   jax.ShapeDtypeStruct((m, C_WIDTH), BF16),
                   jax.ShapeDtypeStruct((m, C_KV_RANK), F32),
                   jax.ShapeDtypeStruct((m, LANES), F32),
                   jax.ShapeDtypeStruct((m, LANES), BF16)],
        compiler_params=_cparams(1),
        name="inproj_c",
    )(*args)


def _matmul_kernel(x_ref, w_ref, o_ref):
    o_ref[...] = jnp.dot(x_ref[...], w_ref[...], preferred_element_type=F32).astype(o_ref.dtype)


def _matmul(x, w, tm):
    m, k = x.shape
    n = w.shape[1]
    return pl.pallas_call(
        _matmul_kernel,
        grid=(m // tm,),
        in_specs=[pl.BlockSpec((tm, k), lambda i: (i, 0)),
                  pl.BlockSpec((k, n), lambda i: (0, 0))],
        out_specs=pl.BlockSpec((tm, n), lambda i: (i, 0)),
        out_shape=jax.ShapeDtypeStruct((m, n), BF16),
        compiler_params=_cparams(1),
        name="matmul",
    )(x, w)


def _mla_keys(kv, kr_slab):
    lane = lax.broadcasted_iota(jnp.int32, kv.shape, 1)
    return jnp.where(lane < C_NOPE, kv, kr_slab)


def _attn_c_kernel(*refs, has_ctx):
    it = iter(refs)
    q_ref, kv_ref, kr_ref = next(it), next(it), next(it)
    kvc_ref, krc_ref = (next(it), next(it)) if has_ctx else (None, None)
    o_ref = next(it)
    kown_ref = next(it)
    kctx_ref = next(it) if has_ctx else None

    @pl.when(pl.program_id(1) == 0)
    def _():
        for h in range(C_HEADS):
            hs = slice(h * C_HEAD_PAD, (h + 1) * C_HEAD_PAD)
            kown_ref[:, hs] = _mla_keys(kv_ref[:, hs], kr_ref[...])
            if has_ctx:
                kctx_ref[:, hs] = _mla_keys(kvc_ref[:, hs], krc_ref[...])

    for h in range(C_HEADS):
        hs = slice(h * C_HEAD_PAD, (h + 1) * C_HEAD_PAD)
        qh = q_ref[:, hs]
        scores = [lax.dot_general(qh, kown_ref[:, hs], _NT, preferred_element_type=F32)]
        values = [kv_ref[:, hs]]
        if has_ctx:
            scores.append(lax.dot_general(qh, kctx_ref[:, hs], _NT, preferred_element_type=F32))
            values.append(kvc_ref[:, hs])
        o = _softmax_pv(scores, values, None)
        o_ref[:, h * C_VDIM:(h + 1) * C_VDIM] = o[:, C_NOPE:].astype(o_ref.dtype)


def _attn_c(q, kv, kr_slab, ctx, batch, seq, tq):
    nq = seq // tq
    w = C_HEADS * C_HEAD_PAD
    has_ctx = ctx is not None
    in_specs = [pl.BlockSpec((tq, w), lambda b, i: (b * nq + i, 0)),
                pl.BlockSpec((seq, w), lambda b, i: (b, 0)),
                pl.BlockSpec((seq, LANES), lambda b, i: (b, 0))]
    args = [q, kv, kr_slab]
    scratch = [pltpu.VMEM((seq, w), BF16)]
    if has_ctx:
        kv_ctx, kr_ctx = ctx
        nctx = kv_ctx.shape[0] // batch
        in_specs += [pl.BlockSpec((nctx, w), lambda b, i: (b, 0)),
                     pl.BlockSpec((nctx, LANES), lambda b, i: (b, 0))]
        args += [kv_ctx, kr_ctx]
        scratch.append(pltpu.VMEM((nctx, w), BF16))
    return pl.pallas_call(
        functools.partial(_attn_c_kernel, has_ctx=has_ctx),
        grid=(batch, nq),
        in_specs=in_specs,
        out_specs=pl.BlockSpec((tq, C_WIDTH), lambda b, i: (b * nq + i, 0)),
        out_shape=jax.ShapeDtypeStruct((batch * seq, C_WIDTH), BF16),
        scratch_shapes=scratch,
        compiler_params=_cparams(2),
        name="attn_c",
    )(*args)


def _rope_tables(n_tokens):
    pos_r = np.repeat(np.arange(n_tokens // GRID_W), GRID_W).astype(np.float64)
    pos_c = np.tile(np.arange(GRID_W), n_tokens // GRID_W).astype(np.float64)

    def seg(d_axis):
        half = d_axis // 2
        freqs = np.power(ROPE_BASE, -np.arange(half, dtype=np.float64) / half)
        cos, sin = [], []
        for pos in (pos_r, pos_c):
            ang = pos[:, None] * freqs[None, :]
            cos += [np.cos(ang), np.cos(ang)]
            sin += [-np.sin(ang), np.sin(ang)]
        return np.concatenate(cos, axis=1), np.concatenate(sin, axis=1)

    cos_a, sin_a = seg(A_HEAD_DIM // 2)
    cos_a, sin_a = np.tile(cos_a, (1, 2)), np.tile(sin_a, (1, 2))
    cos_r, sin_r = seg(C_ROPE // 2)
    ones = np.ones((n_tokens, C_NOPE))
    pad = C_HEAD_PAD - C_NOPE - C_ROPE
    cos_c = np.concatenate([ones, cos_r, np.ones((n_tokens, pad))], axis=1)
    sin_c = np.concatenate([0 * ones, sin_r, np.zeros((n_tokens, pad))], axis=1)
    f = lambda a: jnp.asarray(a, dtype=F32)
    return (f(cos_a), f(sin_a)), (f(cos_c), f(sin_c))


def _pad_cols(w, left, total):
    return jnp.pad(w, ((0, 0), (left, total - left - w.shape[1])))


def kernel(x_prompt, x_sample, cache_a_k, cache_a_v, state_b_mem, state_b_norm, state_b_max, cache_c_kv, cache_c_krope, c, c_ctx, norm_g, w_mod, b_mod, w_in_ab, sink_a, conv_b, gate_bias_b, norm_b, w_out_ab, w_in_c, q_norm_c, w_qb_c, kv_norm_c, w_kvb_c, w_out_c, final_norm):
    bp, tp, _ = x_prompt.shape
    bs, ts, _ = x_sample.shape
    past = cache_a_k.shape[2]
    tables_a, tables_c = _rope_tables(ts)

    cond = jnp.zeros((16, D_MODEL), F32).at[0].set(c_ctx).at[1:1 + bs].set(c)
    mods = _adaln(cond, w_mod, b_mod).reshape(DEPTH, 16, 3, D_MODEL)

    yp = x_prompt.reshape(bp * tp, D_MODEL)
    ys = x_sample.reshape(bs * ts, D_MODEL)
    a_k, a_v, b_mem, b_nrm, b_max, c_kvs, c_krs = [], [], [], [], [], [], []
    for l in range(DEPTH):
        j = l // 2
        mod3 = mods[l]
        g = norm_g[l].reshape(1, D_MODEL)
        last = l == DEPTH - 1
        fin = final_norm if last else None
        if l % 2 == 0:
            w = w_in_ab[j]
            w_main = jnp.concatenate([w[:, 0:1024], w[:, 1536:2560], w[:, 2560:7680], w[:, 1024:1536]],
                                     axis=1).astype(BF16)
            w_gate = _pad_cols(w[:, 7680:], 0, LANES)
            w_aux_p = jnp.concatenate([w[:, 1024:1536], w_gate], axis=1).astype(BF16)
            w_aux_s = w_gate.astype(BF16)
            w_out = w_out_ab[j].astype(BF16)

            main_p, aux_p = _inproj_ab(yp, mod3, g, w_main, w_aux_p, None)
            main_s, aux_s = _inproj_ab(ys, mod3, g, w_main, w_aux_s, ts)
            a_k.append(aux_p[:, 0:A_KV_WIDTH].reshape(bp, tp, A_KV_HEADS, A_HEAD_DIM))
            a_v.append(aux_p[:, A_KV_WIDTH:2 * A_KV_WIDTH].reshape(bp, tp, A_KV_HEADS, A_HEAD_DIM))
            gates_p = aux_p[:, 2 * A_KV_WIDTH:2 * A_KV_WIDTH + 4 * B_HEADS]
            gates_s = aux_s[:, 0:4 * B_HEADS]

            attn_p = _attn_a_prompt(main_p, sink_a[j], bp, tp)
            ck = cache_a_k[:, j].reshape(bs, past, A_KV_WIDTH).astype(BF16)
            cv = cache_a_v[:, j].reshape(bs, past, A_KV_WIDTH).astype(BF16)
            attn_s = _attn_a_sample(main_s, sink_a[j], ck, cv, *tables_a, bs, ts)

            hb_p, states = _mlstm(main_p, gates_p, conv_b[j], gate_bias_b[j], norm_b[j], None, bp, tp, True)
            init = (state_b_mem[:, j], state_b_norm[:, j], state_b_max[:, j])
            hb_s, _ = _mlstm(main_s, gates_s, conv_b[j], gate_bias_b[j], norm_b[j], init, bs, ts, False)
            b_mem.append(states[0])
            b_nrm.append(states[1])
            b_max.append(states[2])

            za, zb = MAIN_ZA // 1024, MAIN_ZB // 1024
            yp = _outproj([(attn_p, 0, main_p, za), (hb_p, 0, main_p, zb)], w_out, yp, mod3, None, fin)
            ys = _outproj([(attn_s, 0, main_s, za), (hb_s, 0, main_s, zb)], w_out, ys, mod3, ts, fin)
        else:
            w = w_in_c[j]
            o_kva = C_Q_RANK
            o_kr = C_Q_RANK + C_KV_RANK
            o_z = o_kr + C_ROPE
            w_in = jnp.concatenate([w[:, :o_kva], w[:, o_kva:o_kr], w[:, o_z:],
                                    _pad_cols(w[:, o_kr:o_z], C_NOPE, C_HEAD_PAD)], axis=1).astype(BF16)
            wq = w_qb_c[j].reshape(C_Q_RANK, C_HEADS, C_NOPE + C_ROPE)
            wq = jnp.pad(wq, ((0, 0), (0, 0), (0, C_HEAD_PAD - C_NOPE - C_ROPE)))
            wq = wq.reshape(C_Q_RANK, C_HEADS * C_HEAD_PAD).astype(BF16)
            wkv = w_kvb_c[j].astype(BF16)
            w_out = w_out_c[j].astype(BF16)
            qn = q_norm_c[j].reshape(1, C_Q_RANK)
            kvn = kv_norm_c[j].reshape(1, C_KV_RANK)

            q_p, kv_p, z_p, ckv_p, kr_p, krb_p = _inproj_c(yp, mod3, g, w_in, qn, wq, kvn, wkv, None, None)
            q_s, kv_s, z_s, _, _, krb_s = _inproj_c(ys, mod3, g, w_in, qn, wq, kvn, wkv, ts, tables_c)
            c_kvs.append(ckv_p.reshape(bp, tp, C_KV_RANK))
            c_krs.append(kr_p[:, C_NOPE:C_NOPE + C_ROPE].reshape(bp, tp, C_ROPE))

            cc = cache_c_kv[:, j].reshape(bs * past, C_KV_RANK).astype(BF16)
            kv_ctx = _matmul(cc, wkv, 512)
            kr_ctx = _pad_cols(cache_c_krope[:, j].reshape(bs * past, C_ROPE), C_NOPE, C_HEAD_PAD).astype(BF16)

            attn_p = _attn_c(q_p, kv_p, krb_p, None, bp, tp, tp)
            attn_s = _attn_c(q_s, kv_s, krb_s, (kv_ctx, kr_ctx), bs, ts, 256)

            yp = _outproj([(attn_p, 0, z_p, 0)], w_out, yp, mod3, None, fin)
            ys = _outproj([(attn_s, 0, z_s, 0)], w_out, ys, mod3, ts, fin)

    return (yp.reshape(bp, tp, D_MODEL), ys.reshape(bs, ts, D_MODEL),
            jnp.stack(a_k, axis=1), jnp.stack(a_v, axis=1), jnp.stack(b_mem, axis=1),
            jnp.stack(b_nrm, axis=1), jnp.stack(b_max, axis=1), jnp.stack(c_kvs, axis=1),
            jnp.stack(c_krs, axis=1))
```

```python
import functools
import math

import numpy as np
import jax
import jax.numpy as jnp
from jax import lax
from jax.experimental import pallas as pl
from jax.experimental.pallas import tpu as pltpu

F32 = jnp.float32
BF16 = jnp.bfloat16

D_MODEL = 1024
DEPTH = 4
N_EVEN = 2
EPS = 1e-6
ROPE_BASE = 10000.0
NEG_INF = -1e30
GRID_W = 64
LOG2E = math.log2(math.e)
A_HEADS = 16
A_KV_HEADS = 4
A_GROUP = A_HEADS // A_KV_HEADS
A_HEAD_DIM = 64
A_WIDTH = A_HEADS * A_HEAD_DIM
A_KV_WIDTH = A_KV_HEADS * A_HEAD_DIM
WINDOW = 128
BLOCK = 128
B_HEADS = 4
B_HEAD_DIM = 256
B_WIDTH = B_HEADS * B_HEAD_DIM
B_CHUNK = 256
C_HEADS = 16
C_NOPE = 64
C_ROPE = 32
C_VDIM = 64
C_Q_RANK = 384
C_KV_RANK = 256
C_WIDTH = C_HEADS * C_VDIM
C_HEAD_PAD = 128

LANES = 128
MAIN_QA, MAIN_KA, MAIN_VA, MAIN_ZA, MAIN_QB, MAIN_KB, MAIN_VB, MAIN_OB, MAIN_ZB = (
    0, 1024, 1280, 1536, 2560, 3584, 4608, 5632, 6656)
MAIN_WIDTH = 7680
CIN_QA, CIN_KVA, CIN_Z, CIN_KR, CIN_WIDTH = 0, 384, 640, 1664, 1792
Z_BLOCK = 512

MIB = 1024 * 1024
VMEM_LIMIT = 48 * MIB
VMEM_LIMIT_WIDE = 56 * MIB

_NT = (((1,), (1,)), ((), ()))
_TN = (((0,), (0,)), ((), ()))


def _cparams(n_axes, vmem=VMEM_LIMIT):
    return pltpu.CompilerParams(dimension_semantics=("arbitrary",) * n_axes,
                                vmem_limit_bytes=vmem)


def _silu(x):
    return x * jax.nn.sigmoid(x)


def _log_sigmoid(x):
    return jnp.minimum(x, 0.0) - jnp.log1p(jnp.exp(-jnp.abs(x)))


def _rms(x):
    return x * lax.rsqrt(jnp.mean(x * x, axis=-1, keepdims=True) + EPS)


def _norm_modulate(x, g, mod_ref):
    y = _rms(x) * g
    return y * (1.0 + mod_ref[0, 1:2, :]) + mod_ref[0, 0:1, :]


def _swap_halves(x, half):
    lane = lax.broadcasted_iota(jnp.int32, x.shape, 1)
    first = (lane % (2 * half)) < half
    return jnp.where(first, pltpu.roll(x, LANES - half, 1), pltpu.roll(x, half, 1))


def _rope(x, cos, sin, half):
    return x * cos + _swap_halves(x, half) * sin


def _softmax_pv(scores, values, sink):
    m = functools.reduce(jnp.maximum, [jnp.max(s, axis=-1, keepdims=True) for s in scores])
    if sink is not None:
        m = jnp.maximum(m, sink)
    res = functools.reduce(jnp.add, [
        jnp.dot(jnp.exp2(s - m).astype(BF16), v, preferred_element_type=F32)
        for s, v in zip(scores, values)])
    den = pltpu.roll(res, LANES // 2, 1)
    if sink is not None:
        den = den + jnp.exp2(sink - m)
    return res / den


def _adaln_kernel(c_ref, w_ref, b_ref, o_ref):
    a = _silu(c_ref[...]).astype(BF16)
    w = w_ref[0].astype(BF16)
    o_ref[0] = jnp.dot(a, w, preferred_element_type=F32) + b_ref[0]


def _adaln(cond, w_mod, b_mod):
    tn = 1024
    n = 3 * D_MODEL
    return pl.pallas_call(
        _adaln_kernel,
        grid=(DEPTH, n // tn),
        in_specs=[pl.BlockSpec((16, D_MODEL), lambda l, j: (0, 0)),
                  pl.BlockSpec((1, D_MODEL, tn), lambda l, j: (l, 0, j)),
                  pl.BlockSpec((1, 1, tn), lambda l, j: (l, 0, j))],
        out_specs=pl.BlockSpec((1, 16, tn), lambda l, j: (l, 0, j)),
        out_shape=jax.ShapeDtypeStruct((DEPTH, 16, n), F32),
        compiler_params=_cparams(2),
        name="adaln",
    )(cond, w_mod, b_mod.reshape(DEPTH, 1, n))


def _inproj_ab_kernel(*refs, emit_kv):
    it = iter(refs)
    x_ref, mod_ref, g_ref, w_ref = (next(it) for _ in range(4))
    wkv_ref = next(it) if emit_kv else None
    wg_ref = next(it)
    main_ref = next(it)
    k_ref, v_ref = (next(it), next(it)) if emit_kv else (None, None)
    gate_ref, hn_ref = next(it), next(it)

    @pl.when(pl.program_id(1) == 0)
    def _():
        hn = _norm_modulate(x_ref[...], g_ref[...], mod_ref).astype(BF16)
        hn_ref[...] = hn
        gate_ref[...] = jnp.dot(hn, wg_ref[...], preferred_element_type=F32)
        if emit_kv:
            kv = jnp.dot(hn, wkv_ref[0].astype(BF16), preferred_element_type=F32)
            k_ref[...] = kv[:, :A_KV_WIDTH]
            v_ref[...] = kv[:, A_KV_WIDTH:]

    main_ref[...] = jnp.dot(hn_ref[...], w_ref[0].astype(BF16),
                            preferred_element_type=F32).astype(BF16)


def _inproj_ab(x, mod3, g, w_in_ab, layer, w_gate, rows_per_mod, emit_kv):
    m = x.shape[0]
    tm, tn = 1024, 1280
    if rows_per_mod is None:
        mod_map = lambda i, j: (0, 0, 0)
    else:
        mod_map = lambda i, j: (1 + (i * tm) // rows_per_mod, 0, 0)
    kv_w = 2 * A_KV_WIDTH
    in_specs = [pl.BlockSpec((tm, D_MODEL), lambda i, j: (i, 0)),
                pl.BlockSpec((1, 3, D_MODEL), mod_map),
                pl.BlockSpec((1, D_MODEL), lambda i, j: (0, 0)),
                pl.BlockSpec((1, D_MODEL, tn), lambda i, j: (layer, 0, j))]
    args = [x, mod3, g, w_in_ab]
    out_specs = [pl.BlockSpec((tm, tn), lambda i, j: (i, j))]
    out_shape = [jax.ShapeDtypeStruct((m, MAIN_WIDTH), BF16)]
    if emit_kv:
        in_specs.append(pl.BlockSpec((1, D_MODEL, kv_w), lambda i, j: (layer, 0, MAIN_KA // kv_w)))
        args.append(w_in_ab)
        out_specs += [pl.BlockSpec((tm, A_KV_WIDTH), lambda i, j: (i, 0))] * 2
        out_shape += [jax.ShapeDtypeStruct((m, A_KV_WIDTH), F32)] * 2
    in_specs.append(pl.BlockSpec((D_MODEL, LANES), lambda i, j: (0, 0)))
    args.append(w_gate)
    out_specs.append(pl.BlockSpec((tm, LANES), lambda i, j: (i, 0)))
    out_shape.append(jax.ShapeDtypeStruct((m, LANES), F32))
    return pl.pallas_call(
        functools.partial(_inproj_ab_kernel, emit_kv=emit_kv),
        grid=(m // tm, MAIN_WIDTH // tn),
        in_specs=in_specs,
        out_specs=out_specs,
        out_shape=out_shape,
        scratch_shapes=[pltpu.VMEM((tm, D_MODEL), BF16)],
        compiler_params=_cparams(2, VMEM_LIMIT_WIDE),
        name="inproj_ab",
    )(*args)


def _value_slabs(v_ref, slab_ref):
    keys = v_ref.shape[0]
    ones = jnp.ones((keys, LANES - A_HEAD_DIM), BF16)
    for g in range(A_KV_HEADS):
        slab_ref[:, g * LANES:g * LANES + A_HEAD_DIM] = v_ref[:, g * A_HEAD_DIM:(g + 1) * A_HEAD_DIM]
        slab_ref[:, g * LANES + A_HEAD_DIM:(g + 1) * LANES] = ones


def _attn_a_prompt_kernel(sink_ref, q_ref, k_ref, v_ref, o_ref, vs_ref):
    scale = A_HEAD_DIM ** -0.5 * LOG2E
    _value_slabs(v_ref, vs_ref)
    for h in range(A_HEADS):
        g = h // A_GROUP
        ks = slice(g * A_HEAD_DIM, (g + 1) * A_HEAD_DIM)
        hs = slice(h * A_HEAD_DIM, (h + 1) * A_HEAD_DIM)
        qh = (q_ref[:, hs].astype(F32) * scale).astype(BF16)
        s = lax.dot_general(qh, k_ref[:, ks], _NT, preferred_element_type=F32)
        o = _softmax_pv([s], [vs_ref[:, g * LANES:(g + 1) * LANES]], sink_ref[h] * LOG2E)
        o_ref[:, hs] = o[:, :A_HEAD_DIM].astype(o_ref.dtype)


def _attn_a_prompt(main, sink, batch, seq):
    kb = MAIN_KA // A_KV_WIDTH
    vb = MAIN_VA // A_KV_WIDTH
    return pl.pallas_call(
        _attn_a_prompt_kernel,
        grid=(batch,),
        in_specs=[pl.BlockSpec(memory_space=pltpu.SMEM),
                  pl.BlockSpec((seq, A_WIDTH), lambda b: (b, MAIN_QA // A_WIDTH)),
                  pl.BlockSpec((seq, A_KV_WIDTH), lambda b: (b, kb)),
                  pl.BlockSpec((seq, A_KV_WIDTH), lambda b: (b, vb))],
        out_specs=pl.BlockSpec((seq, A_WIDTH), lambda b: (b, 0)),
        out_shape=jax.ShapeDtypeStruct((batch * seq, A_WIDTH), BF16),
        scratch_shapes=[pltpu.VMEM((seq, A_KV_HEADS * LANES), BF16)],
        compiler_params=_cparams(1),
        name="attn_a_prompt",
    )(sink, main, main, main)


def _attn_a_sample_kernel(sink_ref, q_ref, k_ref, v_ref, ck_ref, cv_ref, cosq_ref, sinq_ref,
                          cosk_ref, sink_k_ref, o_ref, krope_ref, vs_ref, cvs_ref):
    i = pl.program_id(1)
    seq = k_ref.shape[0]
    half = A_HEAD_DIM // 4
    span = 3 * BLOCK

    @pl.when(i == 0)
    def _():
        for c in range(A_KV_WIDTH // LANES):
            cs = slice(c * LANES, (c + 1) * LANES)
            x = k_ref[:, cs].astype(F32)
            krope_ref[:, cs] = _rope(x, cosk_ref[...], sink_k_ref[...], half).astype(BF16)
        _value_slabs(v_ref, vs_ref)
        _value_slabs(cv_ref.at[0], cvs_ref)

    start = pl.multiple_of(jnp.clip((i - 1) * BLOCK, 0, seq - span), BLOCK)
    kw = krope_ref[pl.ds(start, span), :]
    vw = vs_ref[pl.ds(start, span), :]
    ck = ck_ref[0]
    rows = A_GROUP * BLOCK
    r = lax.broadcasted_iota(jnp.int32, (rows, span), 0) % BLOCK
    c = lax.broadcasted_iota(jnp.int32, (rows, span), 1)
    valid = jnp.abs(i * BLOCK - start + r - c) <= WINDOW
    scale = A_HEAD_DIM ** -0.5 * LOG2E
    cosq = cosq_ref[...]
    sinq = sinq_ref[...]
    head_of_row = lax.broadcasted_iota(jnp.int32, (rows, 1), 0) // BLOCK
    for g in range(A_KV_HEADS):
        ks = slice(g * A_HEAD_DIM, (g + 1) * A_HEAD_DIM)
        gs = slice(g * LANES, (g + 1) * LANES)
        parts = []
        sink = jnp.zeros((rows, 1), F32)
        for hh in range(A_GROUP):
            h = g * A_GROUP + hh
            if h % 2 == 0:
                cs = slice((h // 2) * LANES, (h // 2 + 1) * LANES)
                pair = _rope(q_ref[:, cs].astype(F32), cosq, sinq, half) * scale
            parts.append(pair[:, (h % 2) * A_HEAD_DIM:(h % 2 + 1) * A_HEAD_DIM].astype(BF16))
            sink = jnp.where(head_of_row == hh, sink_ref[h] * LOG2E, sink)
        qg = jnp.concatenate(parts, axis=0)
        s_loc = lax.dot_general(qg, kw[:, ks], _NT, preferred_element_type=F32)
        s_loc = jnp.where(valid, s_loc, NEG_INF)
        s_ctx = lax.dot_general(qg, ck[:, ks], _NT, preferred_element_type=F32)
        o = _softmax_pv([s_loc, s_ctx], [vw[:, gs], cvs_ref[:, gs]], sink)
        for hh in range(A_GROUP):
            h = g * A_GROUP + hh
            o_ref[:, h * A_HEAD_DIM:(h + 1) * A_HEAD_DIM] = (
                o[hh * BLOCK:(hh + 1) * BLOCK, :A_HEAD_DIM].astype(o_ref.dtype))


def _attn_a_sample(main, sink, ck, cv, cos, sin, batch, seq):
    nb = seq // BLOCK
    kb = MAIN_KA // A_KV_WIDTH
    vb = MAIN_VA // A_KV_WIDTH
    ctx = ck.shape[1]
    slab_w = A_KV_HEADS * LANES
    return pl.pallas_call(
        _attn_a_sample_kernel,
        grid=(batch, nb),
        in_specs=[pl.BlockSpec(memory_space=pltpu.SMEM),
                  pl.BlockSpec((BLOCK, A_WIDTH), lambda b, i: (b * nb + i, MAIN_QA // A_WIDTH)),
                  pl.BlockSpec((seq, A_KV_WIDTH), lambda b, i: (b, kb)),
                  pl.BlockSpec((seq, A_KV_WIDTH), lambda b, i: (b, vb)),
                  pl.BlockSpec((1, ctx, A_KV_WIDTH), lambda b, i: (b, 0, 0)),
                  pl.BlockSpec((1, ctx, A_KV_WIDTH), lambda b, i: (b, 0, 0)),
                  pl.BlockSpec((BLOCK, LANES), lambda b, i: (i, 0)),
                  pl.BlockSpec((BLOCK, LANES), lambda b, i: (i, 0)),
                  pl.BlockSpec((seq, LANES), lambda b, i: (0, 0)),
                  pl.BlockSpec((seq, LANES), lambda b, i: (0, 0))],
        out_specs=pl.BlockSpec((BLOCK, A_WIDTH), lambda b, i: (b * nb + i, 0)),
        out_shape=jax.ShapeDtypeStruct((batch * seq, A_WIDTH), BF16),
        scratch_shapes=[pltpu.VMEM((seq, A_KV_WIDTH), BF16),
                        pltpu.VMEM((seq, slab_w), BF16),
                        pltpu.VMEM((ctx, slab_w), BF16)],
        compiler_params=_cparams(2),
        name="attn_a_sample",
    )(sink, main, main, main, ck, cv, cos, sin, cos, sin)


def _mlstm_kernel(*refs, seq, chunk, layer, has_init, emit_state, has_prev):
    it = iter(refs)
    bias_ref = next(it)
    m0_ref = next(it) if has_init else None
    q_ref, k_ref, v_ref, o_ref, g_ref, cwq_ref, cwk_ref, nw_ref = (next(it) for _ in range(8))
    c0_ref, n0_ref = (next(it), next(it)) if has_init else (None, None)
    cprev_ref, nprev_ref = (next(it), next(it)) if has_prev else (None, None)
    h_ref = next(it)
    cst_out, nst_out, mst_out = (next(it), next(it), next(it)) if emit_state else (None, None, None)
    qs_ref, ks_ref, hf_ref, hb_ref, cst_ref, nst_ref = (next(it) for _ in range(6))

    b = pl.program_id(0)
    h = pl.program_id(1)
    nc = seq // chunk

    def conv_silu(x_ref, w_ref, scale):
        x = x_ref[...].astype(F32)
        row = lax.broadcasted_iota(jnp.int32, x.shape, 0)
        prev = jnp.where(row == 0, 0.0, pltpu.roll(x, 1, 0))
        nxt = jnp.where(row == seq - 1, 0.0, pltpu.roll(x, seq - 1, 0))
        y = prev * w_ref[0:1, :] + x * w_ref[1:2, :] + nxt * w_ref[2:3, :]
        return (_silu(y) * scale).astype(BF16)

    qs_ref[...] = conv_silu(q_ref, cwq_ref, 1.0)
    ks_ref[...] = conv_silu(k_ref, cwk_ref, B_HEAD_DIM ** -0.5)

    g = g_ref[0, 0]
    li = [g[2 * d:2 * d + 1, :] + bias_ref[h, 2 * d] for d in range(2)]
    lf = [_log_sigmoid(g[2 * d + 1:2 * d + 2, :] + bias_ref[h, 2 * d + 1]) for d in range(2)]

    rr = lax.broadcasted_iota(jnp.int32, (chunk, chunk), 0)
    cc = lax.broadcasted_iota(jnp.int32, (chunk, chunk), 1)
    diag = rr == cc

    def chunk_step(c, d, m_prev, first):
        rows = pl.ds(c * chunk, chunk)
        lanes = slice(c * chunk, (c + 1) * chunk)
        qc = qs_ref[rows, :]
        kc = ks_ref[rows, :]
        vc = v_ref[rows, :]
        li_row = li[d][:, lanes]
        lf_row = lf[d][:, lanes]
        causal = (cc <= rr) if d == 0 else (cc >= rr)
        b_col = jnp.sum(jnp.where(causal, lf_row, 0.0), axis=1, keepdims=True)
        b_row = jnp.sum(jnp.where(diag, b_col, 0.0), axis=0, keepdims=True)
        a_row = li_row - b_row
        a_col = jnp.sum(jnp.where(diag, a_row, 0.0), axis=1, keepdims=True)
        total = jnp.sum(lf_row, axis=1, keepdims=True)
        log_d = jnp.where(causal, b_col + a_row, NEG_INF)
        log_init = b_col + m_prev
        m_t = jnp.maximum(log_init, jnp.max(log_d, axis=1, keepdims=True))
        d_mat = jnp.exp(log_d - m_t)
        s = lax.dot_general(qc, kc, _NT, preferred_element_type=F32) * d_mat
        num = jnp.dot(s.astype(BF16), vc, preferred_element_type=F32)
        den = jnp.sum(s, axis=1, keepdims=True)
        if not first:
            w_init = jnp.exp(log_init - m_t)
            num = num + w_init * jnp.dot(qc, cst_ref[d].astype(BF16), preferred_element_type=F32)
            den = den + w_init * jnp.sum(qc.astype(F32) * nst_ref[d], axis=1, keepdims=True)
        hc = num / jnp.maximum(jnp.abs(den), jnp.exp(-m_t))
        (hf_ref if d == 0 else hb_ref)[rows, :] = hc
        log_w = total + a_col
        m_new = jnp.maximum(total + m_prev, jnp.max(log_w, axis=0, keepdims=True))
        kw = kc.astype(F32) * jnp.exp(log_w - m_new)
        c_add = lax.dot_general(kw.astype(BF16), vc, _TN, preferred_element_type=F32)
        n_add = jnp.sum(kw, axis=0, keepdims=True)
        if first:
            cst_ref[d] = c_add
            nst_ref[d] = n_add
        else:
            w_0 = jnp.exp(total + m_prev - m_new)
            cst_ref[d] = w_0 * cst_ref[d] + c_add
            nst_ref[d] = w_0 * nst_ref[d] + n_add
        return m_new

    if has_init:
        for d in range(2):
            cst_ref[d] = c0_ref[0, 0, d, 0]
            nst_ref[d] = n0_ref[0, 0, d, 0]
        m = [jnp.full((1, 1), m0_ref[b, layer, d, h], F32) for d in range(2)]
    else:
        m = [jnp.zeros((1, 1), F32) for _ in range(2)]
    for step in range(nc):
        first = (step == 0) and not has_init
        m[0] = chunk_step(step, 0, m[0], first)
        m[1] = chunk_step(nc - 1 - step, 1, m[1], first)

    og = jax.nn.sigmoid(o_ref[...].astype(F32))
    hh = _rms(og * (hf_ref[...] + hb_ref[...]))
    h_ref[...] = (hh * nw_ref[...]).astype(h_ref.dtype)

    if emit_state:
        row = lax.broadcasted_iota(jnp.int32, (8, LANES), 0)
        mst_out[0, 0] = jnp.where(row == 0, m[0], m[1])
        if has_prev:
            for d in range(2):
                cst_out[0, 0, d, 0] = cprev_ref[0, d, 0]
                nst_out[0, 0, d, 0] = nprev_ref[0, d, 0]
                cst_out[0, 1, d, 0] = cst_ref[d]
                nst_out[0, 1, d, 0] = nst_ref[d]
        else:
            for d in range(2):
                cst_out[0, d, 0] = cst_ref[d]
                nst_out[0, d, 0] = nst_ref[d]


def _mlstm(main, gates, conv_w, gate_bias, norm_w, init, layer, batch, seq, emit_state, prev):
    chunk = B_CHUNK
    hd = B_HEAD_DIM
    g4 = gates.reshape(batch, seq, 4, B_HEADS).transpose(0, 3, 2, 1)
    bias = gate_bias.reshape(4, B_HEADS).T
    has_init = init is not None
    has_prev = prev is not None

    def col(off):
        return lambda b, h: (b, off // hd + h)

    in_specs = [pl.BlockSpec(memory_space=pltpu.SMEM)]
    args = [bias]
    if has_init:
        c0, n0, m0 = init
        in_specs.append(pl.BlockSpec(memory_space=pltpu.SMEM))
        args.append(m0)
    in_specs += [pl.BlockSpec((seq, hd), col(MAIN_QB)),
                 pl.BlockSpec((seq, hd), col(MAIN_KB)),
                 pl.BlockSpec((seq, hd), col(MAIN_VB)),
                 pl.BlockSpec((seq, hd), col(MAIN_OB)),
                 pl.BlockSpec((1, 1, 4, seq), lambda b, h: (b, h, 0, 0)),
                 pl.BlockSpec((3, hd), lambda b, h: (0, h)),
                 pl.BlockSpec((3, hd), lambda b, h: (0, B_HEADS + h)),
                 pl.BlockSpec((1, hd), lambda b, h: (0, h))]
    args += [main, main, main, main, g4, conv_w, conv_w, norm_w.reshape(1, B_WIDTH)]
    if has_init:
        in_specs += [pl.BlockSpec((1, 1, 2, 1, hd, hd), lambda b, h: (b, layer, 0, h, 0, 0)),
                     pl.BlockSpec((1, 1, 2, 1, 1, hd), lambda b, h: (b, layer, 0, h, 0, 0))]
        args += [c0, n0.reshape(n0.shape[:4] + (1, hd))]
    if has_prev:
        in_specs += [pl.BlockSpec((1, 2, 1, hd, hd), lambda b, h: (b, 0, h, 0, 0)),
                     pl.BlockSpec((1, 2, 1, 1, hd), lambda b, h: (b, 0, h, 0, 0))]
        args += list(prev)
    out_specs = [pl.BlockSpec((seq, hd), lambda b, h: (b, h))]
    out_shape = [jax.ShapeDtypeStruct((batch * seq, B_WIDTH), BF16)]
    if emit_state:
        if has_prev:
            out_specs += [pl.BlockSpec((1, N_EVEN, 2, 1, hd, hd), lambda b, h: (b, 0, 0, h, 0, 0)),
                          pl.BlockSpec((1, N_EVEN, 2, 1, 1, hd), lambda b, h: (b, 0, 0, h, 0, 0))]
            out_shape += [jax.ShapeDtypeStruct((batch, N_EVEN, 2, B_HEADS, hd, hd), F32),
                          jax.ShapeDtypeStruct((batch, N_EVEN, 2, B_HEADS, 1, hd), F32)]
        else:
            out_specs += [pl.BlockSpec((1, 2, 1, hd, hd), lambda b, h: (b, 0, h, 0, 0)),
                          pl.BlockSpec((1, 2, 1, 1, hd), lambda b, h: (b, 0, h, 0, 0))]
            out_shape += [jax.ShapeDtypeStruct((batch, 2, B_HEADS, hd, hd), F32),
                          jax.ShapeDtypeStruct((batch, 2, B_HEADS, 1, hd), F32)]
        out_specs.append(pl.BlockSpec((1, 1, 8, LANES), lambda b, h: (b, h, 0, 0)))
        out_shape.append(jax.ShapeDtypeStruct((batch, B_HEADS, 8, LANES), F32))
    outs = pl.pallas_call(
        functools.partial(_mlstm_kernel, seq=seq, chunk=chunk, layer=layer, has_init=has_init,
                          emit_state=emit_state, has_prev=has_prev),
        grid=(batch, B_HEADS),
        in_specs=in_specs,
        out_specs=out_specs,
        out_shape=out_shape,
        scratch_shapes=[pltpu.VMEM((seq, hd), BF16), pltpu.VMEM((seq, hd), BF16),
                        pltpu.VMEM((seq, hd), F32), pltpu.VMEM((seq, hd), F32),
                        pltpu.VMEM((2, hd, hd), F32), pltpu.VMEM((2, 1, hd), F32)],
        compiler_params=_cparams(2),
        name="mlstm",
    )(*args)
    if not emit_state:
        return outs[0], None
    hb, cst, nst, mst = outs
    mstate = mst[:, :, 0:2, 0].transpose(0, 2, 1)
    return hb, (cst, nst, mstate)


def _outproj_kernel(*refs, n_branch, final):
    it = iter(refs)
    branches = [(next(it), next(it), next(it)) for _ in range(n_branch)]
    w_ref, x_ref, mod_ref = next(it), next(it), next(it)
    fn_ref = next(it) if final else None
    y_ref, wb_ref = next(it), next(it)

    @pl.when(pl.program_id(0) == 0)
    def _():
        wb_ref[...] = w_ref[0].astype(BF16)

    out = None
    for n, (a_ref, zlo_ref, zhi_ref) in enumerate(branches):
        base = n * 2 * Z_BLOCK
        for part, z_ref in enumerate((zlo_ref, zhi_ref)):
            cols = slice(part * Z_BLOCK, (part + 1) * Z_BLOCK)
            gated = (a_ref[:, cols].astype(F32) * _silu(z_ref[...].astype(F32))).astype(BF16)
            w = wb_ref[base + part * Z_BLOCK:base + (part + 1) * Z_BLOCK, :]
            p = jnp.dot(gated, w, preferred_element_type=F32)
            out = p if out is None else out + p
    y = x_ref[...] + mod_ref[0, 2:3, :] * out
    if final:
        y = _rms(y) * fn_ref[...]
    y_ref[...] = y


def _outproj(branches, w_out, layer, x, mod3, rows_per_mod, final_norm):
    m = x.shape[0]
    tm = 512
    if rows_per_mod is None:
        mod_map = lambda i: (0, 0, 0)
    else:
        mod_map = lambda i: (1 + (i * tm) // rows_per_mod, 0, 0)
    in_specs, args = [], []
    for a, z, z_off in branches:
        zb = z_off // Z_BLOCK
        in_specs += [pl.BlockSpec((tm, 2 * Z_BLOCK), lambda i: (i, 0)),
                     pl.BlockSpec((tm, Z_BLOCK), lambda i, zb=zb: (i, zb)),
                     pl.BlockSpec((tm, Z_BLOCK), lambda i, zb=zb: (i, zb + 1))]
        args += [a, z, z]
    wk = w_out.shape[1]
    in_specs += [pl.BlockSpec((1, wk, D_MODEL), lambda i: (layer, 0, 0)),
                 pl.BlockSpec((tm, D_MODEL), lambda i: (i, 0)),
                 pl.BlockSpec((1, 3, D_MODEL), mod_map)]
    args += [w_out, x, mod3]
    final = final_norm is not None
    if final:
        in_specs.append(pl.BlockSpec((1, D_MODEL), lambda i: (0, 0)))
        args.append(final_norm.reshape(1, D_MODEL))
    return pl.pallas_call(
        functools.partial(_outproj_kernel, n_branch=len(branches), final=final),
        grid=(m // tm,),
        in_specs=in_specs,
        out_specs=pl.BlockSpec((tm, D_MODEL), lambda i: (i, 0)),
        out_shape=jax.ShapeDtypeStruct((m, D_MODEL), F32),
        scratch_shapes=[pltpu.VMEM((wk, D_MODEL), BF16)],
        compiler_params=_cparams(1),
        name="outproj",
    )(*args)


def _inproj_c_kernel(*refs, rope):
    it = iter(refs)
    x_ref, mod_ref, g_ref, w_ref, qn_ref, wqb_ref, kvn_ref, wkvb_ref = (next(it) for _ in range(8))
    cos_ref, sin_ref = (next(it), next(it)) if rope else (None, None)
    q_ref, kv_ref, z_ref, ckv_ref, kr_ref, krb_ref = (next(it) for _ in range(6))

    hn = _norm_modulate(x_ref[...], g_ref[...], mod_ref).astype(BF16)
    r = jnp.dot(hn, w_ref[...], preferred_element_type=F32)
    qa = _rms(r[:, CIN_QA:CIN_KVA]) * qn_ref[...]
    ckv = _rms(r[:, CIN_KVA:CIN_Z]) * kvn_ref[...]
    z_ref[...] = r[:, CIN_Z:CIN_KR].astype(BF16)
    kr = r[:, CIN_KR:CIN_WIDTH]
    ckv_ref[...] = ckv
    kr_ref[...] = kr
    kv_ref[...] = jnp.dot(ckv.astype(BF16), wkvb_ref[...], preferred_element_type=F32).astype(BF16)
    scale = (C_NOPE + C_ROPE) ** -0.5 * LOG2E
    q = jnp.dot(qa.astype(BF16), wqb_ref[...], preferred_element_type=F32) * scale
    half = C_ROPE // 4
    if rope:
        cos = cos_ref[...]
        sin = sin_ref[...]
        kr = _rope(kr, cos, sin, half)
        for hd in range(C_HEADS):
            hs = slice(hd * C_HEAD_PAD, (hd + 1) * C_HEAD_PAD)
            q_ref[:, hs] = _rope(q[:, hs], cos, sin, half).astype(BF16)
    else:
        q_ref[...] = q.astype(BF16)
    krb_ref[...] = kr.astype(BF16)


def _inproj_c(x, mod3, g, w_in, q_norm, w_qb, kv_norm, w_kvb, rows_per_mod, tables):
    m = x.shape[0]
    tm = 512
    rope = tables is not None
    if rows_per_mod is None:
        mod_map = lambda i: (0, 0, 0)
    else:
        mod_map = lambda i: (1 + (i * tm) // rows_per_mod, 0, 0)
    const = lambda i: (0, 0)
    qw = C_HEADS * C_HEAD_PAD
    kvw = C_HEADS * (C_NOPE + C_VDIM)
    in_specs = [pl.BlockSpec((tm, D_MODEL), lambda i: (i, 0)),
                pl.BlockSpec((1, 3, D_MODEL), mod_map),
                pl.BlockSpec((1, D_MODEL), const),
                pl.BlockSpec((D_MODEL, CIN_WIDTH), const),
                pl.BlockSpec((1, C_Q_RANK), const),
                pl.BlockSpec((C_Q_RANK, qw), const),
                pl.BlockSpec((1, C_KV_RANK), const),
                pl.BlockSpec((C_KV_RANK, kvw), const)]
    args = [x, mod3, g, w_in, q_norm, w_qb, kv_norm, w_kvb]
    if rope:
        tpb = rows_per_mod // tm
        in_specs += [pl.BlockSpec((tm, LANES), lambda i: (i % tpb, 0))] * 2
        args += list(tables)
    return pl.pallas_call(
        functools.partial(_inproj_c_kernel, rope=rope),
        grid=(m // tm,),
        in_specs=in_specs,
        out_specs=[pl.BlockSpec((tm, qw), lambda i: (i, 0)),
                   pl.BlockSpec((tm, kvw), lambda i: (i, 0)),
                   pl.BlockSpec((tm, C_WIDTH), lambda i: (i, 0)),
                   pl.BlockSpec((tm, C_KV_RANK), lambda i: (i, 0)),
                   pl.BlockSpec((tm, LANES), lambda i: (i, 0)),
                   pl.BlockSpec((tm, LANES), lambda i: (i, 0))],
        out_shape=[jax.ShapeDtypeStruct((m, qw), BF16),
                   jax.ShapeDtypeStruct((m, kvw), BF16),
                   jax.ShapeDtypeStruct((m, C_WIDTH), BF16),
                   jax.ShapeDtypeStruct((m, C_KV_RANK), F32),
                   jax.ShapeDtypeStruct((m, LANES), F32),
                   jax.ShapeDtypeStruct((m, LANES), BF16)],
        compiler_params=_cparams(1),
        name="inproj_c",
    )(*args)


def _matmul_kernel(x_ref, w_ref, o_ref):
    o_ref[...] = jnp.dot(x_ref[...], w_ref[...], preferred_element_type=F32).astype(o_ref.dtype)


def _matmul(x, w, tm):
    m, k = x.shape
    n = w.shape[1]
    return pl.pallas_call(
        _matmul_kernel,
        grid=(m // tm,),
        in_specs=[pl.BlockSpec((tm, k), lambda i: (i, 0)),
                  pl.BlockSpec((k, n), lambda i: (0, 0))],
        out_specs=pl.BlockSpec((tm, n), lambda i: (i, 0)),
        out_shape=jax.ShapeDtypeStruct((m, n), BF16),
        compiler_params=_cparams(1),
        name="matmul",
    )(x, w)


def _mla_slabs(kv, kr_slab):
    lane = lax.broadcasted_iota(jnp.int32, kv.shape, 1)
    nope = lane < C_NOPE
    return jnp.where(nope, kv, kr_slab), jnp.where(nope, jnp.ones_like(kv), kv)


def _attn_c_kernel(*refs, has_ctx):
    it = iter(refs)
    q_ref, kv_ref, kr_ref = next(it), next(it), next(it)
    kvc_ref, krc_ref = (next(it), next(it)) if has_ctx else (None, None)
    o_ref = next(it)
    kown_ref, vown_ref = next(it), next(it)
    kctx_ref, vctx_ref = (next(it), next(it)) if has_ctx else (None, None)

    @pl.when(pl.program_id(1) == 0)
    def _():
        for h in range(C_HEADS):
            hs = slice(h * C_HEAD_PAD, (h + 1) * C_HEAD_PAD)
            kown_ref[:, hs], vown_ref[:, hs] = _mla_slabs(kv_ref[:, hs], kr_ref[...])
            if has_ctx:
                kctx_ref[:, hs], vctx_ref[:, hs] = _mla_slabs(kvc_ref[:, hs], krc_ref[...])

    for h in range(C_HEADS):
        hs = slice(h * C_HEAD_PAD, (h + 1) * C_HEAD_PAD)
        qh = q_ref[:, hs]
        scores = [lax.dot_general(qh, kown_ref[:, hs], _NT, preferred_element_type=F32)]
        values = [vown_ref[:, hs]]
        if has_ctx:
            scores.append(lax.dot_general(qh, kctx_ref[:, hs], _NT, preferred_element_type=F32))
            values.append(vctx_ref[:, hs])
        o = _softmax_pv(scores, values, None)
        o_ref[:, h * C_VDIM:(h + 1) * C_VDIM] = o[:, C_NOPE:].astype(o_ref.dtype)


def _attn_c(q, kv, kr_slab, ctx, batch, seq, tq):
    nq = seq // tq
    w = C_HEADS * C_HEAD_PAD
    has_ctx = ctx is not None
    in_specs = [pl.BlockSpec((tq, w), lambda b, i: (b * nq + i, 0)),
                pl.BlockSpec((seq, w), lambda b, i: (b, 0)),
                pl.BlockSpec((seq, LANES), lambda b, i: (b, 0))]
    args = [q, kv, kr_slab]
    scratch = [pltpu.VMEM((seq, w), BF16), pltpu.VMEM((seq, w), BF16)]
    if has_ctx:
        kv_ctx, kr_ctx = ctx
        nctx = kv_ctx.shape[0] // batch
        in_specs += [pl.BlockSpec((nctx, w), lambda b, i: (b, 0)),
                     pl.BlockSpec((nctx, LANES), lambda b, i: (b, 0))]
        args += [kv_ctx, kr_ctx]
        scratch += [pltpu.VMEM((nctx, w), BF16), pltpu.VMEM((nctx, w), BF16)]
    return pl.pallas_call(
        functools.partial(_attn_c_kernel, has_ctx=has_ctx),
        grid=(batch, nq),
        in_specs=in_specs,
        out_specs=pl.BlockSpec((tq, C_WIDTH), lambda b, i: (b * nq + i, 0)),
        out_shape=jax.ShapeDtypeStruct((batch * seq, C_WIDTH), BF16),
        scratch_shapes=scratch,
        compiler_params=_cparams(2),
        name="attn_c",
    )(*args)


def _rope_tables(n_tokens):
    pos_r = np.repeat(np.arange(n_tokens // GRID_W), GRID_W).astype(np.float64)
    pos_c = np.tile(np.arange(GRID_W), n_tokens // GRID_W).astype(np.float64)

    def seg(d_axis):
        half = d_axis // 2
        freqs = np.power(ROPE_BASE, -np.arange(half, dtype=np.float64) / half)
        cos, sin = [], []
        for pos in (pos_r, pos_c):
            ang = pos[:, None] * freqs[None, :]
            cos += [np.cos(ang), np.cos(ang)]
            sin += [-np.sin(ang), np.sin(ang)]
        return np.concatenate(cos, axis=1), np.concatenate(sin, axis=1)

    cos_a, sin_a = seg(A_HEAD_DIM // 2)
    cos_a, sin_a = np.tile(cos_a, (1, 2)), np.tile(sin_a, (1, 2))
    cos_r, sin_r = seg(C_ROPE // 2)
    ones = np.ones((n_tokens, C_NOPE))
    pad = C_HEAD_PAD - C_NOPE - C_ROPE
    cos_c = np.concatenate([ones, cos_r, np.ones((n_tokens, pad))], axis=1)
    sin_c = np.concatenate([0 * ones, sin_r, np.zeros((n_tokens, pad))], axis=1)
    f = lambda a: jnp.asarray(a, dtype=F32)
    return (f(cos_a), f(sin_a)), (f(cos_c), f(sin_c))


def _pad_cols(w, left, total):
    return jnp.pad(w, ((0, 0), (left, total - left - w.shape[1])))


def kernel(x_prompt, x_sample, cache_a_k, cache_a_v, state_b_mem, state_b_norm, state_b_max, cache_c_kv, cache_c_krope, c, c_ctx, norm_g, w_mod, b_mod, w_in_ab, sink_a, conv_b, gate_bias_b, norm_b, w_out_ab, w_in_c, q_norm_c, w_qb_c, kv_norm_c, w_kvb_c, w_out_c, final_norm):
    bp, tp, _ = x_prompt.shape
    bs, ts, _ = x_sample.shape
    past = cache_a_k.shape[2]
    tables_a, tables_c = _rope_tables(ts)

    cond = jnp.zeros((16, D_MODEL), F32).at[0].set(c_ctx).at[1:1 + bs].set(c)
    mods = _adaln(cond, w_mod, b_mod).reshape(DEPTH, 16, 3, D_MODEL)

    yp = x_prompt.reshape(bp * tp, D_MODEL)
    ys = x_sample.reshape(bs * ts, D_MODEL)
    a_k, a_v, b_max, c_kvs, c_krs = [], [], [], [], []
    states = None
    for l in range(DEPTH):
        j = l // 2
        mod3 = mods[l]
        g = norm_g[l].reshape(1, D_MODEL)
        fin = final_norm if l == DEPTH - 1 else None
        if l % 2 == 0:
            w_gate = _pad_cols(w_in_ab[j, :, MAIN_WIDTH:], 0, LANES).astype(BF16)
            main_p, k_p, v_p, gates_p = _inproj_ab(yp, mod3, g, w_in_ab, j, w_gate, None, True)
            main_s, gates_s = _inproj_ab(ys, mod3, g, w_in_ab, j, w_gate, ts, False)
            a_k.append(k_p.reshape(bp, tp, A_KV_HEADS, A_HEAD_DIM))
            a_v.append(v_p.reshape(bp, tp, A_KV_HEADS, A_HEAD_DIM))

            attn_p = _attn_a_prompt(main_p, sink_a[j], bp, tp)
            ck = cache_a_k[:, j].reshape(bs, past, A_KV_WIDTH).astype(BF16)
            cv = cache_a_v[:, j].reshape(bs, past, A_KV_WIDTH).astype(BF16)
            attn_s = _attn_a_sample(main_s, sink_a[j], ck, cv, *tables_a, bs, ts)

            prev = None if states is None else states[:2]
            hb_p, states = _mlstm(main_p, gates_p[:, :4 * B_HEADS], conv_b[j], gate_bias_b[j], norm_b[j],
                                  None, j, bp, tp, True, prev)
            init = (state_b_mem, state_b_norm, state_b_max)
            hb_s, _ = _mlstm(main_s, gates_s[:, :4 * B_HEADS], conv_b[j], gate_bias_b[j], norm_b[j],
                             init, j, bs, ts, False, None)
            b_max.append(states[2])

            yp = _outproj([(attn_p, main_p, MAIN_ZA), (hb_p, main_p, MAIN_ZB)], w_out_ab, j, yp, mod3, None, fin)
            ys = _outproj([(attn_s, main_s, MAIN_ZA), (hb_s, main_s, MAIN_ZB)], w_out_ab, j, ys, mod3, ts, fin)
        else:
            w = w_in_c[j]
            o_kva = C_Q_RANK
            o_kr = C_Q_RANK + C_KV_RANK
            o_z = o_kr + C_ROPE
            w_in = jnp.concatenate([w[:, :o_kva], w[:, o_kva:o_kr], w[:, o_z:],
                                    _pad_cols(w[:, o_kr:o_z], C_NOPE, C_HEAD_PAD)], axis=1).astype(BF16)
            wq = w_qb_c[j].reshape(C_Q_RANK, C_HEADS, C_NOPE + C_ROPE)
            wq = jnp.pad(wq, ((0, 0), (0, 0), (0, C_HEAD_PAD - C_NOPE - C_ROPE)))
            wq = wq.reshape(C_Q_RANK, C_HEADS * C_HEAD_PAD).astype(BF16)
            wkv = w_kvb_c[j].astype(BF16)
            qn = q_norm_c[j].reshape(1, C_Q_RANK)
            kvn = kv_norm_c[j].reshape(1, C_KV_RANK)

            q_p, kv_p, z_p, ckv_p, kr_p, krb_p = _inproj_c(yp, mod3, g, w_in, qn, wq, kvn, wkv, None, None)
            q_s, kv_s, z_s, _, _, krb_s = _inproj_c(ys, mod3, g, w_in, qn, wq, kvn, wkv, ts, tables_c)
            c_kvs.append(ckv_p.reshape(bp, tp, C_KV_RANK))
            c_krs.append(kr_p[:, C_NOPE:C_NOPE + C_ROPE].reshape(bp, tp, C_ROPE))

            cc = cache_c_kv[:, j].reshape(bs * past, C_KV_RANK).astype(BF16)
            kv_ctx = _matmul(cc, wkv, 512)
            kr_ctx = _pad_cols(cache_c_krope[:, j].reshape(bs * past, C_ROPE), C_NOPE, C_HEAD_PAD).astype(BF16)

            attn_p = _attn_c(q_p, kv_p, krb_p, None, bp, tp, tp)
            attn_s = _attn_c(q_s, kv_s, krb_s, (kv_ctx, kr_ctx), bs, ts, 512)

            yp = _outproj([(attn_p, z_p, 0)], w_out_c, j, yp, mod3, None, fin)
            ys = _outproj([(attn_s, z_s, 0)], w_out_c, j, ys, mod3, ts, fin)

    b_mem = states[0]
    b_nrm = states[1].reshape(bp, N_EVEN, 2, B_HEADS, B_HEAD_DIM)
    return (yp.reshape(bp, tp, D_MODEL), ys.reshape(bs, ts, D_MODEL),
            jnp.stack(a_k, axis=1), jnp.stack(a_v, axis=1), b_mem, b_nrm,
            jnp.stack(b_max, axis=1), jnp.stack(c_kvs, axis=1), jnp.stack(c_krs, axis=1))
```

```python
import functools
import math

import numpy as np
import jax
import jax.numpy as jnp
from jax import lax
from jax.experimental import pallas as pl
from jax.experimental.pallas import tpu as pltpu

F32 = jnp.float32
BF16 = jnp.bfloat16

D_MODEL = 1024
DEPTH = 4
N_EVEN = 2
EPS = 1e-6
ROPE_BASE = 10000.0
NEG_INF = -1e30
GRID_W = 64
LOG2E = math.log2(math.e)
A_HEADS = 16
A_KV_HEADS = 4
A_GROUP = A_HEADS // A_KV_HEADS
A_HEAD_DIM = 64
A_WIDTH = A_HEADS * A_HEAD_DIM
A_KV_WIDTH = A_KV_HEADS * A_HEAD_DIM
WINDOW = 128
BLOCK = 128
B_HEADS = 4
B_HEAD_DIM = 256
B_WIDTH = B_HEADS * B_HEAD_DIM
B_CHUNK = 256
C_HEADS = 16
C_NOPE = 64
C_ROPE = 32
C_VDIM = 64
C_Q_RANK = 384
C_KV_RANK = 256
C_WIDTH = C_HEADS * C_VDIM
C_HEAD_PAD = 128

LANES = 128
MAIN_QA, MAIN_KA, MAIN_VA, MAIN_ZA, MAIN_QB, MAIN_KB, MAIN_VB, MAIN_OB, MAIN_ZB = (
    0, 1024, 1280, 1536, 2560, 3584, 4608, 5632, 6656)
MAIN_WIDTH = 7680
CIN_QA, CIN_KVA, CIN_Z, CIN_KR, CIN_WIDTH = 0, 384, 640, 1664, 1792
Z_BLOCK = 512
QK_TILE = 1280
A_QSCALE = A_HEAD_DIM ** -0.5 * LOG2E

MIB = 1024 * 1024
VMEM_LIMIT = 48 * MIB
VMEM_LIMIT_WIDE = 56 * MIB

_NT = (((1,), (1,)), ((), ()))
_TN = (((0,), (0,)), ((), ()))


def _cparams(n_axes, vmem=VMEM_LIMIT):
    return pltpu.CompilerParams(dimension_semantics=("arbitrary",) * n_axes,
                                vmem_limit_bytes=vmem)


def _silu(x):
    return x * jax.nn.sigmoid(x)


def _log_sigmoid(x):
    return jnp.minimum(x, 0.0) - jnp.log1p(jnp.exp(-jnp.abs(x)))


def _rms(x):
    return x * lax.rsqrt(jnp.mean(x * x, axis=-1, keepdims=True) + EPS)


def _norm_modulate(x, g, mod_ref):
    y = _rms(x) * g
    return y * (1.0 + mod_ref[0, 1:2, :]) + mod_ref[0, 0:1, :]


def _swap_halves(x, half):
    lane = lax.broadcasted_iota(jnp.int32, x.shape, 1)
    first = (lane % (2 * half)) < half
    return jnp.where(first, pltpu.roll(x, LANES - half, 1), pltpu.roll(x, half, 1))


def _rope(x, cos, sin, half):
    return x * cos + _swap_halves(x, half) * sin


def _softmax_pv(scores, values, sink):
    tiles = [s[:, t * LANES:(t + 1) * LANES] for s in scores for t in range(s.shape[1] // LANES)]
    m = jnp.max(functools.reduce(jnp.maximum, tiles), axis=-1, keepdims=True)
    if sink is not None:
        m = jnp.maximum(m, sink)
    res = functools.reduce(jnp.add, [
        jnp.dot(jnp.exp2(s - m).astype(BF16), v, preferred_element_type=F32)
        for s, v in zip(scores, values)])
    den = pltpu.roll(res, LANES // 2, 1)
    if sink is not None:
        den = den + jnp.exp2(sink - m)
    return res / den


def _adaln_kernel(c_ref, w_ref, b_ref, o_ref):
    a = _silu(c_ref[...]).astype(BF16)
    w = w_ref[0].astype(BF16)
    o_ref[0] = jnp.dot(a, w, preferred_element_type=F32) + b_ref[0]


def _adaln(cond, w_mod, b_mod):
    tn = 1024
    n = 3 * D_MODEL
    return pl.pallas_call(
        _adaln_kernel,
        grid=(DEPTH, n // tn),
        in_specs=[pl.BlockSpec((16, D_MODEL), lambda l, j: (0, 0)),
                  pl.BlockSpec((1, D_MODEL, tn), lambda l, j: (l, 0, j)),
                  pl.BlockSpec((1, 1, tn), lambda l, j: (l, 0, j))],
        out_specs=pl.BlockSpec((1, 16, tn), lambda l, j: (l, 0, j)),
        out_shape=jax.ShapeDtypeStruct((DEPTH, 16, n), F32),
        compiler_params=_cparams(2),
        name="adaln",
    )(cond, w_mod, b_mod.reshape(DEPTH, 1, n))


def _cast_kernel(w_ref, o_ref):
    o_ref[...] = w_ref[0].astype(o_ref.dtype)


def _cast_rows(w_t, layer, rows):
    tr = QK_TILE
    d = w_t.shape[2]
    return pl.pallas_call(
        _cast_kernel,
        grid=(rows // tr,),
        in_specs=[pl.BlockSpec((1, tr, d), lambda i: (layer, i, 0))],
        out_specs=pl.BlockSpec((tr, d), lambda i: (i, 0)),
        out_shape=jax.ShapeDtypeStruct((rows, d), BF16),
        compiler_params=_cparams(1),
        name="cast_rows",
    )(w_t)


def _inproj_ab_kernel(*refs, emit_kv, rope):
    it = iter(refs)
    x_ref, mod_ref, g_ref, wb_ref, wg_ref = (next(it) for _ in range(5))
    cos_ref, sin_ref = (next(it), next(it)) if rope else (None, None)
    main_ref = next(it)
    k_ref, v_ref = (next(it), next(it)) if emit_kv else (None, None)
    gate_ref = next(it)

    hn = _norm_modulate(x_ref[...], g_ref[...], mod_ref).astype(BF16)
    gate_ref[...] = lax.dot_general(hn, wg_ref[0].astype(BF16), _NT, preferred_element_type=F32)
    if emit_kv:
        kv = lax.dot_general(hn, wb_ref[MAIN_KA:MAIN_ZA, :], _NT, preferred_element_type=F32)
        k_ref[...] = kv[:, :A_KV_WIDTH]
        v_ref[...] = kv[:, A_KV_WIDTH:]
    for t in range(MAIN_WIDTH // QK_TILE):
        cols = slice(t * QK_TILE, (t + 1) * QK_TILE)
        res = lax.dot_general(hn, wb_ref[cols, :], _NT, preferred_element_type=F32)
        if t > 0:
            main_ref[:, cols] = res.astype(BF16)
            continue
        for c in range(QK_TILE // LANES):
            cs = slice(c * LANES, (c + 1) * LANES)
            chunk = res[:, cs]
            if rope:
                chunk = _rope(chunk, cos_ref[...], sin_ref[...], A_HEAD_DIM // 4)
            if c < A_WIDTH // LANES:
                chunk = chunk * A_QSCALE
            main_ref[:, cs] = chunk.astype(BF16)


def _inproj_ab(x, mod3, g, w_bf, w_t, layer, rows_per_mod, emit_kv, tables):
    m = x.shape[0]
    tm = 512
    rope = tables is not None
    if rows_per_mod is None:
        mod_map = lambda i: (0, 0, 0)
    else:
        mod_map = lambda i: (1 + (i * tm) // rows_per_mod, 0, 0)
    n_gate = 4 * B_HEADS
    in_specs = [pl.BlockSpec((tm, D_MODEL), lambda i: (i, 0)),
                pl.BlockSpec((1, 3, D_MODEL), mod_map),
                pl.BlockSpec((1, D_MODEL), lambda i: (0, 0)),
                pl.BlockSpec((MAIN_WIDTH, D_MODEL), lambda i: (0, 0), pipeline_mode=pl.Buffered(1)),
                pl.BlockSpec((1, n_gate, D_MODEL), lambda i: (layer, MAIN_WIDTH // n_gate, 0))]
    args = [x, mod3, g, w_bf, w_t]
    if rope:
        tpb = rows_per_mod // tm
        in_specs += [pl.BlockSpec((tm, LANES), lambda i: (i % tpb, 0))] * 2
        args += list(tables)
    out_specs = [pl.BlockSpec((tm, MAIN_WIDTH), lambda i: (i, 0))]
    out_shape = [jax.ShapeDtypeStruct((m, MAIN_WIDTH), BF16)]
    if emit_kv:
        out_specs += [pl.BlockSpec((tm, A_KV_WIDTH), lambda i: (i, 0))] * 2
        out_shape += [jax.ShapeDtypeStruct((m, A_KV_WIDTH), F32)] * 2
    out_specs.append(pl.BlockSpec((tm, n_gate), lambda i: (i, 0)))
    out_shape.append(jax.ShapeDtypeStruct((m, n_gate), F32))
    return pl.pallas_call(
        functools.partial(_inproj_ab_kernel, emit_kv=emit_kv, rope=rope),
        grid=(m // tm,),
        in_specs=in_specs,
        out_specs=out_specs,
        out_shape=out_shape,
        compiler_params=_cparams(1, VMEM_LIMIT_WIDE),
        name="inproj_ab",
    )(*args)


def _value_slabs(v_ref, slab_ref):
    keys = v_ref.shape[0]
    ones = jnp.ones((keys, LANES - A_HEAD_DIM), BF16)
    for g in range(A_KV_HEADS):
        slab_ref[:, g * LANES:g * LANES + A_HEAD_DIM] = v_ref[:, g * A_HEAD_DIM:(g + 1) * A_HEAD_DIM]
        slab_ref[:, g * LANES + A_HEAD_DIM:(g + 1) * LANES] = ones


def _group_queries(q_ref, g):
    return jnp.concatenate([q_ref[:, (g * A_GROUP + hh) * A_HEAD_DIM:(g * A_GROUP + hh + 1) * A_HEAD_DIM]
                            for hh in range(A_GROUP)], axis=0)


def _sink_softmax(s_ref, e_ref, t_ref, sink_ref):
    s = s_ref[...]
    sink = sink_ref[...] * LOG2E
    m = jnp.maximum(jnp.max(s, axis=-1, keepdims=True), sink)
    e_ref[...] = jnp.exp2(s - m).astype(BF16)
    t_ref[...] = jnp.exp2(sink - m)


def _attn_a_prompt_kernel(sink_ref, q_ref, k_ref, v_ref, o_ref, s_ref, e_ref, t_ref, vs_ref):
    _value_slabs(v_ref, vs_ref)
    seq = q_ref.shape[0]
    rows = A_GROUP * seq
    for g in range(A_KV_HEADS):
        heads = slice(g * A_GROUP, (g + 1) * A_GROUP)
        ks = slice(g * A_HEAD_DIM, (g + 1) * A_HEAD_DIM)
        s = lax.dot_general(_group_queries(q_ref, g), k_ref[:, ks], _NT, preferred_element_type=F32)
        s_ref[heads] = s.reshape(A_GROUP, seq, seq)
    _sink_softmax(s_ref, e_ref, t_ref, sink_ref)
    for g in range(A_KV_HEADS):
        heads = slice(g * A_GROUP, (g + 1) * A_GROUP)
        res = jnp.dot(e_ref[heads].reshape(rows, seq), vs_ref[:, g * LANES:(g + 1) * LANES],
                      preferred_element_type=F32)
        o = res / (pltpu.roll(res, LANES // 2, 1) + t_ref[heads].reshape(rows, 1))
        for hh in range(A_GROUP):
            h = g * A_GROUP + hh
            o_ref[:, h * A_HEAD_DIM:(h + 1) * A_HEAD_DIM] = (
                o[hh * seq:(hh + 1) * seq, :A_HEAD_DIM].astype(o_ref.dtype))


def _attn_a_prompt(main, sink, batch, seq):
    kb = MAIN_KA // A_KV_WIDTH
    vb = MAIN_VA // A_KV_WIDTH
    return pl.pallas_call(
        _attn_a_prompt_kernel,
        grid=(batch,),
        in_specs=[pl.BlockSpec((A_HEADS, 1, 1), lambda b: (0, 0, 0)),
                  pl.BlockSpec((seq, A_WIDTH), lambda b: (b, MAIN_QA // A_WIDTH)),
                  pl.BlockSpec((seq, A_KV_WIDTH), lambda b: (b, kb)),
                  pl.BlockSpec((seq, A_KV_WIDTH), lambda b: (b, vb))],
        out_specs=pl.BlockSpec((seq, A_WIDTH), lambda b: (b, 0)),
        out_shape=jax.ShapeDtypeStruct((batch * seq, A_WIDTH), BF16),
        scratch_shapes=[pltpu.VMEM((A_HEADS, seq, seq), F32),
                        pltpu.VMEM((A_HEADS, seq, seq), BF16),
                        pltpu.VMEM((A_HEADS, seq, 1), F32),
                        pltpu.VMEM((seq, A_KV_HEADS * LANES), BF16)],
        compiler_params=_cparams(1),
        name="attn_a_prompt",
    )(sink.reshape(A_HEADS, 1, 1), main, main, main)


def _attn_a_sample_kernel(sink_ref, q_ref, k_ref, v_ref, ck_ref, cv_ref, bias_ref, o_ref, vs_ref, cvs_ref):
    i = pl.program_id(1)
    seq = k_ref.shape[0]
    span = 3 * BLOCK

    @pl.when(i == 0)
    def _():
        _value_slabs(v_ref, vs_ref)
        _value_slabs(cv_ref.at[0], cvs_ref)

    start = pl.multiple_of(jnp.clip((i - 1) * BLOCK, 0, seq - span), BLOCK)
    kw = k_ref[pl.ds(start, span), :]
    vw = vs_ref[pl.ds(start, span), :]
    ck = ck_ref[0]
    bias = bias_ref[(i * BLOCK - start) // BLOCK]
    bias = jnp.concatenate([bias] * A_GROUP, axis=0)
    rows = A_GROUP * BLOCK
    head_of_row = lax.broadcasted_iota(jnp.int32, (rows, 1), 0) // BLOCK
    for g in range(A_KV_HEADS):
        ks = slice(g * A_HEAD_DIM, (g + 1) * A_HEAD_DIM)
        gs = slice(g * LANES, (g + 1) * LANES)
        sink = jnp.zeros((rows, 1), F32)
        for hh in range(A_GROUP):
            sink = jnp.where(head_of_row == hh, sink_ref[g * A_GROUP + hh] * LOG2E, sink)
        qg = _group_queries(q_ref, g)
        s_loc = lax.dot_general(qg, kw[:, ks], _NT, preferred_element_type=F32) + bias
        s_ctx = lax.dot_general(qg, ck[:, ks], _NT, preferred_element_type=F32)
        o = _softmax_pv([s_loc, s_ctx], [vw[:, gs], cvs_ref[:, gs]], sink)
        for hh in range(A_GROUP):
            h = g * A_GROUP + hh
            o_ref[:, h * A_HEAD_DIM:(h + 1) * A_HEAD_DIM] = (
                o[hh * BLOCK:(hh + 1) * BLOCK, :A_HEAD_DIM].astype(o_ref.dtype))


def _window_bias():
    r = np.arange(BLOCK)[:, None]
    c = np.arange(3 * BLOCK)[None, :]
    masks = [np.where(np.abs(off + r - c) <= WINDOW, 0.0, NEG_INF) for off in (0, BLOCK, 2 * BLOCK)]
    return jnp.asarray(np.stack(masks), dtype=F32)


def _attn_a_sample(main, sink, ck, cv, batch, seq):
    nb = seq // BLOCK
    kb = MAIN_KA // A_KV_WIDTH
    vb = MAIN_VA // A_KV_WIDTH
    ctx = ck.shape[1]
    slab_w = A_KV_HEADS * LANES
    return pl.pallas_call(
        _attn_a_sample_kernel,
        grid=(batch, nb),
        in_specs=[pl.BlockSpec(memory_space=pltpu.SMEM),
                  pl.BlockSpec((BLOCK, A_WIDTH), lambda b, i: (b * nb + i, MAIN_QA // A_WIDTH)),
                  pl.BlockSpec((seq, A_KV_WIDTH), lambda b, i: (b, kb)),
                  pl.BlockSpec((seq, A_KV_WIDTH), lambda b, i: (b, vb)),
                  pl.BlockSpec((1, ctx, A_KV_WIDTH), lambda b, i: (b, 0, 0)),
                  pl.BlockSpec((1, ctx, A_KV_WIDTH), lambda b, i: (b, 0, 0)),
                  pl.BlockSpec((3, BLOCK, 3 * BLOCK), lambda b, i: (0, 0, 0))],
        out_specs=pl.BlockSpec((BLOCK, A_WIDTH), lambda b, i: (b * nb + i, 0)),
        out_shape=jax.ShapeDtypeStruct((batch * seq, A_WIDTH), BF16),
        scratch_shapes=[pltpu.VMEM((seq, slab_w), BF16),
                        pltpu.VMEM((ctx, slab_w), BF16)],
        compiler_params=_cparams(2),
        name="attn_a_sample",
    )(sink, main, main, main, ck, cv, _window_bias())


def _mlstm_kernel(*refs, seq, chunk, layer, has_init, emit_state, has_prev):
    it = iter(refs)
    bias_ref = next(it)
    m0_ref = next(it) if has_init else None
    q_ref, k_ref, v_ref, o_ref, g_ref, cwq_ref, cwk_ref, nw_ref = (next(it) for _ in range(8))
    c0_ref, n0_ref = (next(it), next(it)) if has_init else (None, None)
    cprev_ref, nprev_ref = (next(it), next(it)) if has_prev else (None, None)
    h_ref = next(it)
    cst_out, nst_out, mst_out = (next(it), next(it), next(it)) if emit_state else (None, None, None)
    qs_ref, ks_ref, hf_ref, hb_ref, cst_ref, nst_ref = (next(it) for _ in range(6))

    b = pl.program_id(0)
    h = pl.program_id(1)
    nc = seq // chunk

    def conv_silu(x_ref, w_ref, scale):
        x = x_ref[...].astype(F32)
        row = lax.broadcasted_iota(jnp.int32, x.shape, 0)
        prev = jnp.where(row == 0, 0.0, pltpu.roll(x, 1, 0))
        nxt = jnp.where(row == seq - 1, 0.0, pltpu.roll(x, seq - 1, 0))
        y = prev * w_ref[0:1, :] + x * w_ref[1:2, :] + nxt * w_ref[2:3, :]
        return (_silu(y) * scale).astype(BF16)

    qs_ref[...] = conv_silu(q_ref, cwq_ref, 1.0)
    ks_ref[...] = conv_silu(k_ref, cwk_ref, B_HEAD_DIM ** -0.5)

    g = g_ref[0, 0]
    li = [g[2 * d:2 * d + 1, :] + bias_ref[h, 2 * d] for d in range(2)]
    lf = [_log_sigmoid(g[2 * d + 1:2 * d + 2, :] + bias_ref[h, 2 * d + 1]) for d in range(2)]

    rr = lax.broadcasted_iota(jnp.int32, (chunk, chunk), 0)
    cc = lax.broadcasted_iota(jnp.int32, (chunk, chunk), 1)
    diag = rr == cc

    def chunk_step(c, d, m_prev, first):
        rows = pl.ds(c * chunk, chunk)
        lanes = slice(c * chunk, (c + 1) * chunk)
        qc = qs_ref[rows, :]
        kc = ks_ref[rows, :]
        vc = v_ref[rows, :]
        li_row = li[d][:, lanes]
        lf_row = lf[d][:, lanes]
        causal = (cc <= rr) if d == 0 else (cc >= rr)
        b_col = jnp.sum(jnp.where(causal, lf_row, 0.0), axis=1, keepdims=True)
        b_row = jnp.sum(jnp.where(diag, b_col, 0.0), axis=0, keepdims=True)
        a_row = li_row - b_row
        a_col = jnp.sum(jnp.where(diag, a_row, 0.0), axis=1, keepdims=True)
        total = jnp.sum(lf_row, axis=1, keepdims=True)
        log_d = jnp.where(causal, b_col + a_row, NEG_INF)
        log_init = b_col + m_prev
        m_t = jnp.maximum(log_init, jnp.max(log_d, axis=1, keepdims=True))
        d_mat = jnp.exp(log_d - m_t)
        s = lax.dot_general(qc, kc, _NT, preferred_element_type=F32) * d_mat
        num = jnp.dot(s.astype(BF16), vc, preferred_element_type=F32)
        den = jnp.sum(s, axis=1, keepdims=True)
        if not first:
            w_init = jnp.exp(log_init - m_t)
            num = num + w_init * jnp.dot(qc, cst_ref[d].astype(BF16), preferred_element_type=F32)
            den = den + w_init * jnp.sum(qc.astype(F32) * nst_ref[d], axis=1, keepdims=True)
        hc = num / jnp.maximum(jnp.abs(den), jnp.exp(-m_t))
        (hf_ref if d == 0 else hb_ref)[rows, :] = hc
        log_w = total + a_col
        m_new = jnp.maximum(total + m_prev, jnp.max(log_w, axis=0, keepdims=True))
        kw = kc.astype(F32) * jnp.exp(log_w - m_new)
        c_add = lax.dot_general(kw.astype(BF16), vc, _TN, preferred_element_type=F32)
        n_add = jnp.sum(kw, axis=0, keepdims=True)
        if first:
            cst_ref[d] = c_add
            nst_ref[d] = n_add
        else:
            w_0 = jnp.exp(total + m_prev - m_new)
            cst_ref[d] = w_0 * cst_ref[d] + c_add
            nst_ref[d] = w_0 * nst_ref[d] + n_add
        return m_new

    if has_init:
        for d in range(2):
            cst_ref[d] = c0_ref[0, 0, d, 0]
            nst_ref[d] = n0_ref[0, 0, d, 0]
        m = [jnp.full((1, 1), m0_ref[b, layer, d, h], F32) for d in range(2)]
    else:
        m = [jnp.zeros((1, 1), F32) for _ in range(2)]
    for step in range(nc):
        first = (step == 0) and not has_init
        m[0] = chunk_step(step, 0, m[0], first)
        m[1] = chunk_step(nc - 1 - step, 1, m[1], first)

    og = jax.nn.sigmoid(o_ref[...].astype(F32))
    hh = _rms(og * (hf_ref[...] + hb_ref[...]))
    h_ref[...] = (hh * nw_ref[...]).astype(h_ref.dtype)

    if emit_state:
        row = lax.broadcasted_iota(jnp.int32, (8, LANES), 0)
        mst_out[0, 0] = jnp.where(row == 0, m[0], m[1])
        if has_prev:
            for d in range(2):
                cst_out[0, 0, d, 0] = cprev_ref[0, d, 0]
                nst_out[0, 0, d, 0] = nprev_ref[0, d, 0]
                cst_out[0, 1, d, 0] = cst_ref[d]
                nst_out[0, 1, d, 0] = nst_ref[d]
        else:
            for d in range(2):
                cst_out[0, d, 0] = cst_ref[d]
                nst_out[0, d, 0] = nst_ref[d]


def _mlstm(main, gates, conv_w, gate_bias, norm_w, init, layer, batch, seq, emit_state, prev):
    chunk = B_CHUNK
    hd = B_HEAD_DIM
    g4 = gates.reshape(batch, seq, 4, B_HEADS).transpose(0, 3, 2, 1)
    bias = gate_bias.reshape(4, B_HEADS).T
    has_init = init is not None
    has_prev = prev is not None

    def col(off):
        return lambda b, h: (b, off // hd + h)

    in_specs = [pl.BlockSpec(memory_space=pltpu.SMEM)]
    args = [bias]
    if has_init:
        c0, n0, m0 = init
        in_specs.append(pl.BlockSpec(memory_space=pltpu.SMEM))
        args.append(m0)
    in_specs += [pl.BlockSpec((seq, hd), col(MAIN_QB)),
                 pl.BlockSpec((seq, hd), col(MAIN_KB)),
                 pl.BlockSpec((seq, hd), col(MAIN_VB)),
                 pl.BlockSpec((seq, hd), col(MAIN_OB)),
                 pl.BlockSpec((1, 1, 4, seq), lambda b, h: (b, h, 0, 0)),
                 pl.BlockSpec((3, hd), lambda b, h: (0, h)),
                 pl.BlockSpec((3, hd), lambda b, h: (0, B_HEADS + h)),
                 pl.BlockSpec((1, hd), lambda b, h: (0, h))]
    args += [main, main, main, main, g4, conv_w, conv_w, norm_w.reshape(1, B_WIDTH)]
    if has_init:
        in_specs += [pl.BlockSpec((1, 1, 2, 1, hd, hd), lambda b, h: (b, layer, 0, h, 0, 0)),
                     pl.BlockSpec((1, 1, 2, 1, 1, hd), lambda b, h: (b, layer, 0, h, 0, 0))]
        args += [c0, n0.reshape(n0.shape[:4] + (1, hd))]
    if has_prev:
        in_specs += [pl.BlockSpec((1, 2, 1, hd, hd), lambda b, h: (b, 0, h, 0, 0)),
                     pl.BlockSpec((1, 2, 1, 1, hd), lambda b, h: (b, 0, h, 0, 0))]
        args += list(prev)
    out_specs = [pl.BlockSpec((seq, hd), lambda b, h: (b, h))]
    out_shape = [jax.ShapeDtypeStruct((batch * seq, B_WIDTH), BF16)]
    if emit_state:
        if has_prev:
            out_specs += [pl.BlockSpec((1, N_EVEN, 2, 1, hd, hd), lambda b, h: (b, 0, 0, h, 0, 0)),
                          pl.BlockSpec((1, N_EVEN, 2, 1, 1, hd), lambda b, h: (b, 0, 0, h, 0, 0))]
            out_shape += [jax.ShapeDtypeStruct((batch, N_EVEN, 2, B_HEADS, hd, hd), F32),
                          jax.ShapeDtypeStruct((batch, N_EVEN, 2, B_HEADS, 1, hd), F32)]
        else:
            out_specs += [pl.BlockSpec((1, 2, 1, hd, hd), lambda b, h: (b, 0, h, 0, 0)),
                          pl.BlockSpec((1, 2, 1, 1, hd), lambda b, h: (b, 0, h, 0, 0))]
            out_shape += [jax.ShapeDtypeStruct((batch, 2, B_HEADS, hd, hd), F32),
                          jax.ShapeDtypeStruct((batch, 2, B_HEADS, 1, hd), F32)]
        out_specs.append(pl.BlockSpec((1, 1, 8, LANES), lambda b, h: (b, h, 0, 0)))
        out_shape.append(jax.ShapeDtypeStruct((batch, B_HEADS, 8, LANES), F32))
    outs = pl.pallas_call(
        functools.partial(_mlstm_kernel, seq=seq, chunk=chunk, layer=layer, has_init=has_init,
                          emit_state=emit_state, has_prev=has_prev),
        grid=(batch, B_HEADS),
        in_specs=in_specs,
        out_specs=out_specs,
        out_shape=out_shape,
        scratch_shapes=[pltpu.VMEM((seq, hd), BF16), pltpu.VMEM((seq, hd), BF16),
                        pltpu.VMEM((seq, hd), F32), pltpu.VMEM((seq, hd), F32),
                        pltpu.VMEM((2, hd, hd), F32), pltpu.VMEM((2, 1, hd), F32)],
        compiler_params=_cparams(2),
        name="mlstm",
    )(*args)
    if not emit_state:
        return outs[0], None
    hb, cst, nst, mst = outs
    mstate = mst[:, :, 0:2, 0].transpose(0, 2, 1)
    return hb, (cst, nst, mstate)


def _outproj_kernel(*refs, n_branch, final):
    it = iter(refs)
    branches = [(next(it), next(it), next(it)) for _ in range(n_branch)]
    w_ref, x_ref, mod_ref = next(it), next(it), next(it)
    fn_ref = next(it) if final else None
    y_ref, wb_ref = next(it), next(it)

    @pl.when(pl.program_id(0) == 0)
    def _():
        wb_ref[...] = w_ref[0].astype(BF16)

    out = None
    for n, (a_ref, zlo_ref, zhi_ref) in enumerate(branches):
        base = n * 2 * Z_BLOCK
        for part, z_ref in enumerate((zlo_ref, zhi_ref)):
            cols = slice(part * Z_BLOCK, (part + 1) * Z_BLOCK)
            gated = (a_ref[:, cols].astype(F32) * _silu(z_ref[...].astype(F32))).astype(BF16)
            w = wb_ref[base + part * Z_BLOCK:base + (part + 1) * Z_BLOCK, :]
            p = jnp.dot(gated, w, preferred_element_type=F32)
            out = p if out is None else out + p
    y = x_ref[...] + mod_ref[0, 2:3, :] * out
    if final:
        y = _rms(y) * fn_ref[...]
    y_ref[...] = y


def _outproj(branches, w_out, layer, x, mod3, rows_per_mod, final_norm):
    m = x.shape[0]
    tm = 512
    if rows_per_mod is None:
        mod_map = lambda i: (0, 0, 0)
    else:
        mod_map = lambda i: (1 + (i * tm) // rows_per_mod, 0, 0)
    in_specs, args = [], []
    for a, z, z_off in branches:
        zb = z_off // Z_BLOCK
        in_specs += [pl.BlockSpec((tm, 2 * Z_BLOCK), lambda i: (i, 0)),
                     pl.BlockSpec((tm, Z_BLOCK), lambda i, zb=zb: (i, zb)),
                     pl.BlockSpec((tm, Z_BLOCK), lambda i, zb=zb: (i, zb + 1))]
        args += [a, z, z]
    wk = w_out.shape[1]
    in_specs += [pl.BlockSpec((1, wk, D_MODEL), lambda i: (layer, 0, 0)),
                 pl.BlockSpec((tm, D_MODEL), lambda i: (i, 0)),
                 pl.BlockSpec((1, 3, D_MODEL), mod_map)]
    args += [w_out, x, mod3]
    final = final_norm is not None
    if final:
        in_specs.append(pl.BlockSpec((1, D_MODEL), lambda i: (0, 0)))
        args.append(final_norm.reshape(1, D_MODEL))
    return pl.pallas_call(
        functools.partial(_outproj_kernel, n_branch=len(branches), final=final),
        grid=(m // tm,),
        in_specs=in_specs,
        out_specs=pl.BlockSpec((tm, D_MODEL), lambda i: (i, 0)),
        out_shape=jax.ShapeDtypeStruct((m, D_MODEL), F32),
        scratch_shapes=[pltpu.VMEM((wk, D_MODEL), BF16)],
        compiler_params=_cparams(1),
        name="outproj",
    )(*args)


def _inproj_c_kernel(*refs, rope):
    it = iter(refs)
    x_ref, mod_ref, g_ref, w_ref, qn_ref, wqb_ref, kvn_ref, wkvb_ref = (next(it) for _ in range(8))
    cos_ref, sin_ref = (next(it), next(it)) if rope else (None, None)
    q_ref, kv_ref, z_ref, ckv_ref, kr_ref, krb_ref = (next(it) for _ in range(6))

    hn = _norm_modulate(x_ref[...], g_ref[...], mod_ref).astype(BF16)
    r = jnp.dot(hn, w_ref[...], preferred_element_type=F32)
    qa = _rms(r[:, CIN_QA:CIN_KVA]) * qn_ref[...]
    ckv = _rms(r[:, CIN_KVA:CIN_Z]) * kvn_ref[...]
    z_ref[...] = r[:, CIN_Z:CIN_KR].astype(BF16)
    kr = r[:, CIN_KR:CIN_WIDTH]
    ckv_ref[...] = ckv
    kr_ref[...] = kr
    kv_ref[...] = jnp.dot(ckv.astype(BF16), wkvb_ref[...], preferred_element_type=F32).astype(BF16)
    scale = (C_NOPE + C_ROPE) ** -0.5 * LOG2E
    q = jnp.dot(qa.astype(BF16), wqb_ref[...], preferred_element_type=F32) * scale
    half = C_ROPE // 4
    if rope:
        cos = cos_ref[...]
        sin = sin_ref[...]
        kr = _rope(kr, cos, sin, half)
        for hd in range(C_HEADS):
            hs = slice(hd * C_HEAD_PAD, (hd + 1) * C_HEAD_PAD)
            q_ref[:, hs] = _rope(q[:, hs], cos, sin, half).astype(BF16)
    else:
        q_ref[...] = q.astype(BF16)
    krb_ref[...] = kr.astype(BF16)


def _inproj_c(x, mod3, g, w_in, q_norm, w_qb, kv_norm, w_kvb, rows_per_mod, tables):
    m = x.shape[0]
    tm = 512
    rope = tables is not None
    if rows_per_mod is None:
        mod_map = lambda i: (0, 0, 0)
    else:
        mod_map = lambda i: (1 + (i * tm) // rows_per_mod, 0, 0)
    const = lambda i: (0, 0)
    qw = C_HEADS * C_HEAD_PAD
    kvw = C_HEADS * (C_NOPE + C_VDIM)
    in_specs = [pl.BlockSpec((tm, D_MODEL), lambda i: (i, 0)),
                pl.BlockSpec((1, 3, D_MODEL), mod_map),
                pl.BlockSpec((1, D_MODEL), const),
                pl.BlockSpec((D_MODEL, CIN_WIDTH), const),
                pl.BlockSpec((1, C_Q_RANK), const),
                pl.BlockSpec((C_Q_RANK, qw), const),
                pl.BlockSpec((1, C_KV_RANK), const),
                pl.BlockSpec((C_KV_RANK, kvw), const)]
    args = [x, mod3, g, w_in, q_norm, w_qb, kv_norm, w_kvb]
    if rope:
        tpb = rows_per_mod // tm
        in_specs += [pl.BlockSpec((tm, LANES), lambda i: (i % tpb, 0))] * 2
        args += list(tables)
    return pl.pallas_call(
        functools.partial(_inproj_c_kernel, rope=rope),
        grid=(m // tm,),
        in_specs=in_specs,
        out_specs=[pl.BlockSpec((tm, qw), lambda i: (i, 0)),
                   pl.BlockSpec((tm, kvw), lambda i: (i, 0)),
                   pl.BlockSpec((tm, C_WIDTH), lambda i: (i, 0)),
                   pl.BlockSpec((tm, C_KV_RANK), lambda i: (i, 0)),
                   pl.BlockSpec((tm, LANES), lambda i: (i, 0)),
                   pl.BlockSpec((tm, LANES), lambda i: (i, 0))],
        out_shape=[jax.ShapeDtypeStruct((m, qw), BF16),
                   jax.ShapeDtypeStruct((m, kvw), BF16),
                   jax.ShapeDtypeStruct((m, C_WIDTH), BF16),
                   jax.ShapeDtypeStruct((m, C_KV_RANK), F32),
                   jax.ShapeDtypeStruct((m, LANES), F32),
                   jax.ShapeDtypeStruct((m, LANES), BF16)],
        compiler_params=_cparams(1),
        name="inproj_c",
    )(*args)


def _matmul_kernel(x_ref, w_ref, o_ref):
    o_ref[...] = jnp.dot(x_ref[...], w_ref[...], preferred_element_type=F32).astype(o_ref.dtype)


def _matmul(x, w, tm):
    m, k = x.shape
    n = w.shape[1]
    return pl.pallas_call(
        _matmul_kernel,
        grid=(m // tm,),
        in_specs=[pl.BlockSpec((tm, k), lambda i: (i, 0)),
                  pl.BlockSpec((k, n), lambda i: (0, 0))],
        out_specs=pl.BlockSpec((tm, n), lambda i: (i, 0)),
        out_shape=jax.ShapeDtypeStruct((m, n), BF16),
        compiler_params=_cparams(1),
        name="matmul",
    )(x, w)


def _mla_slabs(kv, kr_slab):
    lane = lax.broadcasted_iota(jnp.int32, kv.shape, 1)
    nope = lane < C_NOPE
    return jnp.where(nope, kv, kr_slab), jnp.where(nope, jnp.ones_like(kv), kv)


def _attn_c_kernel(*refs, has_ctx):
    it = iter(refs)
    q_ref, kv_ref, kr_ref = next(it), next(it), next(it)
    kvc_ref, krc_ref = (next(it), next(it)) if has_ctx else (None, None)
    o_ref = next(it)
    kown_ref, vown_ref = next(it), next(it)
    kctx_ref, vctx_ref = (next(it), next(it)) if has_ctx else (None, None)

    @pl.when(pl.program_id(1) == 0)
    def _():
        for h in range(C_HEADS):
            hs = slice(h * C_HEAD_PAD, (h + 1) * C_HEAD_PAD)
            kown_ref[:, hs], vown_ref[:, hs] = _mla_slabs(kv_ref[:, hs], kr_ref[...])
            if has_ctx:
                kctx_ref[:, hs], vctx_ref[:, hs] = _mla_slabs(kvc_ref[:, hs], krc_ref[...])

    for h in range(C_HEADS):
        hs = slice(h * C_HEAD_PAD, (h + 1) * C_HEAD_PAD)
        qh = q_ref[:, hs]
        scores = [lax.dot_general(qh, kown_ref[:, hs], _NT, preferred_element_type=F32)]
        values = [vown_ref[:, hs]]
        if has_ctx:
            scores.append(lax.dot_general(qh, kctx_ref[:, hs], _NT, preferred_element_type=F32))
            values.append(vctx_ref[:, hs])
        o = _softmax_pv(scores, values, None)
        o_ref[:, h * C_VDIM:(h + 1) * C_VDIM] = o[:, C_NOPE:].astype(o_ref.dtype)


def _attn_c_dense_kernel(q_ref, kv_ref, kr_ref, o_ref, s_ref, e_ref, ks_ref, vs_ref):
    for h in range(C_HEADS):
        hs = slice(h * C_HEAD_PAD, (h + 1) * C_HEAD_PAD)
        ks_ref[:, hs], vs_ref[:, hs] = _mla_slabs(kv_ref[:, hs], kr_ref[...])
    for h in range(C_HEADS):
        hs = slice(h * C_HEAD_PAD, (h + 1) * C_HEAD_PAD)
        s_ref[h] = lax.dot_general(q_ref[:, hs], ks_ref[:, hs], _NT, preferred_element_type=F32)
    s = s_ref[...]
    e_ref[...] = jnp.exp2(s - jnp.max(s, axis=-1, keepdims=True)).astype(BF16)
    for h in range(C_HEADS):
        hs = slice(h * C_HEAD_PAD, (h + 1) * C_HEAD_PAD)
        res = jnp.dot(e_ref[h], vs_ref[:, hs], preferred_element_type=F32)
        o = res / pltpu.roll(res, LANES // 2, 1)
        o_ref[:, h * C_VDIM:(h + 1) * C_VDIM] = o[:, C_NOPE:].astype(o_ref.dtype)


def _attn_c_dense(q, kv, kr_slab, batch, seq):
    w = C_HEADS * C_HEAD_PAD
    return pl.pallas_call(
        _attn_c_dense_kernel,
        grid=(batch,),
        in_specs=[pl.BlockSpec((seq, w), lambda b: (b, 0)),
                  pl.BlockSpec((seq, w), lambda b: (b, 0)),
                  pl.BlockSpec((seq, LANES), lambda b: (b, 0))],
        out_specs=pl.BlockSpec((seq, C_WIDTH), lambda b: (b, 0)),
        out_shape=jax.ShapeDtypeStruct((batch * seq, C_WIDTH), BF16),
        scratch_shapes=[pltpu.VMEM((C_HEADS, seq, seq), F32), pltpu.VMEM((C_HEADS, seq, seq), BF16),
                        pltpu.VMEM((seq, w), BF16), pltpu.VMEM((seq, w), BF16)],
        compiler_params=_cparams(1),
        name="attn_c_dense",
    )(q, kv, kr_slab)


def _attn_c(q, kv, kr_slab, ctx, batch, seq, tq):
    nq = seq // tq
    w = C_HEADS * C_HEAD_PAD
    has_ctx = ctx is not None
    in_specs = [pl.BlockSpec((tq, w), lambda b, i: (b * nq + i, 0)),
                pl.BlockSpec((seq, w), lambda b, i: (b, 0)),
                pl.BlockSpec((seq, LANES), lambda b, i: (b, 0))]
    args = [q, kv, kr_slab]
    scratch = [pltpu.VMEM((seq, w), BF16), pltpu.VMEM((seq, w), BF16)]
    if has_ctx:
        kv_ctx, kr_ctx = ctx
        nctx = kv_ctx.shape[0] // batch
        in_specs += [pl.BlockSpec((nctx, w), lambda b, i: (b, 0)),
                     pl.BlockSpec((nctx, LANES), lambda b, i: (b, 0))]
        args += [kv_ctx, kr_ctx]
        scratch += [pltpu.VMEM((nctx, w), BF16), pltpu.VMEM((nctx, w), BF16)]
    return pl.pallas_call(
        functools.partial(_attn_c_kernel, has_ctx=has_ctx),
        grid=(batch, nq),
        in_specs=in_specs,
        out_specs=pl.BlockSpec((tq, C_WIDTH), lambda b, i: (b * nq + i, 0)),
        out_shape=jax.ShapeDtypeStruct((batch * seq, C_WIDTH), BF16),
        scratch_shapes=scratch,
        compiler_params=_cparams(2),
        name="attn_c",
    )(*args)


def _rope_tables(n_tokens):
    pos_r = np.repeat(np.arange(n_tokens // GRID_W), GRID_W).astype(np.float64)
    pos_c = np.tile(np.arange(GRID_W), n_tokens // GRID_W).astype(np.float64)

    def seg(d_axis):
        half = d_axis // 2
        freqs = np.power(ROPE_BASE, -np.arange(half, dtype=np.float64) / half)
        cos, sin = [], []
        for pos in (pos_r, pos_c):
            ang = pos[:, None] * freqs[None, :]
            cos += [np.cos(ang), np.cos(ang)]
            sin += [-np.sin(ang), np.sin(ang)]
        return np.concatenate(cos, axis=1), np.concatenate(sin, axis=1)

    cos_a, sin_a = seg(A_HEAD_DIM // 2)
    cos_a, sin_a = np.tile(cos_a, (1, 2)), np.tile(sin_a, (1, 2))
    cos_r, sin_r = seg(C_ROPE // 2)
    ones = np.ones((n_tokens, C_NOPE))
    pad = C_HEAD_PAD - C_NOPE - C_ROPE
    cos_c = np.concatenate([ones, cos_r, np.ones((n_tokens, pad))], axis=1)
    sin_c = np.concatenate([0 * ones, sin_r, np.zeros((n_tokens, pad))], axis=1)
    f = lambda a: jnp.asarray(a, dtype=F32)
    return (f(cos_a), f(sin_a)), (f(cos_c), f(sin_c))


def _pad_cols(w, left, total):
    return jnp.pad(w, ((0, 0), (left, total - left - w.shape[1])))


def kernel(x_prompt, x_sample, cache_a_k, cache_a_v, state_b_mem, state_b_norm, state_b_max, cache_c_kv, cache_c_krope, c, c_ctx, norm_g, w_mod, b_mod, w_in_ab, sink_a, conv_b, gate_bias_b, norm_b, w_out_ab, w_in_c, q_norm_c, w_qb_c, kv_norm_c, w_kvb_c, w_out_c, final_norm):
    bp, tp, _ = x_prompt.shape
    bs, ts, _ = x_sample.shape
    past = cache_a_k.shape[2]
    tables_a, tables_c = _rope_tables(ts)
    w_ab_t = jnp.swapaxes(w_in_ab, 1, 2)

    cond = jnp.zeros((16, D_MODEL), F32).at[0].set(c_ctx).at[1:1 + bs].set(c)
    mods = _adaln(cond, w_mod, b_mod).reshape(DEPTH, 16, 3, D_MODEL)

    yp = x_prompt.reshape(bp * tp, D_MODEL)
    ys = x_sample.reshape(bs * ts, D_MODEL)
    a_k, a_v, b_max, c_kvs, c_krs = [], [], [], [], []
    states = None
    for l in range(DEPTH):
        j = l // 2
        mod3 = mods[l]
        g = norm_g[l].reshape(1, D_MODEL)
        fin = final_norm if l == DEPTH - 1 else None
        if l % 2 == 0:
            w_bf = _cast_rows(w_ab_t, j, MAIN_WIDTH)
            main_p, k_p, v_p, gates_p = _inproj_ab(yp, mod3, g, w_bf, w_ab_t, j, None, True, None)
            main_s, gates_s = _inproj_ab(ys, mod3, g, w_bf, w_ab_t, j, ts, False, tables_a)
            a_k.append(k_p.reshape(bp, tp, A_KV_HEADS, A_HEAD_DIM))
            a_v.append(v_p.reshape(bp, tp, A_KV_HEADS, A_HEAD_DIM))

            attn_p = _attn_a_prompt(main_p, sink_a[j], bp, tp)
            ck = cache_a_k[:, j].reshape(bs, past, A_KV_WIDTH).astype(BF16)
            cv = cache_a_v[:, j].reshape(bs, past, A_KV_WIDTH).astype(BF16)
            attn_s = _attn_a_sample(main_s, sink_a[j], ck, cv, bs, ts)

            prev = None if states is None else states[:2]
            hb_p, states = _mlstm(main_p, gates_p, conv_b[j], gate_bias_b[j], norm_b[j],
                                  None, j, bp, tp, True, prev)
            init = (state_b_mem, state_b_norm, state_b_max)
            hb_s, _ = _mlstm(main_s, gates_s, conv_b[j], gate_bias_b[j], norm_b[j],
                             init, j, bs, ts, False, None)
            b_max.append(states[2])

            yp = _outproj([(attn_p, main_p, MAIN_ZA), (hb_p, main_p, MAIN_ZB)], w_out_ab, j, yp, mod3, None, fin)
            ys = _outproj([(attn_s, main_s, MAIN_ZA), (hb_s, main_s, MAIN_ZB)], w_out_ab, j, ys, mod3, ts, fin)
        else:
            w = w_in_c[j]
            o_kva = C_Q_RANK
            o_kr = C_Q_RANK + C_KV_RANK
            o_z = o_kr + C_ROPE
            w_in = jnp.concatenate([w[:, :o_kva], w[:, o_kva:o_kr], w[:, o_z:],
                                    _pad_cols(w[:, o_kr:o_z], C_NOPE, C_HEAD_PAD)], axis=1).astype(BF16)
            wq = w_qb_c[j].reshape(C_Q_RANK, C_HEADS, C_NOPE + C_ROPE)
            wq = jnp.pad(wq, ((0, 0), (0, 0), (0, C_HEAD_PAD - C_NOPE - C_ROPE)))
            wq = wq.reshape(C_Q_RANK, C_HEADS * C_HEAD_PAD).astype(BF16)
            wkv = w_kvb_c[j].astype(BF16)
            qn = q_norm_c[j].reshape(1, C_Q_RANK)
            kvn = kv_norm_c[j].reshape(1, C_KV_RANK)

            q_p, kv_p, z_p, ckv_p, kr_p, krb_p = _inproj_c(yp, mod3, g, w_in, qn, wq, kvn, wkv, None, None)
            q_s, kv_s, z_s, _, _, krb_s = _inproj_c(ys, mod3, g, w_in, qn, wq, kvn, wkv, ts, tables_c)
            c_kvs.append(ckv_p.reshape(bp, tp, C_KV_RANK))
            c_krs.append(kr_p[:, C_NOPE:C_NOPE + C_ROPE].reshape(bp, tp, C_ROPE))

            cc = cache_c_kv[:, j].reshape(bs * past, C_KV_RANK).astype(BF16)
            kv_ctx = _matmul(cc, wkv, 512)
            kr_ctx = _pad_cols(cache_c_krope[:, j].reshape(bs * past, C_ROPE), C_NOPE, C_HEAD_PAD).astype(BF16)

            attn_p = _attn_c_dense(q_p, kv_p, krb_p, bp, tp)
            attn_s = _attn_c(q_s, kv_s, krb_s, (kv_ctx, kr_ctx), bs, ts, 512)

            yp = _outproj([(attn_p, z_p, 0)], w_out_c, j, yp, mod3, None, fin)
            ys = _outproj([(attn_s, z_s, 0)], w_out_c, j, ys, mod3, ts, fin)

    b_mem = states[0]
    b_nrm = states[1].reshape(bp, N_EVEN, 2, B_HEADS, B_HEAD_DIM)
    return (yp.reshape(bp, tp, D_MODEL), ys.reshape(bs, ts, D_MODEL),
            jnp.stack(a_k, axis=1), jnp.stack(a_v, axis=1), b_mem, b_nrm,
            jnp.stack(b_max, axis=1), jnp.stack(c_kvs, axis=1), jnp.stack(c_krs, axis=1))
```

```python
import functools
import math

import numpy as np
import jax
import jax.numpy as jnp
from jax import lax
from jax.experimental import pallas as pl
from jax.experimental.pallas import tpu as pltpu

F32 = jnp.float32
BF16 = jnp.bfloat16

D_MODEL = 1024
DEPTH = 4
N_EVEN = 2
EPS = 1e-6
ROPE_BASE = 10000.0
NEG_INF = -1e30
GRID_W = 64
LOG2E = math.log2(math.e)
A_HEADS = 16
A_KV_HEADS = 4
A_GROUP = A_HEADS // A_KV_HEADS
A_HEAD_DIM = 64
A_WIDTH = A_HEADS * A_HEAD_DIM
A_KV_WIDTH = A_KV_HEADS * A_HEAD_DIM
WINDOW = 128
BLOCK = 128
B_HEADS = 4
B_HEAD_DIM = 256
B_WIDTH = B_HEADS * B_HEAD_DIM
B_CHUNK = 256
C_HEADS = 16
C_NOPE = 64
C_ROPE = 32
C_VDIM = 64
C_Q_RANK = 384
C_KV_RANK = 256
C_WIDTH = C_HEADS * C_VDIM
C_HEAD_PAD = 128

LANES = 128
SUBLANES = 8
B_CONV = 3
MAIN_QA, MAIN_KA, MAIN_VA, MAIN_ZA, MAIN_QB, MAIN_KB, MAIN_VB, MAIN_OB, MAIN_ZB = (
    0, 1024, 1280, 1536, 2560, 3584, 4608, 5632, 6656)
MAIN_WIDTH = 7680
CIN_QA, CIN_KVA, CIN_Z, CIN_KR, CIN_WIDTH = 0, 384, 640, 1664, 1792
Z_BLOCK = 512
QK_TILE = 1280
A_QSCALE = A_HEAD_DIM ** -0.5 * LOG2E

MIB = 1024 * 1024
VMEM_LIMIT = 48 * MIB
VMEM_LIMIT_WIDE = 56 * MIB

_NT = (((1,), (1,)), ((), ()))
_TN = (((0,), (0,)), ((), ()))


def _cparams(n_axes, vmem=VMEM_LIMIT):
    return pltpu.CompilerParams(dimension_semantics=("arbitrary",) * n_axes,
                                vmem_limit_bytes=vmem)


def _silu(x):
    return x * jax.nn.sigmoid(x)


def _log_sigmoid(x):
    return jnp.minimum(x, 0.0) - jnp.log1p(jnp.exp(-jnp.abs(x)))


def _rms(x):
    return x * lax.rsqrt(jnp.mean(x * x, axis=-1, keepdims=True) + EPS)


def _norm_modulate(x, g, mod):
    y = _rms(x) * g
    return y * (1.0 + mod[1:2, :]) + mod[0:1, :]


def _swap_halves(x, half):
    lane = lax.broadcasted_iota(jnp.int32, x.shape, 1)
    first = (lane % (2 * half)) < half
    return jnp.where(first, pltpu.roll(x, LANES - half, 1), pltpu.roll(x, half, 1))


def _rope(x, cos, sin, half):
    return x * cos + _swap_halves(x, half) * sin


def _softmax_pv(scores, values, sink):
    tiles = [s[:, t * LANES:(t + 1) * LANES] for s in scores for t in range(s.shape[1] // LANES)]
    m = jnp.max(functools.reduce(jnp.maximum, tiles), axis=-1, keepdims=True)
    if sink is not None:
        m = jnp.maximum(m, sink)
    res = functools.reduce(jnp.add, [
        jnp.dot(jnp.exp2(s - m).astype(BF16), v, preferred_element_type=F32)
        for s, v in zip(scores, values)])
    den = pltpu.roll(res, LANES // 2, 1)
    if sink is not None:
        den = den + jnp.exp2(sink - m)
    return res / den


def _adaln_kernel(c_ref, w_ref, b_ref, o_ref):
    a = _silu(c_ref[...]).astype(BF16)
    w = w_ref[0].astype(BF16)
    o_ref[0] = jnp.dot(a, w, preferred_element_type=F32) + b_ref[0]


def _adaln(cond, w_mod, b_mod):
    tn = 1024
    n = 3 * D_MODEL
    return pl.pallas_call(
        _adaln_kernel,
        grid=(DEPTH, n // tn),
        in_specs=[pl.BlockSpec((16, D_MODEL), lambda l, j: (0, 0)),
                  pl.BlockSpec((1, D_MODEL, tn), lambda l, j: (l, 0, j)),
                  pl.BlockSpec((1, 1, tn), lambda l, j: (l, 0, j))],
        out_specs=pl.BlockSpec((1, 16, tn), lambda l, j: (l, 0, j)),
        out_shape=jax.ShapeDtypeStruct((DEPTH, 16, n), F32),
        compiler_params=_cparams(2),
        name="adaln",
    )(cond, w_mod, b_mod.reshape(DEPTH, 1, n))


def _cast_kernel(w_ref, o_ref):
    o_ref[...] = w_ref[0].astype(o_ref.dtype)


def _cast_rows(w_t, layer, rows):
    tr = QK_TILE
    d = w_t.shape[2]
    return pl.pallas_call(
        _cast_kernel,
        grid=(rows // tr,),
        in_specs=[pl.BlockSpec((1, tr, d), lambda i: (layer, i, 0))],
        out_specs=pl.BlockSpec((tr, d), lambda i: (i, 0)),
        out_shape=jax.ShapeDtypeStruct((rows, d), BF16),
        compiler_params=_cparams(1),
        name="cast_rows",
    )(w_t)


def _main_chunk_kind(col):
    bounds = ((MAIN_KA, "q"), (MAIN_VA, "k"), (MAIN_ZA, "plain"), (MAIN_QB, "silu"), (MAIN_OB, "plain"),
              (MAIN_ZB, "sigmoid"), (MAIN_WIDTH, "silu"))
    return next(kind for end, kind in bounds if col < end)


def _inproj_ab_kernel(*refs, emit_kv, rope):
    it = iter(refs)
    x_ref, mod_ref, g_ref, wb_ref, wg_ref = (next(it) for _ in range(5))
    cos_ref, sin_ref = (next(it), next(it)) if rope else (None, None)
    main_ref = next(it)
    k_ref, v_ref = (next(it), next(it)) if emit_kv else (None, None)
    gate_ref = next(it)

    mod = mod_ref[0, 0]
    g = g_ref[0]
    hn = _norm_modulate(x_ref[...], g, mod).astype(BF16)
    gate_ref[...] = lax.dot_general(wg_ref[0].astype(BF16), hn, _NT, preferred_element_type=F32)
    if emit_kv:
        kv = lax.dot_general(hn, wb_ref[MAIN_KA:MAIN_ZA, :], _NT, preferred_element_type=F32)
        k_ref[...] = kv[:, :A_KV_WIDTH]
        v_ref[...] = kv[:, A_KV_WIDTH:]
    for t in range(MAIN_WIDTH // QK_TILE):
        res = lax.dot_general(hn, wb_ref[t * QK_TILE:(t + 1) * QK_TILE, :], _NT, preferred_element_type=F32)
        for c in range(QK_TILE // LANES):
            col = t * QK_TILE + c * LANES
            kind = _main_chunk_kind(col)
            chunk = res[:, c * LANES:(c + 1) * LANES]
            if kind in ("q", "k"):
                if rope:
                    chunk = _rope(chunk, cos_ref[...], sin_ref[...], A_HEAD_DIM // 4)
                if kind == "q":
                    chunk = chunk * A_QSCALE
            elif kind == "silu":
                chunk = _silu(chunk)
            elif kind == "sigmoid":
                chunk = jax.nn.sigmoid(chunk)
            main_ref[:, col:col + LANES] = chunk.astype(BF16)


def _inproj_ab(x, mods, norm_g, depth_idx, w_bf, w_t, layer, seq, shared_mod, emit_kv, tables):
    m = x.shape[0]
    tm = 512
    rope = tables is not None
    if shared_mod:
        mod_map = lambda i: (depth_idx, 0, 0, 0)
    else:
        mod_map = lambda i: (depth_idx, 1 + (i * tm) // seq, 0, 0)
    n_gate = 4 * B_HEADS
    tiles_per_table = max(seq, tm) // tm
    table_spec = pl.BlockSpec((tm, LANES), lambda i: (i % tiles_per_table, 0))
    in_specs = [pl.BlockSpec((tm, D_MODEL), lambda i: (i, 0)),
                pl.BlockSpec((1, 1, 3, D_MODEL), mod_map),
                pl.BlockSpec((1, 1, D_MODEL), lambda i: (depth_idx, 0, 0)),
                pl.BlockSpec((MAIN_WIDTH, D_MODEL), lambda i: (0, 0), pipeline_mode=pl.Buffered(1)),
                pl.BlockSpec((1, n_gate, D_MODEL), lambda i: (layer, MAIN_WIDTH // n_gate, 0))]
    args = [x, mods, norm_g, w_bf, w_t]
    if rope:
        in_specs += [table_spec] * 2
        args += list(tables)
    out_specs = [pl.BlockSpec((tm, MAIN_WIDTH), lambda i: (i, 0))]
    out_shape = [jax.ShapeDtypeStruct((m, MAIN_WIDTH), BF16)]
    if emit_kv:
        out_specs += [pl.BlockSpec((tm, A_KV_WIDTH), lambda i: (i, 0))] * 2
        out_shape += [jax.ShapeDtypeStruct((m, A_KV_WIDTH), F32)] * 2
    out_specs.append(pl.BlockSpec((n_gate, tm), lambda i: (0, i)))
    out_shape.append(jax.ShapeDtypeStruct((n_gate, m), F32))
    return pl.pallas_call(
        functools.partial(_inproj_ab_kernel, emit_kv=emit_kv, rope=rope),
        grid=(m // tm,),
        in_specs=in_specs,
        out_specs=out_specs,
        out_shape=out_shape,
        compiler_params=_cparams(1, VMEM_LIMIT_WIDE),
        name="inproj_ab",
    )(*args)


def _value_slabs(v_ref, slab_ref):
    keys = v_ref.shape[0]
    ones = jnp.ones((keys, LANES - A_HEAD_DIM), BF16)
    for g in range(A_KV_HEADS):
        slab_ref[:, g * LANES:g * LANES + A_HEAD_DIM] = v_ref[:, g * A_HEAD_DIM:(g + 1) * A_HEAD_DIM]
        slab_ref[:, g * LANES + A_HEAD_DIM:(g + 1) * LANES] = ones


def _group_queries(q_ref, g):
    return jnp.concatenate([q_ref[:, (g * A_GROUP + hh) * A_HEAD_DIM:(g * A_GROUP + hh + 1) * A_HEAD_DIM]
                            for hh in range(A_GROUP)], axis=0)


def _sink_softmax(s_ref, e_ref, t_ref, sink_ref):
    s = s_ref[...]
    sink = sink_ref[...] * LOG2E
    m = jnp.maximum(jnp.max(s, axis=-1, keepdims=True), sink)
    e_ref[...] = jnp.exp2(s - m).astype(BF16)
    t_ref[...] = jnp.exp2(sink - m)


def _attn_a_prompt_kernel(sink_ref, q_ref, k_ref, v_ref, o_ref, s_ref, e_ref, t_ref, vs_ref):
    _value_slabs(v_ref, vs_ref)
    seq = q_ref.shape[0]
    rows = A_GROUP * seq
    for g in range(A_KV_HEADS):
        heads = slice(g * A_GROUP, (g + 1) * A_GROUP)
        ks = slice(g * A_HEAD_DIM, (g + 1) * A_HEAD_DIM)
        s = lax.dot_general(_group_queries(q_ref, g), k_ref[:, ks], _NT, preferred_element_type=F32)
        s_ref[heads] = s.reshape(A_GROUP, seq, seq)
    _sink_softmax(s_ref, e_ref, t_ref, sink_ref)
    for g in range(A_KV_HEADS):
        heads = slice(g * A_GROUP, (g + 1) * A_GROUP)
        res = jnp.dot(e_ref[heads].reshape(rows, seq), vs_ref[:, g * LANES:(g + 1) * LANES],
                      preferred_element_type=F32)
        o = res / (pltpu.roll(res, LANES // 2, 1) + t_ref[heads].reshape(rows, 1))
        for hh in range(A_GROUP):
            h = g * A_GROUP + hh
            o_ref[:, h * A_HEAD_DIM:(h + 1) * A_HEAD_DIM] = (
                o[hh * seq:(hh + 1) * seq, :A_HEAD_DIM].astype(o_ref.dtype))


def _attn_a_prompt(main, sink, batch, seq):
    kb = MAIN_KA // A_KV_WIDTH
    vb = MAIN_VA // A_KV_WIDTH
    return pl.pallas_call(
        _attn_a_prompt_kernel,
        grid=(batch,),
        in_specs=[pl.BlockSpec((A_HEADS, 1, 1), lambda b: (0, 0, 0)),
                  pl.BlockSpec((seq, A_WIDTH), lambda b: (b, MAIN_QA // A_WIDTH)),
                  pl.BlockSpec((seq, A_KV_WIDTH), lambda b: (b, kb)),
                  pl.BlockSpec((seq, A_KV_WIDTH), lambda b: (b, vb))],
        out_specs=pl.BlockSpec((seq, A_WIDTH), lambda b: (b, 0)),
        out_shape=jax.ShapeDtypeStruct((batch * seq, A_WIDTH), BF16),
        scratch_shapes=[pltpu.VMEM((A_HEADS, seq, seq), F32),
                        pltpu.VMEM((A_HEADS, seq, seq), BF16),
                        pltpu.VMEM((A_HEADS, seq, 1), F32),
                        pltpu.VMEM((seq, A_KV_HEADS * LANES), BF16)],
        compiler_params=_cparams(1),
        name="attn_a_prompt",
    )(sink.reshape(A_HEADS, 1, 1), main, main, main)


def _attn_a_sample_kernel(sink_ref, q_ref, k_ref, v_ref, ck_ref, cv_ref, bias_ref, o_ref, vs_ref, cvs_ref):
    i = pl.program_id(1)
    seq = k_ref.shape[0]
    span = 3 * BLOCK

    @pl.when(i == 0)
    def _():
        _value_slabs(v_ref, vs_ref)
        _value_slabs(cv_ref.at[0], cvs_ref)

    start = pl.multiple_of(jnp.clip((i - 1) * BLOCK, 0, seq - span), BLOCK)
    kw = k_ref[pl.ds(start, span), :]
    vw = vs_ref[pl.ds(start, span), :]
    ck = ck_ref[0]
    bias = bias_ref[(i * BLOCK - start) // BLOCK]
    bias = jnp.concatenate([bias] * A_GROUP, axis=0)
    rows = A_GROUP * BLOCK
    head_of_row = lax.broadcasted_iota(jnp.int32, (rows, 1), 0) // BLOCK
    for g in range(A_KV_HEADS):
        ks = slice(g * A_HEAD_DIM, (g + 1) * A_HEAD_DIM)
        gs = slice(g * LANES, (g + 1) * LANES)
        sink = jnp.zeros((rows, 1), F32)
        for hh in range(A_GROUP):
            sink = jnp.where(head_of_row == hh, sink_ref[g * A_GROUP + hh] * LOG2E, sink)
        qg = _group_queries(q_ref, g)
        s_loc = lax.dot_general(qg, kw[:, ks], _NT, preferred_element_type=F32) + bias
        s_ctx = lax.dot_general(qg, ck[:, ks], _NT, preferred_element_type=F32)
        o = _softmax_pv([s_loc, s_ctx], [vw[:, gs], cvs_ref[:, gs]], sink)
        for hh in range(A_GROUP):
            h = g * A_GROUP + hh
            o_ref[:, h * A_HEAD_DIM:(h + 1) * A_HEAD_DIM] = (
                o[hh * BLOCK:(hh + 1) * BLOCK, :A_HEAD_DIM].astype(o_ref.dtype))


def _window_bias():
    r = np.arange(BLOCK)[:, None]
    c = np.arange(3 * BLOCK)[None, :]
    masks = [np.where(np.abs(off + r - c) <= WINDOW, 0.0, NEG_INF) for off in (0, BLOCK, 2 * BLOCK)]
    return jnp.asarray(np.stack(masks), dtype=F32)


def _attn_a_sample(main, sink, ck, cv, batch, seq):
    nb = seq // BLOCK
    kb = MAIN_KA // A_KV_WIDTH
    vb = MAIN_VA // A_KV_WIDTH
    ctx = ck.shape[1]
    slab_w = A_KV_HEADS * LANES
    return pl.pallas_call(
        _attn_a_sample_kernel,
        grid=(batch, nb),
        in_specs=[pl.BlockSpec(memory_space=pltpu.SMEM),
                  pl.BlockSpec((BLOCK, A_WIDTH), lambda b, i: (b * nb + i, MAIN_QA // A_WIDTH)),
                  pl.BlockSpec((seq, A_KV_WIDTH), lambda b, i: (b, kb)),
                  pl.BlockSpec((seq, A_KV_WIDTH), lambda b, i: (b, vb)),
                  pl.BlockSpec((1, ctx, A_KV_WIDTH), lambda b, i: (b, 0, 0)),
                  pl.BlockSpec((1, ctx, A_KV_WIDTH), lambda b, i: (b, 0, 0)),
                  pl.BlockSpec((3, BLOCK, 3 * BLOCK), lambda b, i: (0, 0, 0))],
        out_specs=pl.BlockSpec((BLOCK, A_WIDTH), lambda b, i: (b * nb + i, 0)),
        out_shape=jax.ShapeDtypeStruct((batch * seq, A_WIDTH), BF16),
        scratch_shapes=[pltpu.VMEM((seq, slab_w), BF16),
                        pltpu.VMEM((ctx, slab_w), BF16)],
        compiler_params=_cparams(2),
        name="attn_a_sample",
    )(sink, main, main, main, ck, cv, _window_bias())


def _mlstm_kernel(*refs, seq, chunk, layer, has_init, emit_state, has_prev):
    it = iter(refs)
    bias_ref = next(it)
    m0_ref = next(it) if has_init else None
    q_ref, k_ref, v_ref, o_ref, g_ref, cwq_ref, cwk_ref, nw_ref = (next(it) for _ in range(8))
    c0_ref, n0_ref = (next(it), next(it)) if has_init else (None, None)
    cprev_ref, nprev_ref = (next(it), next(it)) if has_prev else (None, None)
    h_ref = next(it)
    cst_out, nst_out, mst_out = (next(it), next(it), next(it)) if emit_state else (None, None, None)
    qs_ref, ks_ref, hf_ref, hb_ref, cst_ref, nst_ref = (next(it) for _ in range(6))

    b = pl.program_id(0)
    h = pl.program_id(1)
    nc = seq // chunk

    edge = lax.broadcasted_iota(jnp.int32, (SUBLANES, 1), 0)
    drop_first = (edge != 0).astype(F32)
    drop_last = (edge != SUBLANES - 1).astype(F32)

    def conv_silu(x_ref, w_ref, scale):
        x = x_ref[...].astype(F32)
        w = w_ref[0]
        x_up = jnp.concatenate([x[:seq - SUBLANES], x[seq - SUBLANES:] * drop_last], axis=0)
        x_dn = jnp.concatenate([x[:SUBLANES] * drop_first, x[SUBLANES:]], axis=0)
        y = pltpu.roll(x_up, 1, 0) * w[0:1] + x * w[1:2] + pltpu.roll(x_dn, seq - 1, 0) * w[2:3]
        return (_silu(y) * scale).astype(BF16)

    qs_ref[...] = conv_silu(q_ref, cwq_ref, 1.0)
    ks_ref[...] = conv_silu(k_ref, cwk_ref, B_HEAD_DIM ** -0.5)

    def gate_row(kind):
        ch = kind * B_HEADS + h
        return g_ref[pl.ds(ch, 1), :] + bias_ref[layer, ch]

    li = [gate_row(2 * d) for d in range(2)]
    lf = [_log_sigmoid(gate_row(2 * d + 1)) for d in range(2)]

    rr = lax.broadcasted_iota(jnp.int32, (chunk, chunk), 0)
    cc = lax.broadcasted_iota(jnp.int32, (chunk, chunk), 1)
    diag = rr == cc

    def chunk_step(c, d, m_prev, first):
        rows = pl.ds(c * chunk, chunk)
        lanes = slice(c * chunk, (c + 1) * chunk)
        qc = qs_ref[rows, :]
        kc = ks_ref[rows, :]
        vc = v_ref[rows, :]
        li_row = li[d][:, lanes]
        lf_row = lf[d][:, lanes]
        causal = (cc <= rr) if d == 0 else (cc >= rr)
        b_col = jnp.sum(jnp.where(causal, lf_row, 0.0), axis=1, keepdims=True)
        b_row = jnp.sum(jnp.where(diag, b_col, 0.0), axis=0, keepdims=True)
        a_row = li_row - b_row
        a_col = jnp.sum(jnp.where(diag, a_row, 0.0), axis=1, keepdims=True)
        total = jnp.sum(lf_row, axis=1, keepdims=True)
        log_d = jnp.where(causal, b_col + a_row, NEG_INF)
        log_init = b_col + m_prev
        m_t = jnp.maximum(log_init, jnp.max(log_d, axis=1, keepdims=True))
        d_mat = jnp.exp(log_d - m_t)
        s = lax.dot_general(qc, kc, _NT, preferred_element_type=F32) * d_mat
        num = jnp.dot(s.astype(BF16), vc, preferred_element_type=F32)
        den = jnp.sum(s, axis=1, keepdims=True)
        if not first:
            w_init = jnp.exp(log_init - m_t)
            num = num + w_init * jnp.dot(qc, cst_ref[d].astype(BF16), preferred_element_type=F32)
            den = den + w_init * jnp.sum(qc.astype(F32) * nst_ref[d], axis=1, keepdims=True)
        hc = num / jnp.maximum(jnp.abs(den), jnp.exp(-m_t))
        (hf_ref if d == 0 else hb_ref)[rows, :] = hc
        log_w = total + a_col
        m_new = jnp.maximum(total + m_prev, jnp.max(log_w, axis=0, keepdims=True))
        kw = kc.astype(F32) * jnp.exp(log_w - m_new)
        c_add = lax.dot_general(kw.astype(BF16), vc, _TN, preferred_element_type=F32)
        n_add = jnp.sum(kw, axis=0, keepdims=True)
        if first:
            cst_ref[d] = c_add
            nst_ref[d] = n_add
        else:
            w_0 = jnp.exp(total + m_prev - m_new)
            cst_ref[d] = w_0 * cst_ref[d] + c_add
            nst_ref[d] = w_0 * nst_ref[d] + n_add
        return m_new

    if has_init:
        for d in range(2):
            cst_ref[d] = c0_ref[0, 0, d, 0]
            nst_ref[d] = n0_ref[0, 0, d, 0]
        m = [jnp.full((1, 1), m0_ref[b, layer, d, h], F32) for d in range(2)]
    else:
        m = [jnp.zeros((1, 1), F32) for _ in range(2)]
    for step in range(nc):
        first = (step == 0) and not has_init
        m[0] = chunk_step(step, 0, m[0], first)
        m[1] = chunk_step(nc - 1 - step, 1, m[1], first)

    hh = _rms(o_ref[...].astype(F32) * (hf_ref[...] + hb_ref[...]))
    h_ref[...] = (hh * nw_ref[0]).astype(h_ref.dtype)

    if emit_state:
        row = lax.broadcasted_iota(jnp.int32, (8, LANES), 0)
        mst_out[0, 0] = jnp.where(row == 0, m[0], m[1])
        if has_prev:
            for d in range(2):
                cst_out[0, 0, d, 0] = cprev_ref[0, d, 0]
                nst_out[0, 0, d, 0] = nprev_ref[0, d, 0]
                cst_out[0, 1, d, 0] = cst_ref[d]
                nst_out[0, 1, d, 0] = nst_ref[d]
        else:
            for d in range(2):
                cst_out[0, d, 0] = cst_ref[d]
                nst_out[0, d, 0] = nst_ref[d]


def _mlstm(main, gates, conv_w, gate_bias, norm_w, init, layer, batch, seq, emit_state, prev):
    chunk = B_CHUNK
    hd = B_HEAD_DIM
    has_init = init is not None
    has_prev = prev is not None

    def col(off):
        return lambda b, h: (b, off // hd + h)

    in_specs = [pl.BlockSpec(memory_space=pltpu.SMEM)]
    args = [gate_bias]
    if has_init:
        c0, n0, m0 = init
        in_specs.append(pl.BlockSpec(memory_space=pltpu.SMEM))
        args.append(m0)
    in_specs += [pl.BlockSpec((seq, hd), col(MAIN_QB)),
                 pl.BlockSpec((seq, hd), col(MAIN_KB)),
                 pl.BlockSpec((seq, hd), col(MAIN_VB)),
                 pl.BlockSpec((seq, hd), col(MAIN_OB)),
                 pl.BlockSpec((4 * B_HEADS, seq), lambda b, h: (0, b)),
                 pl.BlockSpec((1, B_CONV, hd), lambda b, h: (layer, 0, h)),
                 pl.BlockSpec((1, B_CONV, hd), lambda b, h: (layer, 0, B_HEADS + h)),
                 pl.BlockSpec((1, 1, hd), lambda b, h: (layer, 0, h))]
    args += [main, main, main, main, gates, conv_w, conv_w, norm_w.reshape(N_EVEN, 1, B_WIDTH)]
    if has_init:
        in_specs += [pl.BlockSpec((1, 1, 2, 1, hd, hd), lambda b, h: (b, layer, 0, h, 0, 0)),
                     pl.BlockSpec((1, 1, 2, 1, 1, hd), lambda b, h: (b, layer, 0, h, 0, 0))]
        args += [c0, n0.reshape(n0.shape[:4] + (1, hd))]
    if has_prev:
        in_specs += [pl.BlockSpec((1, 2, 1, hd, hd), lambda b, h: (b, 0, h, 0, 0)),
                     pl.BlockSpec((1, 2, 1, 1, hd), lambda b, h: (b, 0, h, 0, 0))]
        args += list(prev)
    out_specs = [pl.BlockSpec((seq, hd), lambda b, h: (b, h))]
    out_shape = [jax.ShapeDtypeStruct((batch * seq, B_WIDTH), BF16)]
    if emit_state:
        if has_prev:
            out_specs += [pl.BlockSpec((1, N_EVEN, 2, 1, hd, hd), lambda b, h: (b, 0, 0, h, 0, 0)),
                          pl.BlockSpec((1, N_EVEN, 2, 1, 1, hd), lambda b, h: (b, 0, 0, h, 0, 0))]
            out_shape += [jax.ShapeDtypeStruct((batch, N_EVEN, 2, B_HEADS, hd, hd), F32),
                          jax.ShapeDtypeStruct((batch, N_EVEN, 2, B_HEADS, 1, hd), F32)]
        else:
            out_specs += [pl.BlockSpec((1, 2, 1, hd, hd), lambda b, h: (b, 0, h, 0, 0)),
                          pl.BlockSpec((1, 2, 1, 1, hd), lambda b, h: (b, 0, h, 0, 0))]
            out_shape += [jax.ShapeDtypeStruct((batch, 2, B_HEADS, hd, hd), F32),
                          jax.ShapeDtypeStruct((batch, 2, B_HEADS, 1, hd), F32)]
        out_specs.append(pl.BlockSpec((1, 1, 8, LANES), lambda b, h: (b, h, 0, 0)))
        out_shape.append(jax.ShapeDtypeStruct((batch, B_HEADS, 8, LANES), F32))
    outs = pl.pallas_call(
        functools.partial(_mlstm_kernel, seq=seq, chunk=chunk, layer=layer, has_init=has_init,
                          emit_state=emit_state, has_prev=has_prev),
        grid=(batch, B_HEADS),
        in_specs=in_specs,
        out_specs=out_specs,
        out_shape=out_shape,
        scratch_shapes=[pltpu.VMEM((seq, hd), BF16), pltpu.VMEM((seq, hd), BF16),
                        pltpu.VMEM((seq, hd), F32), pltpu.VMEM((seq, hd), F32),
                        pltpu.VMEM((2, hd, hd), F32), pltpu.VMEM((2, 1, hd), F32)],
        compiler_params=_cparams(2),
        name="mlstm",
    )(*args)
    if not emit_state:
        return outs[0], None
    hb, cst, nst, mst = outs
    mstate = mst[:, :, 0:2, 0].transpose(0, 2, 1)
    return hb, (cst, nst, mstate)


def _outproj_kernel(*refs, n_branch, final):
    it = iter(refs)
    branches = [(next(it), next(it), next(it)) for _ in range(n_branch)]
    w_ref, x_ref, mod_ref = next(it), next(it), next(it)
    fn_ref = next(it) if final else None
    y_ref, wb_ref = next(it), next(it)

    @pl.when(pl.program_id(0) == 0)
    def _():
        wb_ref[...] = w_ref[0].astype(BF16)

    out = None
    for n, (a_ref, zlo_ref, zhi_ref) in enumerate(branches):
        base = n * 2 * Z_BLOCK
        for part, z_ref in enumerate((zlo_ref, zhi_ref)):
            cols = slice(part * Z_BLOCK, (part + 1) * Z_BLOCK)
            gated = a_ref[:, cols] * z_ref[...]
            w = wb_ref[base + part * Z_BLOCK:base + (part + 1) * Z_BLOCK, :]
            p = jnp.dot(gated, w, preferred_element_type=F32)
            out = p if out is None else out + p
    y = x_ref[...] + mod_ref[0, 0, 2:3, :] * out
    if final:
        y = _rms(y) * fn_ref[...]
    y_ref[...] = y


def _outproj(branches, w_out, layer, x, mods, depth_idx, rows_per_mod, final_norm):
    m = x.shape[0]
    tm = 1024
    if rows_per_mod is None:
        mod_map = lambda i: (depth_idx, 0, 0, 0)
    else:
        mod_map = lambda i: (depth_idx, 1 + (i * tm) // rows_per_mod, 0, 0)
    in_specs, args = [], []
    for a, z, z_off in branches:
        zb = z_off // Z_BLOCK
        in_specs += [pl.BlockSpec((tm, 2 * Z_BLOCK), lambda i: (i, 0)),
                     pl.BlockSpec((tm, Z_BLOCK), lambda i, zb=zb: (i, zb)),
                     pl.BlockSpec((tm, Z_BLOCK), lambda i, zb=zb: (i, zb + 1))]
        args += [a, z, z]
    wk = w_out.shape[1]
    in_specs += [pl.BlockSpec((1, wk, D_MODEL), lambda i: (layer, 0, 0), pipeline_mode=pl.Buffered(1)),
                 pl.BlockSpec((tm, D_MODEL), lambda i: (i, 0)),
                 pl.BlockSpec((1, 1, 3, D_MODEL), mod_map)]
    args += [w_out, x, mods]
    final = final_norm is not None
    if final:
        in_specs.append(pl.BlockSpec((1, D_MODEL), lambda i: (0, 0)))
        args.append(final_norm.reshape(1, D_MODEL))
    return pl.pallas_call(
        functools.partial(_outproj_kernel, n_branch=len(branches), final=final),
        grid=(m // tm,),
        in_specs=in_specs,
        out_specs=pl.BlockSpec((tm, D_MODEL), lambda i: (i, 0)),
        out_shape=jax.ShapeDtypeStruct((m, D_MODEL), F32),
        scratch_shapes=[pltpu.VMEM((wk, D_MODEL), BF16)],
        compiler_params=_cparams(1, VMEM_LIMIT_WIDE),
        name="outproj",
    )(*args)


def _mla_slabs(kv, kr_slab):
    lane = lax.broadcasted_iota(jnp.int32, kv.shape, 1)
    nope = lane < C_NOPE
    keys = jnp.where(nope, kv, kr_slab).astype(BF16)
    values = jnp.where(nope, 1.0, kv).astype(BF16)
    return keys, values


def _inproj_c_kernel(*refs, rope):
    it = iter(refs)
    x_ref, mod_ref, g_ref, w_ref, qn_ref, wqb_ref, kvn_ref, wkvb_ref = (next(it) for _ in range(8))
    cos_ref, sin_ref = (next(it), next(it)) if rope else (None, None)
    q_ref, ks_ref, vs_ref, z_ref, ckv_ref, kr_ref = (next(it) for _ in range(6))

    hn = _norm_modulate(x_ref[...], g_ref[0], mod_ref[0, 0]).astype(BF16)
    r = jnp.dot(hn, w_ref[...], preferred_element_type=F32)
    qa = _rms(r[:, CIN_QA:CIN_KVA]) * qn_ref[...]
    ckv = _rms(r[:, CIN_KVA:CIN_Z]) * kvn_ref[...]
    z_ref[...] = _silu(r[:, CIN_Z:CIN_KR]).astype(BF16)
    kr = r[:, CIN_KR:CIN_WIDTH]
    ckv_ref[...] = ckv
    kr_ref[...] = kr
    kv = jnp.dot(ckv.astype(BF16), wkvb_ref[...], preferred_element_type=F32)
    scale = (C_NOPE + C_ROPE) ** -0.5 * LOG2E
    q = jnp.dot(qa.astype(BF16), wqb_ref[...], preferred_element_type=F32) * scale
    half = C_ROPE // 4
    if rope:
        cos = cos_ref[...]
        sin = sin_ref[...]
        kr = _rope(kr, cos, sin, half)
        for hd in range(C_HEADS):
            hs = slice(hd * C_HEAD_PAD, (hd + 1) * C_HEAD_PAD)
            q_ref[:, hs] = _rope(q[:, hs], cos, sin, half).astype(BF16)
    else:
        q_ref[...] = q.astype(BF16)
    for hd in range(C_HEADS):
        hs = slice(hd * C_HEAD_PAD, (hd + 1) * C_HEAD_PAD)
        ks_ref[:, hs], vs_ref[:, hs] = _mla_slabs(kv[:, hs], kr)


def _inproj_c(x, mods, norm_g, depth_idx, w_in, q_norm, w_qb, kv_norm, w_kvb, rows_per_mod, tables):
    m = x.shape[0]
    tm = 512
    rope = tables is not None
    if rows_per_mod is None:
        mod_map = lambda i: (depth_idx, 0, 0, 0)
    else:
        mod_map = lambda i: (depth_idx, 1 + (i * tm) // rows_per_mod, 0, 0)
    const = lambda i: (0, 0)
    qw = C_HEADS * C_HEAD_PAD
    kvw = C_HEADS * (C_NOPE + C_VDIM)
    in_specs = [pl.BlockSpec((tm, D_MODEL), lambda i: (i, 0)),
                pl.BlockSpec((1, 1, 3, D_MODEL), mod_map),
                pl.BlockSpec((1, 1, D_MODEL), lambda i: (depth_idx, 0, 0)),
                pl.BlockSpec((D_MODEL, CIN_WIDTH), const),
                pl.BlockSpec((1, C_Q_RANK), const),
                pl.BlockSpec((C_Q_RANK, qw), const),
                pl.BlockSpec((1, C_KV_RANK), const),
                pl.BlockSpec((C_KV_RANK, kvw), const)]
    args = [x, mods, norm_g, w_in, q_norm, w_qb, kv_norm, w_kvb]
    if rope:
        tpb = rows_per_mod // tm
        in_specs += [pl.BlockSpec((tm, LANES), lambda i: (i % tpb, 0))] * 2
        args += list(tables)
    return pl.pallas_call(
        functools.partial(_inproj_c_kernel, rope=rope),
        grid=(m // tm,),
        in_specs=in_specs,
        out_specs=[pl.BlockSpec((tm, qw), lambda i: (i, 0)),
                   pl.BlockSpec((tm, kvw), lambda i: (i, 0)),
                   pl.BlockSpec((tm, kvw), lambda i: (i, 0)),
                   pl.BlockSpec((tm, C_WIDTH), lambda i: (i, 0)),
                   pl.BlockSpec((tm, C_KV_RANK), lambda i: (i, 0)),
                   pl.BlockSpec((tm, LANES), lambda i: (i, 0))],
        out_shape=[jax.ShapeDtypeStruct((m, qw), BF16),
                   jax.ShapeDtypeStruct((m, kvw), BF16),
                   jax.ShapeDtypeStruct((m, kvw), BF16),
                   jax.ShapeDtypeStruct((m, C_WIDTH), BF16),
                   jax.ShapeDtypeStruct((m, C_KV_RANK), F32),
                   jax.ShapeDtypeStruct((m, LANES), F32)],
        compiler_params=_cparams(1),
        name="inproj_c",
    )(*args)


def _matmul_kernel(x_ref, w_ref, o_ref):
    o_ref[...] = jnp.dot(x_ref[...], w_ref[...], preferred_element_type=F32).astype(o_ref.dtype)


def _matmul(x, w, tm):
    m, k = x.shape
    n = w.shape[1]
    return pl.pallas_call(
        _matmul_kernel,
        grid=(m // tm,),
        in_specs=[pl.BlockSpec((tm, k), lambda i: (i, 0)),
                  pl.BlockSpec((k, n), lambda i: (0, 0))],
        out_specs=pl.BlockSpec((tm, n), lambda i: (i, 0)),
        out_shape=jax.ShapeDtypeStruct((m, n), BF16),
        compiler_params=_cparams(1),
        name="matmul",
    )(x, w)


def _attn_c_kernel(*refs, has_ctx):
    it = iter(refs)
    q_ref, kown_ref, vown_ref = next(it), next(it), next(it)
    kvc_ref, krc_ref = (next(it), next(it)) if has_ctx else (None, None)
    o_ref = next(it)
    kctx_ref, vctx_ref = (next(it), next(it)) if has_ctx else (None, None)

    if has_ctx:
        @pl.when(pl.program_id(1) == 0)
        def _():
            for h in range(C_HEADS):
                hs = slice(h * C_HEAD_PAD, (h + 1) * C_HEAD_PAD)
                kctx_ref[:, hs], vctx_ref[:, hs] = _mla_slabs(kvc_ref[:, hs], krc_ref[...])

    for h in range(C_HEADS):
        hs = slice(h * C_HEAD_PAD, (h + 1) * C_HEAD_PAD)
        qh = q_ref[:, hs]
        scores = [lax.dot_general(qh, kown_ref[:, hs], _NT, preferred_element_type=F32)]
        values = [vown_ref[:, hs]]
        if has_ctx:
            scores.append(lax.dot_general(qh, kctx_ref[:, hs], _NT, preferred_element_type=F32))
            values.append(vctx_ref[:, hs])
        o = _softmax_pv(scores, values, None)
        o_ref[:, h * C_VDIM:(h + 1) * C_VDIM] = o[:, C_NOPE:].astype(o_ref.dtype)


def _attn_c_dense_kernel(q_ref, ks_ref, vs_ref, o_ref, s_ref, e_ref):
    for h in range(C_HEADS):
        hs = slice(h * C_HEAD_PAD, (h + 1) * C_HEAD_PAD)
        s_ref[h] = lax.dot_general(q_ref[:, hs], ks_ref[:, hs], _NT, preferred_element_type=F32)
    s = s_ref[...]
    e_ref[...] = jnp.exp2(s - jnp.max(s, axis=-1, keepdims=True)).astype(BF16)
    for h in range(C_HEADS):
        hs = slice(h * C_HEAD_PAD, (h + 1) * C_HEAD_PAD)
        res = jnp.dot(e_ref[h], vs_ref[:, hs], preferred_element_type=F32)
        o = res / pltpu.roll(res, LANES // 2, 1)
        o_ref[:, h * C_VDIM:(h + 1) * C_VDIM] = o[:, C_NOPE:].astype(o_ref.dtype)


def _attn_c_dense(q, ks, vs, batch, seq):
    w = C_HEADS * C_HEAD_PAD
    return pl.pallas_call(
        _attn_c_dense_kernel,
        grid=(batch,),
        in_specs=[pl.BlockSpec((seq, w), lambda b: (b, 0))] * 3,
        out_specs=pl.BlockSpec((seq, C_WIDTH), lambda b: (b, 0)),
        out_shape=jax.ShapeDtypeStruct((batch * seq, C_WIDTH), BF16),
        scratch_shapes=[pltpu.VMEM((C_HEADS, seq, seq), F32), pltpu.VMEM((C_HEADS, seq, seq), BF16)],
        compiler_params=_cparams(1),
        name="attn_c_dense",
    )(q, ks, vs)


def _attn_c(q, ks, vs, ctx, batch, seq, tq):
    nq = seq // tq
    w = C_HEADS * C_HEAD_PAD
    has_ctx = ctx is not None
    in_specs = [pl.BlockSpec((tq, w), lambda b, i: (b * nq + i, 0)),
                pl.BlockSpec((seq, w), lambda b, i: (b, 0)),
                pl.BlockSpec((seq, w), lambda b, i: (b, 0))]
    args = [q, ks, vs]
    scratch = []
    if has_ctx:
        kv_ctx, kr_ctx = ctx
        nctx = kv_ctx.shape[0] // batch
        in_specs += [pl.BlockSpec((nctx, w), lambda b, i: (b, 0)),
                     pl.BlockSpec((nctx, LANES), lambda b, i: (b, 0))]
        args += [kv_ctx, kr_ctx]
        scratch += [pltpu.VMEM((nctx, w), BF16), pltpu.VMEM((nctx, w), BF16)]
    return pl.pallas_call(
        functools.partial(_attn_c_kernel, has_ctx=has_ctx),
        grid=(batch, nq),
        in_specs=in_specs,
        out_specs=pl.BlockSpec((tq, C_WIDTH), lambda b, i: (b * nq + i, 0)),
        out_shape=jax.ShapeDtypeStruct((batch * seq, C_WIDTH), BF16),
        scratch_shapes=scratch,
        compiler_params=_cparams(2),
        name="attn_c",
    )(*args)


def _rope_tables(n_tokens):
    pos_r = np.repeat(np.arange(n_tokens // GRID_W), GRID_W).astype(np.float64)
    pos_c = np.tile(np.arange(GRID_W), n_tokens // GRID_W).astype(np.float64)

    def seg(d_axis):
        half = d_axis // 2
        freqs = np.power(ROPE_BASE, -np.arange(half, dtype=np.float64) / half)
        cos, sin = [], []
        for pos in (pos_r, pos_c):
            ang = pos[:, None] * freqs[None, :]
            cos += [np.cos(ang), np.cos(ang)]
            sin += [-np.sin(ang), np.sin(ang)]
        return np.concatenate(cos, axis=1), np.concatenate(sin, axis=1)

    cos_a, sin_a = seg(A_HEAD_DIM // 2)
    cos_a, sin_a = np.tile(cos_a, (1, 2)), np.tile(sin_a, (1, 2))
    cos_r, sin_r = seg(C_ROPE // 2)
    ones = np.ones((n_tokens, C_NOPE))
    pad = C_HEAD_PAD - C_NOPE - C_ROPE
    cos_c = np.concatenate([ones, cos_r, np.ones((n_tokens, pad))], axis=1)
    sin_c = np.concatenate([0 * ones, sin_r, np.zeros((n_tokens, pad))], axis=1)
    f = lambda a: jnp.asarray(a, dtype=F32)
    return (f(cos_a), f(sin_a)), (f(cos_c), f(sin_c))


def _pad_cols(w, left, total):
    return jnp.pad(w, ((0, 0), (left, total - left - w.shape[1])))


def kernel(x_prompt, x_sample, cache_a_k, cache_a_v, state_b_mem, state_b_norm, state_b_max, cache_c_kv, cache_c_krope, c, c_ctx, norm_g, w_mod, b_mod, w_in_ab, sink_a, conv_b, gate_bias_b, norm_b, w_out_ab, w_in_c, q_norm_c, w_qb_c, kv_norm_c, w_kvb_c, w_out_c, final_norm):
    bp, tp, _ = x_prompt.shape
    bs, ts, _ = x_sample.shape
    past = cache_a_k.shape[2]
    tables_a, tables_c = _rope_tables(ts)
    w_ab_t = jnp.swapaxes(w_in_ab, 1, 2)

    cond = jnp.zeros((16, D_MODEL), F32).at[0].set(c_ctx).at[1:1 + bs].set(c)
    mods = _adaln(cond, w_mod, b_mod).reshape(DEPTH, 16, 3, D_MODEL)
    gains = norm_g.reshape(DEPTH, 1, D_MODEL)

    yp = x_prompt.reshape(bp * tp, D_MODEL)
    ys = x_sample.reshape(bs * ts, D_MODEL)
    a_k, a_v, b_max, c_kvs, c_krs = [], [], [], [], []
    states = None
    for l in range(DEPTH):
        j = l // 2
        fin = final_norm if l == DEPTH - 1 else None
        if l % 2 == 0:
            w_bf = _cast_rows(w_ab_t, j, MAIN_WIDTH)
            main_p, k_p, v_p, gates_p = _inproj_ab(yp, mods, gains, l, w_bf, w_ab_t, j, tp, True, True, None)
            main_s, gates_s = _inproj_ab(ys, mods, gains, l, w_bf, w_ab_t, j, ts, False, False, tables_a)
            a_k.append(k_p.reshape(bp, tp, A_KV_HEADS, A_HEAD_DIM))
            a_v.append(v_p.reshape(bp, tp, A_KV_HEADS, A_HEAD_DIM))

            attn_p = _attn_a_prompt(main_p, sink_a[j], bp, tp)
            ck = cache_a_k[:, j].reshape(bs, past, A_KV_WIDTH).astype(BF16)
            cv = cache_a_v[:, j].reshape(bs, past, A_KV_WIDTH).astype(BF16)
            attn_s = _attn_a_sample(main_s, sink_a[j], ck, cv, bs, ts)

            prev = None if states is None else states[:2]
            hb_p, states = _mlstm(main_p, gates_p, conv_b, gate_bias_b, norm_b, None, j, bp, tp, True, prev)
            init = (state_b_mem, state_b_norm, state_b_max)
            hb_s, _ = _mlstm(main_s, gates_s, conv_b, gate_bias_b, norm_b, init, j, bs, ts, False, None)
            b_max.append(states[2])

            yp = _outproj([(attn_p, main_p, MAIN_ZA), (hb_p, main_p, MAIN_ZB)], w_out_ab, j, yp, mods, l, None, fin)
            ys = _outproj([(attn_s, main_s, MAIN_ZA), (hb_s, main_s, MAIN_ZB)], w_out_ab, j, ys, mods, l, ts, fin)
        else:
            w = w_in_c[j]
            o_kva = C_Q_RANK
            o_kr = C_Q_RANK + C_KV_RANK
            o_z = o_kr + C_ROPE
            w_in = jnp.concatenate([w[:, :o_kva], w[:, o_kva:o_kr], w[:, o_z:],
                                    _pad_cols(w[:, o_kr:o_z], C_NOPE, C_HEAD_PAD)], axis=1).astype(BF16)
            wq = w_qb_c[j].reshape(C_Q_RANK, C_HEADS, C_NOPE + C_ROPE)
            wq = jnp.pad(wq, ((0, 0), (0, 0), (0, C_HEAD_PAD - C_NOPE - C_ROPE)))
            wq = wq.reshape(C_Q_RANK, C_HEADS * C_HEAD_PAD).astype(BF16)
            wkv = w_kvb_c[j].astype(BF16)
            qn = q_norm_c[j].reshape(1, C_Q_RANK)
            kvn = kv_norm_c[j].reshape(1, C_KV_RANK)

            q_p, ks_p, vs_p, z_p, ckv_p, kr_p = _inproj_c(yp, mods, gains, l, w_in, qn, wq, kvn, wkv, None, None)
            q_s, ks_s, vs_s, z_s, _, _ = _inproj_c(ys, mods, gains, l, w_in, qn, wq, kvn, wkv, ts, tables_c)
            c_kvs.append(ckv_p.reshape(bp, tp, C_KV_RANK))
            c_krs.append(kr_p[:, C_NOPE:C_NOPE + C_ROPE].reshape(bp, tp, C_ROPE))

            cc = cache_c_kv[:, j].reshape(bs * past, C_KV_RANK).astype(BF16)
            kv_ctx = _matmul(cc, wkv, 512)
            kr_ctx = _pad_cols(cache_c_krope[:, j].reshape(bs * past, C_ROPE), C_NOPE, C_HEAD_PAD).astype(BF16)

            attn_p = _attn_c_dense(q_p, ks_p, vs_p, bp, tp)
            attn_s = _attn_c(q_s, ks_s, vs_s, (kv_ctx, kr_ctx), bs, ts, 512)

            yp = _outproj([(attn_p, z_p, 0)], w_out_c, j, yp, mods, l, None, fin)
            ys = _outproj([(attn_s, z_s, 0)], w_out_c, j, ys, mods, l, ts, fin)

    b_mem = states[0]
    b_nrm = states[1].reshape(bp, N_EVEN, 2, B_HEADS, B_HEAD_DIM)
    return (yp.reshape(bp, tp, D_MODEL), ys.reshape(bs, ts, D_MODEL),
            jnp.stack(a_k, axis=1), jnp.stack(a_v, axis=1), b_mem, b_nrm,
            jnp.stack(b_max, axis=1), jnp.stack(c_kvs, axis=1), jnp.stack(c_krs, axis=1))
```

```python
import functools
import math

import numpy as np
import jax
import jax.numpy as jnp
from jax import lax
from jax.experimental import pallas as pl
from jax.experimental.pallas import tpu as pltpu

F32 = jnp.float32
BF16 = jnp.bfloat16

D_MODEL = 1024
DEPTH = 4
N_EVEN = 2
EPS = 1e-6
ROPE_BASE = 10000.0
NEG_INF = -1e30
GRID_W = 64
LOG2E = math.log2(math.e)
A_HEADS = 16
A_KV_HEADS = 4
A_GROUP = A_HEADS // A_KV_HEADS
A_HEAD_DIM = 64
A_WIDTH = A_HEADS * A_HEAD_DIM
A_KV_WIDTH = A_KV_HEADS * A_HEAD_DIM
WINDOW = 128
BLOCK = 128
B_HEADS = 4
B_HEAD_DIM = 256
B_WIDTH = B_HEADS * B_HEAD_DIM
B_CHUNK = 256
C_HEADS = 16
C_NOPE = 64
C_ROPE = 32
C_VDIM = 64
C_Q_RANK = 384
C_KV_RANK = 256
C_WIDTH = C_HEADS * C_VDIM
C_HEAD_PAD = 128

LANES = 128
SUBLANES = 8
B_CONV = 3
MAIN_QA, MAIN_KA, MAIN_VA, MAIN_ZA, MAIN_QB, MAIN_KB, MAIN_VB, MAIN_OB, MAIN_ZB = (
    0, 1024, 1280, 1536, 2560, 3584, 4608, 5632, 6656)
MAIN_WIDTH = 7680
CIN_QA, CIN_KVA, CIN_KR, CIN_Z, CIN_WIDTH = 0, 384, 640, 672, 1696
Z_BLOCK = 512
QK_TILE = 1280
A_QSCALE = A_HEAD_DIM ** -0.5 * LOG2E

MIB = 1024 * 1024
VMEM_LIMIT = 48 * MIB
VMEM_LIMIT_WIDE = 56 * MIB

_NT = (((1,), (1,)), ((), ()))
_TN = (((0,), (0,)), ((), ()))


def _cparams(n_axes, vmem=VMEM_LIMIT):
    return pltpu.CompilerParams(dimension_semantics=("arbitrary",) * n_axes,
                                vmem_limit_bytes=vmem)


def _silu(x):
    return x * jax.nn.sigmoid(x)


def _log_sigmoid(x):
    return jnp.minimum(x, 0.0) - jnp.log1p(jnp.exp(-jnp.abs(x)))


def _rms(x):
    return x * lax.rsqrt(jnp.mean(x * x, axis=-1, keepdims=True) + EPS)


def _norm_modulate(x, g, mod):
    y = _rms(x) * g
    return y * (1.0 + mod[1:2, :]) + mod[0:1, :]


def _swap_halves(x, half):
    lane = lax.broadcasted_iota(jnp.int32, x.shape, 1)
    first = (lane % (2 * half)) < half
    return jnp.where(first, pltpu.roll(x, LANES - half, 1), pltpu.roll(x, half, 1))


def _rope(x, cos, sin, half):
    return x * cos + _swap_halves(x, half) * sin


def _softmax_pv(scores, values, sink):
    tiles = [s[:, t * LANES:(t + 1) * LANES] for s in scores for t in range(s.shape[1] // LANES)]
    m = jnp.max(functools.reduce(jnp.maximum, tiles), axis=-1, keepdims=True)
    if sink is not None:
        m = jnp.maximum(m, sink)
    res = functools.reduce(jnp.add, [
        jnp.dot(jnp.exp2(s - m).astype(BF16), v, preferred_element_type=F32)
        for s, v in zip(scores, values)])
    den = pltpu.roll(res, LANES // 2, 1)
    if sink is not None:
        den = den + jnp.exp2(sink - m)
    return res / den


def _adaln_kernel(c_ref, w_ref, b_ref, o_ref):
    a = _silu(c_ref[...]).astype(BF16)
    w = w_ref[0].astype(BF16)
    o_ref[0] = jnp.dot(a, w, preferred_element_type=F32) + b_ref[0]


def _adaln(cond, w_mod, b_mod):
    tn = 1024
    n = 3 * D_MODEL
    return pl.pallas_call(
        _adaln_kernel,
        grid=(DEPTH, n // tn),
        in_specs=[pl.BlockSpec((16, D_MODEL), lambda l, j: (0, 0)),
                  pl.BlockSpec((1, D_MODEL, tn), lambda l, j: (l, 0, j)),
                  pl.BlockSpec((1, 1, tn), lambda l, j: (l, 0, j))],
        out_specs=pl.BlockSpec((1, 16, tn), lambda l, j: (l, 0, j)),
        out_shape=jax.ShapeDtypeStruct((DEPTH, 16, n), F32),
        compiler_params=_cparams(2),
        name="adaln",
    )(cond, w_mod, b_mod.reshape(DEPTH, 1, n))


def _cast_kernel(w_ref, o_ref):
    o_ref[...] = w_ref[0].astype(o_ref.dtype)


def _cast_rows(w_t, layer, rows):
    tr = QK_TILE
    d = w_t.shape[2]
    return pl.pallas_call(
        _cast_kernel,
        grid=(rows // tr,),
        in_specs=[pl.BlockSpec((1, tr, d), lambda i: (layer, i, 0))],
        out_specs=pl.BlockSpec((tr, d), lambda i: (i, 0)),
        out_shape=jax.ShapeDtypeStruct((rows, d), BF16),
        compiler_params=_cparams(1),
        name="cast_rows",
    )(w_t)


def _main_chunk_kind(col):
    bounds = ((MAIN_KA, "q"), (MAIN_VA, "k"), (MAIN_ZA, "plain"), (MAIN_QB, "silu"), (MAIN_OB, "plain"),
              (MAIN_ZB, "sigmoid"), (MAIN_WIDTH, "silu"))
    return next(kind for end, kind in bounds if col < end)


def _inproj_ab_kernel(*refs, emit_kv, rope):
    it = iter(refs)
    x_ref, mod_ref, g_ref, wb_ref, wg_ref = (next(it) for _ in range(5))
    cos_ref, sin_ref = (next(it), next(it)) if rope else (None, None)
    main_ref = next(it)
    k_ref, v_ref = (next(it), next(it)) if emit_kv else (None, None)
    gate_ref = next(it)

    mod = mod_ref[0, 0]
    g = g_ref[0]
    hn = _norm_modulate(x_ref[...], g, mod).astype(BF16)
    gate_ref[...] = lax.dot_general(wg_ref[0].astype(BF16), hn, _NT, preferred_element_type=F32)
    if emit_kv:
        kv_t = lax.dot_general(wb_ref[MAIN_KA:MAIN_ZA, :], hn, _NT, preferred_element_type=F32)
        seq = k_ref.shape[2]
        for n in range(k_ref.shape[0]):
            k_ref[n] = kv_t[:A_KV_WIDTH, n * seq:(n + 1) * seq]
            v_ref[n] = kv_t[A_KV_WIDTH:, n * seq:(n + 1) * seq]
    for t in range(MAIN_WIDTH // QK_TILE):
        res = lax.dot_general(hn, wb_ref[t * QK_TILE:(t + 1) * QK_TILE, :], _NT, preferred_element_type=F32)
        for c in range(QK_TILE // LANES):
            col = t * QK_TILE + c * LANES
            kind = _main_chunk_kind(col)
            chunk = res[:, c * LANES:(c + 1) * LANES]
            if kind in ("q", "k"):
                if rope:
                    chunk = _rope(chunk, cos_ref[...], sin_ref[...], A_HEAD_DIM // 4)
                if kind == "q":
                    chunk = chunk * A_QSCALE
            elif kind == "silu":
                chunk = _silu(chunk)
            elif kind == "sigmoid":
                chunk = jax.nn.sigmoid(chunk)
            main_ref[:, col:col + LANES] = chunk.astype(BF16)


def _inproj_ab(x, mods, norm_g, depth_idx, w_bf, w_t, layer, seq, shared_mod, emit_kv, tables):
    m = x.shape[0]
    tm = 512
    rope = tables is not None
    if shared_mod:
        mod_map = lambda i: (depth_idx, 0, 0, 0)
    else:
        mod_map = lambda i: (depth_idx, 1 + (i * tm) // seq, 0, 0)
    n_gate = 4 * B_HEADS
    tiles_per_table = max(seq, tm) // tm
    table_spec = pl.BlockSpec((tm, LANES), lambda i: (i % tiles_per_table, 0))
    in_specs = [pl.BlockSpec((tm, D_MODEL), lambda i: (i, 0)),
                pl.BlockSpec((1, 1, 3, D_MODEL), mod_map),
                pl.BlockSpec((1, 1, D_MODEL), lambda i: (depth_idx, 0, 0)),
                pl.BlockSpec((MAIN_WIDTH, D_MODEL), lambda i: (0, 0), pipeline_mode=pl.Buffered(1)),
                pl.BlockSpec((1, n_gate, D_MODEL), lambda i: (layer, MAIN_WIDTH // n_gate, 0))]
    args = [x, mods, norm_g, w_bf, w_t]
    if rope:
        in_specs += [table_spec] * 2
        args += list(tables)
    out_specs = [pl.BlockSpec((tm, MAIN_WIDTH), lambda i: (i, 0))]
    out_shape = [jax.ShapeDtypeStruct((m, MAIN_WIDTH), BF16)]
    if emit_kv:
        out_specs += [pl.BlockSpec((tm // seq, A_KV_WIDTH, seq), lambda i: (i, 0, 0))] * 2
        out_shape += [jax.ShapeDtypeStruct((m // seq, A_KV_WIDTH, seq), F32)] * 2
    out_specs.append(pl.BlockSpec((n_gate, tm), lambda i: (0, i)))
    out_shape.append(jax.ShapeDtypeStruct((n_gate, m), F32))
    return pl.pallas_call(
        functools.partial(_inproj_ab_kernel, emit_kv=emit_kv, rope=rope),
        grid=(m // tm,),
        in_specs=in_specs,
        out_specs=out_specs,
        out_shape=out_shape,
        compiler_params=_cparams(1, VMEM_LIMIT_WIDE),
        name="inproj_ab",
    )(*args)


def _value_slabs(v_ref, slab_ref):
    keys = v_ref.shape[0]
    ones = jnp.ones((keys, LANES - A_HEAD_DIM), BF16)
    for g in range(A_KV_HEADS):
        slab_ref[:, g * LANES:g * LANES + A_HEAD_DIM] = v_ref[:, g * A_HEAD_DIM:(g + 1) * A_HEAD_DIM]
        slab_ref[:, g * LANES + A_HEAD_DIM:(g + 1) * LANES] = ones


def _group_queries(q_ref, g):
    return jnp.concatenate([q_ref[:, (g * A_GROUP + hh) * A_HEAD_DIM:(g * A_GROUP + hh + 1) * A_HEAD_DIM]
                            for hh in range(A_GROUP)], axis=0)


def _sink_softmax(s_ref, e_ref, t_ref, sink_ref):
    s = s_ref[...]
    sink = sink_ref[...] * LOG2E
    m = jnp.maximum(jnp.max(s, axis=-1, keepdims=True), sink)
    e_ref[...] = jnp.exp2(s - m).astype(BF16)
    t_ref[...] = jnp.exp2(sink - m)


def _attn_a_prompt_kernel(sink_ref, q_ref, k_ref, v_ref, o_ref, s_ref, e_ref, t_ref, vs_ref):
    _value_slabs(v_ref, vs_ref)
    seq = q_ref.shape[0]
    rows = A_GROUP * seq
    for g in range(A_KV_HEADS):
        heads = slice(g * A_GROUP, (g + 1) * A_GROUP)
        ks = slice(g * A_HEAD_DIM, (g + 1) * A_HEAD_DIM)
        s = lax.dot_general(_group_queries(q_ref, g), k_ref[:, ks], _NT, preferred_element_type=F32)
        s_ref[heads] = s.reshape(A_GROUP, seq, seq)
    _sink_softmax(s_ref, e_ref, t_ref, sink_ref)
    for g in range(A_KV_HEADS):
        heads = slice(g * A_GROUP, (g + 1) * A_GROUP)
        res = jnp.dot(e_ref[heads].reshape(rows, seq), vs_ref[:, g * LANES:(g + 1) * LANES],
                      preferred_element_type=F32)
        o = res / (pltpu.roll(res, LANES // 2, 1) + t_ref[heads].reshape(rows, 1))
        for hh in range(A_GROUP):
            h = g * A_GROUP + hh
            o_ref[:, h * A_HEAD_DIM:(h + 1) * A_HEAD_DIM] = (
                o[hh * seq:(hh + 1) * seq, :A_HEAD_DIM].astype(o_ref.dtype))


def _attn_a_prompt(main, sink, batch, seq):
    kb = MAIN_KA // A_KV_WIDTH
    vb = MAIN_VA // A_KV_WIDTH
    return pl.pallas_call(
        _attn_a_prompt_kernel,
        grid=(batch,),
        in_specs=[pl.BlockSpec((A_HEADS, 1, 1), lambda b: (0, 0, 0)),
                  pl.BlockSpec((seq, A_WIDTH), lambda b: (b, MAIN_QA // A_WIDTH)),
                  pl.BlockSpec((seq, A_KV_WIDTH), lambda b: (b, kb)),
                  pl.BlockSpec((seq, A_KV_WIDTH), lambda b: (b, vb))],
        out_specs=pl.BlockSpec((seq, A_WIDTH), lambda b: (b, 0)),
        out_shape=jax.ShapeDtypeStruct((batch * seq, A_WIDTH), BF16),
        scratch_shapes=[pltpu.VMEM((A_HEADS, seq, seq), F32),
                        pltpu.VMEM((A_HEADS, seq, seq), BF16),
                        pltpu.VMEM((A_HEADS, seq, 1), F32),
                        pltpu.VMEM((seq, A_KV_HEADS * LANES), BF16)],
        compiler_params=_cparams(1),
        name="attn_a_prompt",
    )(sink.reshape(A_HEADS, 1, 1), main, main, main)


def _attn_a_sample_kernel(sink_ref, q_ref, k_ref, v_ref, ck_ref, cv_ref, bias_ref, o_ref, vs_ref, cvs_ref):
    i = pl.program_id(1)
    seq = k_ref.shape[0]
    span = 3 * BLOCK

    @pl.when(i == 0)
    def _():
        _value_slabs(v_ref, vs_ref)
        _value_slabs(cv_ref.at[0], cvs_ref)

    start = pl.multiple_of(jnp.clip((i - 1) * BLOCK, 0, seq - span), BLOCK)
    kw = k_ref[pl.ds(start, span), :]
    vw = vs_ref[pl.ds(start, span), :]
    ck = ck_ref[0]
    bias = bias_ref[(i * BLOCK - start) // BLOCK]
    bias = jnp.concatenate([bias] * A_GROUP, axis=0)
    rows = A_GROUP * BLOCK
    head_of_row = lax.broadcasted_iota(jnp.int32, (rows, 1), 0) // BLOCK
    for g in range(A_KV_HEADS):
        ks = slice(g * A_HEAD_DIM, (g + 1) * A_HEAD_DIM)
        gs = slice(g * LANES, (g + 1) * LANES)
        sink = jnp.zeros((rows, 1), F32)
        for hh in range(A_GROUP):
            sink = jnp.where(head_of_row == hh, sink_ref[g * A_GROUP + hh] * LOG2E, sink)
        qg = _group_queries(q_ref, g)
        s_loc = lax.dot_general(qg, kw[:, ks], _NT, preferred_element_type=F32) + bias
        s_ctx = lax.dot_general(qg, ck[:, ks], _NT, preferred_element_type=F32)
        o = _softmax_pv([s_loc, s_ctx], [vw[:, gs], cvs_ref[:, gs]], sink)
        for hh in range(A_GROUP):
            h = g * A_GROUP + hh
            o_ref[:, h * A_HEAD_DIM:(h + 1) * A_HEAD_DIM] = (
                o[hh * BLOCK:(hh + 1) * BLOCK, :A_HEAD_DIM].astype(o_ref.dtype))


def _window_bias():
    r = np.arange(BLOCK)[:, None]
    c = np.arange(3 * BLOCK)[None, :]
    masks = [np.where(np.abs(off + r - c) <= WINDOW, 0.0, NEG_INF) for off in (0, BLOCK, 2 * BLOCK)]
    return jnp.asarray(np.stack(masks), dtype=F32)


def _attn_a_sample(main, sink, ck, cv, batch, seq):
    nb = seq // BLOCK
    kb = MAIN_KA // A_KV_WIDTH
    vb = MAIN_VA // A_KV_WIDTH
    ctx = ck.shape[1]
    slab_w = A_KV_HEADS * LANES
    return pl.pallas_call(
        _attn_a_sample_kernel,
        grid=(batch, nb),
        in_specs=[pl.BlockSpec(memory_space=pltpu.SMEM),
                  pl.BlockSpec((BLOCK, A_WIDTH), lambda b, i: (b * nb + i, MAIN_QA // A_WIDTH)),
                  pl.BlockSpec((seq, A_KV_WIDTH), lambda b, i: (b, kb)),
                  pl.BlockSpec((seq, A_KV_WIDTH), lambda b, i: (b, vb)),
                  pl.BlockSpec((1, ctx, A_KV_WIDTH), lambda b, i: (b, 0, 0)),
                  pl.BlockSpec((1, ctx, A_KV_WIDTH), lambda b, i: (b, 0, 0)),
                  pl.BlockSpec((3, BLOCK, 3 * BLOCK), lambda b, i: (0, 0, 0))],
        out_specs=pl.BlockSpec((BLOCK, A_WIDTH), lambda b, i: (b * nb + i, 0)),
        out_shape=jax.ShapeDtypeStruct((batch * seq, A_WIDTH), BF16),
        scratch_shapes=[pltpu.VMEM((seq, slab_w), BF16),
                        pltpu.VMEM((ctx, slab_w), BF16)],
        compiler_params=_cparams(2),
        name="attn_a_sample",
    )(sink, main, main, main, ck, cv, _window_bias())


def _mlstm_kernel(*refs, seq, chunk, layer, has_init, emit_state, has_prev):
    it = iter(refs)
    bias_ref = next(it)
    m0_ref = next(it) if has_init else None
    q_ref, k_ref, v_ref, o_ref, g_ref, cwq_ref, cwk_ref, nw_ref = (next(it) for _ in range(8))
    c0_ref, n0_ref = (next(it), next(it)) if has_init else (None, None)
    cprev_ref, nprev_ref = (next(it), next(it)) if has_prev else (None, None)
    h_ref = next(it)
    cst_out, nst_out, mst_out = (next(it), next(it), next(it)) if emit_state else (None, None, None)
    qs_ref, ks_ref, hf_ref, hb_ref, cst_ref, nst_ref = (next(it) for _ in range(6))

    b = pl.program_id(0)
    h = pl.program_id(1)
    nc = seq // chunk

    edge = lax.broadcasted_iota(jnp.int32, (SUBLANES, 1), 0)
    drop_first = (edge != 0).astype(F32)
    drop_last = (edge != SUBLANES - 1).astype(F32)

    def conv_silu(x_ref, w_ref, scale):
        x = x_ref[...].astype(F32)
        w = w_ref[0]
        x_up = jnp.concatenate([x[:seq - SUBLANES], x[seq - SUBLANES:] * drop_last], axis=0)
        x_dn = jnp.concatenate([x[:SUBLANES] * drop_first, x[SUBLANES:]], axis=0)
        y = pltpu.roll(x_up, 1, 0) * w[0:1] + x * w[1:2] + pltpu.roll(x_dn, seq - 1, 0) * w[2:3]
        return (_silu(y) * scale).astype(BF16)

    qs_ref[...] = conv_silu(q_ref, cwq_ref, 1.0)
    ks_ref[...] = conv_silu(k_ref, cwk_ref, B_HEAD_DIM ** -0.5)

    def gate_row(kind):
        ch = kind * B_HEADS + h
        return g_ref[pl.ds(ch, 1), :] + bias_ref[layer, ch]

    li = [gate_row(2 * d) for d in range(2)]
    lf = [_log_sigmoid(gate_row(2 * d + 1)) for d in range(2)]

    rr = lax.broadcasted_iota(jnp.int32, (chunk, chunk), 0)
    cc = lax.broadcasted_iota(jnp.int32, (chunk, chunk), 1)
    diag = rr == cc

    def chunk_step(c, d, m_prev, first):
        rows = pl.ds(c * chunk, chunk)
        lanes = slice(c * chunk, (c + 1) * chunk)
        qc = qs_ref[rows, :]
        kc = ks_ref[rows, :]
        vc = v_ref[rows, :]
        li_row = li[d][:, lanes]
        lf_row = lf[d][:, lanes]
        causal = (cc <= rr) if d == 0 else (cc >= rr)
        b_col = jnp.sum(jnp.where(causal, lf_row, 0.0), axis=1, keepdims=True)
        b_row = jnp.sum(jnp.where(diag, b_col, 0.0), axis=0, keepdims=True)
        a_row = li_row - b_row
        a_col = jnp.sum(jnp.where(diag, a_row, 0.0), axis=1, keepdims=True)
        total = jnp.sum(lf_row, axis=1, keepdims=True)
        log_d = jnp.where(causal, b_col + a_row, NEG_INF)
        log_init = b_col + m_prev
        m_t = jnp.maximum(log_init, jnp.max(log_d, axis=1, keepdims=True))
        d_mat = jnp.exp(log_d - m_t)
        s = lax.dot_general(qc, kc, _NT, preferred_element_type=F32) * d_mat
        num = jnp.dot(s.astype(BF16), vc, preferred_element_type=F32)
        den = jnp.sum(s, axis=1, keepdims=True)
        if not first:
            w_init = jnp.exp(log_init - m_t)
            num = num + w_init * jnp.dot(qc, cst_ref[d].astype(BF16), preferred_element_type=F32)
            n_rows = jnp.broadcast_to(nst_ref[d], (SUBLANES, B_HEAD_DIM)).astype(BF16)
            den = den + w_init * lax.dot_general(qc, n_rows, _NT, preferred_element_type=F32)[:, 0:1]
        hc = num / jnp.maximum(jnp.abs(den), jnp.exp(-m_t))
        (hf_ref if d == 0 else hb_ref)[rows, :] = hc
        log_w = total + a_col
        m_new = jnp.maximum(total + m_prev, jnp.max(log_w, axis=0, keepdims=True))
        kw = kc.astype(F32) * jnp.exp(log_w - m_new)
        c_add = lax.dot_general(kw.astype(BF16), vc, _TN, preferred_element_type=F32)
        n_add = jnp.sum(kw, axis=0, keepdims=True)
        if first:
            cst_ref[d] = c_add
            nst_ref[d] = n_add
        else:
            w_0 = jnp.exp(total + m_prev - m_new)
            cst_ref[d] = w_0 * cst_ref[d] + c_add
            nst_ref[d] = w_0 * nst_ref[d] + n_add
        return m_new

    if has_init:
        for d in range(2):
            cst_ref[d] = c0_ref[0, 0, d, 0]
            nst_ref[d] = n0_ref[0, 0, d, 0]
        m = [jnp.full((1, 1), m0_ref[b, layer, d, h], F32) for d in range(2)]
    else:
        m = [jnp.zeros((1, 1), F32) for _ in range(2)]
    for step in range(nc):
        first = (step == 0) and not has_init
        m[0] = chunk_step(step, 0, m[0], first)
        m[1] = chunk_step(nc - 1 - step, 1, m[1], first)

    hh = _rms(o_ref[...].astype(F32) * (hf_ref[...] + hb_ref[...]))
    h_ref[...] = (hh * nw_ref[0]).astype(h_ref.dtype)

    if emit_state:
        row = lax.broadcasted_iota(jnp.int32, (8, LANES), 0)
        mst_out[0, 0] = jnp.where(row == 0, m[0], m[1])
        if has_prev:
            for d in range(2):
                cst_out[0, 0, d, 0] = cprev_ref[0, d, 0]
                nst_out[0, 0, d, 0] = nprev_ref[0, d, 0]
                cst_out[0, 1, d, 0] = cst_ref[d]
                nst_out[0, 1, d, 0] = nst_ref[d]
        else:
            for d in range(2):
                cst_out[0, d, 0] = cst_ref[d]
                nst_out[0, d, 0] = nst_ref[d]


def _mlstm(main, gates, conv_w, gate_bias, norm_w, init, layer, batch, seq, emit_state, prev):
    chunk = B_CHUNK
    hd = B_HEAD_DIM
    has_init = init is not None
    has_prev = prev is not None

    def col(off):
        return lambda b, h: (b, off // hd + h)

    in_specs = [pl.BlockSpec(memory_space=pltpu.SMEM)]
    args = [gate_bias]
    if has_init:
        c0, n0, m0 = init
        in_specs.append(pl.BlockSpec(memory_space=pltpu.SMEM))
        args.append(m0)
    in_specs += [pl.BlockSpec((seq, hd), col(MAIN_QB)),
                 pl.BlockSpec((seq, hd), col(MAIN_KB)),
                 pl.BlockSpec((seq, hd), col(MAIN_VB)),
                 pl.BlockSpec((seq, hd), col(MAIN_OB)),
                 pl.BlockSpec((4 * B_HEADS, seq), lambda b, h: (0, b)),
                 pl.BlockSpec((1, B_CONV, hd), lambda b, h: (layer, 0, h)),
                 pl.BlockSpec((1, B_CONV, hd), lambda b, h: (layer, 0, B_HEADS + h)),
                 pl.BlockSpec((1, 1, hd), lambda b, h: (layer, 0, h))]
    args += [main, main, main, main, gates, conv_w, conv_w, norm_w.reshape(N_EVEN, 1, B_WIDTH)]
    if has_init:
        in_specs += [pl.BlockSpec((1, 1, 2, 1, hd, hd), lambda b, h: (b, layer, 0, h, 0, 0)),
                     pl.BlockSpec((1, 1, 2, 1, 1, hd), lambda b, h: (b, layer, 0, h, 0, 0))]
        args += [c0, n0.reshape(n0.shape[:4] + (1, hd))]
    if has_prev:
        in_specs += [pl.BlockSpec((1, 2, 1, hd, hd), lambda b, h: (b, 0, h, 0, 0)),
                     pl.BlockSpec((1, 2, 1, 1, hd), lambda b, h: (b, 0, h, 0, 0))]
        args += list(prev)
    out_specs = [pl.BlockSpec((seq, hd), lambda b, h: (b, h))]
    out_shape = [jax.ShapeDtypeStruct((batch * seq, B_WIDTH), BF16)]
    if emit_state:
        if has_prev:
            out_specs += [pl.BlockSpec((1, N_EVEN, 2, 1, hd, hd), lambda b, h: (b, 0, 0, h, 0, 0)),
                          pl.BlockSpec((1, N_EVEN, 2, 1, 1, hd), lambda b, h: (b, 0, 0, h, 0, 0))]
            out_shape += [jax.ShapeDtypeStruct((batch, N_EVEN, 2, B_HEADS, hd, hd), F32),
                          jax.ShapeDtypeStruct((batch, N_EVEN, 2, B_HEADS, 1, hd), F32)]
        else:
            out_specs += [pl.BlockSpec((1, 2, 1, hd, hd), lambda b, h: (b, 0, h, 0, 0)),
                          pl.BlockSpec((1, 2, 1, 1, hd), lambda b, h: (b, 0, h, 0, 0))]
            out_shape += [jax.ShapeDtypeStruct((batch, 2, B_HEADS, hd, hd), F32),
                          jax.ShapeDtypeStruct((batch, 2, B_HEADS, 1, hd), F32)]
        out_specs.append(pl.BlockSpec((1, 1, 8, LANES), lambda b, h: (b, h, 0, 0)))
        out_shape.append(jax.ShapeDtypeStruct((batch, B_HEADS, 8, LANES), F32))
    outs = pl.pallas_call(
        functools.partial(_mlstm_kernel, seq=seq, chunk=chunk, layer=layer, has_init=has_init,
                          emit_state=emit_state, has_prev=has_prev),
        grid=(batch, B_HEADS),
        in_specs=in_specs,
        out_specs=out_specs,
        out_shape=out_shape,
        scratch_shapes=[pltpu.VMEM((seq, hd), BF16), pltpu.VMEM((seq, hd), BF16),
                        pltpu.VMEM((seq, hd), F32), pltpu.VMEM((seq, hd), F32),
                        pltpu.VMEM((2, hd, hd), F32), pltpu.VMEM((2, 1, hd), F32)],
        compiler_params=_cparams(2),
        name="mlstm",
    )(*args)
    if not emit_state:
        return outs[0], None
    hb, cst, nst, mst = outs
    mstate = mst[:, :, 0:2, 0].transpose(0, 2, 1)
    return hb, (cst, nst, mstate)


def _outproj_kernel(*refs, n_branch, final):
    it = iter(refs)
    branches = [(next(it), next(it), next(it)) for _ in range(n_branch)]
    w_ref, x_ref, mod_ref = next(it), next(it), next(it)
    fn_ref = next(it) if final else None
    y_ref, wb_ref = next(it), next(it)

    @pl.when(pl.program_id(0) == 0)
    def _():
        wb_ref[...] = w_ref[0].astype(BF16)

    out = None
    for n, (a_ref, zlo_ref, zhi_ref) in enumerate(branches):
        base = n * 2 * Z_BLOCK
        for part, z_ref in enumerate((zlo_ref, zhi_ref)):
            cols = slice(part * Z_BLOCK, (part + 1) * Z_BLOCK)
            gated = a_ref[:, cols] * z_ref[...]
            w = wb_ref[base + part * Z_BLOCK:base + (part + 1) * Z_BLOCK, :]
            p = jnp.dot(gated, w, preferred_element_type=F32)
            out = p if out is None else out + p
    y = x_ref[...] + mod_ref[0, 0, 2:3, :] * out
    if final:
        y = _rms(y) * fn_ref[...]
    y_ref[...] = y


def _outproj(branches, w_out, layer, x, mods, depth_idx, rows_per_mod, final_norm):
    m = x.shape[0]
    tm = 1024
    if rows_per_mod is None:
        mod_map = lambda i: (depth_idx, 0, 0, 0)
    else:
        mod_map = lambda i: (depth_idx, 1 + (i * tm) // rows_per_mod, 0, 0)
    in_specs, args = [], []
    for a, z, z_off in branches:
        zb = z_off // Z_BLOCK
        in_specs += [pl.BlockSpec((tm, 2 * Z_BLOCK), lambda i: (i, 0)),
                     pl.BlockSpec((tm, Z_BLOCK), lambda i, zb=zb: (i, zb)),
                     pl.BlockSpec((tm, Z_BLOCK), lambda i, zb=zb: (i, zb + 1))]
        args += [a, z, z]
    wk = w_out.shape[1]
    in_specs += [pl.BlockSpec((1, wk, D_MODEL), lambda i: (layer, 0, 0), pipeline_mode=pl.Buffered(1)),
                 pl.BlockSpec((tm, D_MODEL), lambda i: (i, 0)),
                 pl.BlockSpec((1, 1, 3, D_MODEL), mod_map)]
    args += [w_out, x, mods]
    final = final_norm is not None
    if final:
        in_specs.append(pl.BlockSpec((1, D_MODEL), lambda i: (0, 0)))
        args.append(final_norm.reshape(1, D_MODEL))
    return pl.pallas_call(
        functools.partial(_outproj_kernel, n_branch=len(branches), final=final),
        grid=(m // tm,),
        in_specs=in_specs,
        out_specs=pl.BlockSpec((tm, D_MODEL), lambda i: (i, 0)),
        out_shape=jax.ShapeDtypeStruct((m, D_MODEL), F32),
        scratch_shapes=[pltpu.VMEM((wk, D_MODEL), BF16)],
        compiler_params=_cparams(1, VMEM_LIMIT_WIDE),
        name="outproj",
    )(*args)


def _mla_slabs(kv, kr_slab):
    lane = lax.broadcasted_iota(jnp.int32, kv.shape, 1)
    nope = lane < C_NOPE
    keys = jnp.where(nope, kv, kr_slab).astype(BF16)
    values = jnp.where(nope, 1.0, kv).astype(BF16)
    return keys, values


def _inproj_c_kernel(*refs, rope):
    it = iter(refs)
    x_ref, mod_ref, g_ref, w_ref, qn_ref, wqb_ref, kvn_ref, wkvb_ref = (next(it) for _ in range(8))
    cos_ref, sin_ref = (next(it), next(it)) if rope else (None, None)
    q_ref, ks_ref, vs_ref, z_ref, ckv_ref, kr_ref, wb_ref = (next(it) for _ in range(7))

    @pl.when(pl.program_id(0) == 0)
    def _():
        wb_ref[...] = w_ref[0].astype(BF16)

    hn = _norm_modulate(x_ref[...], g_ref[0], mod_ref[0, 0]).astype(BF16)

    def project(lo, hi):
        return lax.dot_general(hn, wb_ref[lo:hi, :], _NT, preferred_element_type=F32)

    qa = _rms(project(CIN_QA, CIN_KVA)) * qn_ref[...]
    ckv = _rms(project(CIN_KVA, CIN_KR)) * kvn_ref[...]
    z_ref[...] = _silu(project(CIN_Z, CIN_WIDTH)).astype(BF16)
    tm = hn.shape[0]
    kr = jnp.concatenate([jnp.zeros((tm, C_NOPE), F32), project(CIN_KR, CIN_Z),
                          jnp.zeros((tm, C_HEAD_PAD - C_NOPE - C_ROPE), F32)], axis=1)
    ckv_ref[...] = ckv
    kr_ref[...] = kr
    kv = jnp.dot(ckv.astype(BF16), wkvb_ref[...], preferred_element_type=F32)
    scale = (C_NOPE + C_ROPE) ** -0.5 * LOG2E
    q = jnp.dot(qa.astype(BF16), wqb_ref[...], preferred_element_type=F32) * scale
    half = C_ROPE // 4
    if rope:
        cos = cos_ref[...]
        sin = sin_ref[...]
        kr = _rope(kr, cos, sin, half)
        for hd in range(C_HEADS):
            hs = slice(hd * C_HEAD_PAD, (hd + 1) * C_HEAD_PAD)
            q_ref[:, hs] = _rope(q[:, hs], cos, sin, half).astype(BF16)
    else:
        q_ref[...] = q.astype(BF16)
    for hd in range(C_HEADS):
        hs = slice(hd * C_HEAD_PAD, (hd + 1) * C_HEAD_PAD)
        ks_ref[:, hs], vs_ref[:, hs] = _mla_slabs(kv[:, hs], kr)


def _inproj_c(x, mods, norm_g, depth_idx, w_t, layer, q_norm, w_qb, kv_norm, w_kvb, rows_per_mod, tables):
    m = x.shape[0]
    tm = 512
    rope = tables is not None
    if rows_per_mod is None:
        mod_map = lambda i: (depth_idx, 0, 0, 0)
    else:
        mod_map = lambda i: (depth_idx, 1 + (i * tm) // rows_per_mod, 0, 0)
    const = lambda i: (0, 0)
    qw = C_HEADS * C_HEAD_PAD
    kvw = C_HEADS * (C_NOPE + C_VDIM)
    in_specs = [pl.BlockSpec((tm, D_MODEL), lambda i: (i, 0)),
                pl.BlockSpec((1, 1, 3, D_MODEL), mod_map),
                pl.BlockSpec((1, 1, D_MODEL), lambda i: (depth_idx, 0, 0)),
                pl.BlockSpec((1, CIN_WIDTH, D_MODEL), lambda i: (layer, 0, 0), pipeline_mode=pl.Buffered(1)),
                pl.BlockSpec((1, C_Q_RANK), const),
                pl.BlockSpec((C_Q_RANK, qw), const),
                pl.BlockSpec((1, C_KV_RANK), const),
                pl.BlockSpec((C_KV_RANK, kvw), const)]
    args = [x, mods, norm_g, w_t, q_norm, w_qb, kv_norm, w_kvb]
    if rope:
        tpb = rows_per_mod // tm
        in_specs += [pl.BlockSpec((tm, LANES), lambda i: (i % tpb, 0))] * 2
        args += list(tables)
    return pl.pallas_call(
        functools.partial(_inproj_c_kernel, rope=rope),
        grid=(m // tm,),
        in_specs=in_specs,
        out_specs=[pl.BlockSpec((tm, qw), lambda i: (i, 0)),
                   pl.BlockSpec((tm, kvw), lambda i: (i, 0)),
                   pl.BlockSpec((tm, kvw), lambda i: (i, 0)),
                   pl.BlockSpec((tm, C_WIDTH), lambda i: (i, 0)),
                   pl.BlockSpec((tm, C_KV_RANK), lambda i: (i, 0)),
                   pl.BlockSpec((tm, LANES), lambda i: (i, 0))],
        out_shape=[jax.ShapeDtypeStruct((m, qw), BF16),
                   jax.ShapeDtypeStruct((m, kvw), BF16),
                   jax.ShapeDtypeStruct((m, kvw), BF16),
                   jax.ShapeDtypeStruct((m, C_WIDTH), BF16),
                   jax.ShapeDtypeStruct((m, C_KV_RANK), F32),
                   jax.ShapeDtypeStruct((m, LANES), F32)],
        scratch_shapes=[pltpu.VMEM((CIN_WIDTH, D_MODEL), BF16)],
        compiler_params=_cparams(1),
        name="inproj_c",
    )(*args)


def _matmul_kernel(x_ref, w_ref, o_ref):
    o_ref[...] = jnp.dot(x_ref[...], w_ref[...], preferred_element_type=F32).astype(o_ref.dtype)


def _matmul(x, w, tm):
    m, k = x.shape
    n = w.shape[1]
    return pl.pallas_call(
        _matmul_kernel,
        grid=(m // tm,),
        in_specs=[pl.BlockSpec((tm, k), lambda i: (i, 0)),
                  pl.BlockSpec((k, n), lambda i: (0, 0))],
        out_specs=pl.BlockSpec((tm, n), lambda i: (i, 0)),
        out_shape=jax.ShapeDtypeStruct((m, n), BF16),
        compiler_params=_cparams(1),
        name="matmul",
    )(x, w)


def _attn_c_kernel(*refs, has_ctx):
    it = iter(refs)
    q_ref, kown_ref, vown_ref = next(it), next(it), next(it)
    kvc_ref, krc_ref = (next(it), next(it)) if has_ctx else (None, None)
    o_ref = next(it)
    kctx_ref, vctx_ref = (next(it), next(it)) if has_ctx else (None, None)

    if has_ctx:
        @pl.when(pl.program_id(1) == 0)
        def _():
            for h in range(C_HEADS):
                hs = slice(h * C_HEAD_PAD, (h + 1) * C_HEAD_PAD)
                kctx_ref[:, hs], vctx_ref[:, hs] = _mla_slabs(kvc_ref[:, hs], krc_ref[...])

    for h in range(C_HEADS):
        hs = slice(h * C_HEAD_PAD, (h + 1) * C_HEAD_PAD)
        qh = q_ref[:, hs]
        scores = [lax.dot_general(qh, kown_ref[:, hs], _NT, preferred_element_type=F32)]
        values = [vown_ref[:, hs]]
        if has_ctx:
            scores.append(lax.dot_general(qh, kctx_ref[:, hs], _NT, preferred_element_type=F32))
            values.append(vctx_ref[:, hs])
        o = _softmax_pv(scores, values, None)
        o_ref[:, h * C_VDIM:(h + 1) * C_VDIM] = o[:, C_NOPE:].astype(o_ref.dtype)


def _attn_c_dense_kernel(q_ref, ks_ref, vs_ref, o_ref, s_ref, e_ref):
    for h in range(C_HEADS):
        hs = slice(h * C_HEAD_PAD, (h + 1) * C_HEAD_PAD)
        s_ref[h] = lax.dot_general(q_ref[:, hs], ks_ref[:, hs], _NT, preferred_element_type=F32)
    s = s_ref[...]
    e_ref[...] = jnp.exp2(s - jnp.max(s, axis=-1, keepdims=True)).astype(BF16)
    for h in range(C_HEADS):
        hs = slice(h * C_HEAD_PAD, (h + 1) * C_HEAD_PAD)
        res = jnp.dot(e_ref[h], vs_ref[:, hs], preferred_element_type=F32)
        o = res / pltpu.roll(res, LANES // 2, 1)
        o_ref[:, h * C_VDIM:(h + 1) * C_VDIM] = o[:, C_NOPE:].astype(o_ref.dtype)


def _attn_c_dense(q, ks, vs, batch, seq):
    w = C_HEADS * C_HEAD_PAD
    return pl.pallas_call(
        _attn_c_dense_kernel,
        grid=(batch,),
        in_specs=[pl.BlockSpec((seq, w), lambda b: (b, 0))] * 3,
        out_specs=pl.BlockSpec((seq, C_WIDTH), lambda b: (b, 0)),
        out_shape=jax.ShapeDtypeStruct((batch * seq, C_WIDTH), BF16),
        scratch_shapes=[pltpu.VMEM((C_HEADS, seq, seq), F32), pltpu.VMEM((C_HEADS, seq, seq), BF16)],
        compiler_params=_cparams(1),
        name="attn_c_dense",
    )(q, ks, vs)


def _attn_c(q, ks, vs, ctx, batch, seq, tq):
    nq = seq // tq
    w = C_HEADS * C_HEAD_PAD
    has_ctx = ctx is not None
    in_specs = [pl.BlockSpec((tq, w), lambda b, i: (b * nq + i, 0)),
                pl.BlockSpec((seq, w), lambda b, i: (b, 0)),
                pl.BlockSpec((seq, w), lambda b, i: (b, 0))]
    args = [q, ks, vs]
    scratch = []
    if has_ctx:
        kv_ctx, kr_ctx = ctx
        nctx = kv_ctx.shape[0] // batch
        in_specs += [pl.BlockSpec((nctx, w), lambda b, i: (b, 0)),
                     pl.BlockSpec((nctx, LANES), lambda b, i: (b, 0))]
        args += [kv_ctx, kr_ctx]
        scratch += [pltpu.VMEM((nctx, w), BF16), pltpu.VMEM((nctx, w), BF16)]
    return pl.pallas_call(
        functools.partial(_attn_c_kernel, has_ctx=has_ctx),
        grid=(batch, nq),
        in_specs=in_specs,
        out_specs=pl.BlockSpec((tq, C_WIDTH), lambda b, i: (b * nq + i, 0)),
        out_shape=jax.ShapeDtypeStruct((batch * seq, C_WIDTH), BF16),
        scratch_shapes=scratch,
        compiler_params=_cparams(2),
        name="attn_c",
    )(*args)


def _rope_tables(n_tokens):
    pos_r = np.repeat(np.arange(n_tokens // GRID_W), GRID_W).astype(np.float64)
    pos_c = np.tile(np.arange(GRID_W), n_tokens // GRID_W).astype(np.float64)

    def seg(d_axis):
        half = d_axis // 2
        freqs = np.power(ROPE_BASE, -np.arange(half, dtype=np.float64) / half)
        cos, sin = [], []
        for pos in (pos_r, pos_c):
            ang = pos[:, None] * freqs[None, :]
            cos += [np.cos(ang), np.cos(ang)]
            sin += [-np.sin(ang), np.sin(ang)]
        return np.concatenate(cos, axis=1), np.concatenate(sin, axis=1)

    cos_a, sin_a = seg(A_HEAD_DIM // 2)
    cos_a, sin_a = np.tile(cos_a, (1, 2)), np.tile(sin_a, (1, 2))
    cos_r, sin_r = seg(C_ROPE // 2)
    ones = np.ones((n_tokens, C_NOPE))
    pad = C_HEAD_PAD - C_NOPE - C_ROPE
    cos_c = np.concatenate([ones, cos_r, np.ones((n_tokens, pad))], axis=1)
    sin_c = np.concatenate([0 * ones, sin_r, np.zeros((n_tokens, pad))], axis=1)
    f = lambda a: jnp.asarray(a, dtype=F32)
    return (f(cos_a), f(sin_a)), (f(cos_c), f(sin_c))


def _pad_cols(w, left, total):
    return jnp.pad(w, ((0, 0), (left, total - left - w.shape[1])))


def kernel(x_prompt, x_sample, cache_a_k, cache_a_v, state_b_mem, state_b_norm, state_b_max, cache_c_kv, cache_c_krope, c, c_ctx, norm_g, w_mod, b_mod, w_in_ab, sink_a, conv_b, gate_bias_b, norm_b, w_out_ab, w_in_c, q_norm_c, w_qb_c, kv_norm_c, w_kvb_c, w_out_c, final_norm):
    bp, tp, _ = x_prompt.shape
    bs, ts, _ = x_sample.shape
    past = cache_a_k.shape[2]
    tables_a, tables_c = _rope_tables(ts)
    w_ab_t = jnp.swapaxes(w_in_ab, 1, 2)
    w_c_t = jnp.swapaxes(w_in_c, 1, 2)

    cond = jnp.zeros((16, D_MODEL), F32).at[0].set(c_ctx).at[1:1 + bs].set(c)
    mods = _adaln(cond, w_mod, b_mod).reshape(DEPTH, 16, 3, D_MODEL)
    gains = norm_g.reshape(DEPTH, 1, D_MODEL)

    yp = x_prompt.reshape(bp * tp, D_MODEL)
    ys = x_sample.reshape(bs * ts, D_MODEL)
    a_k, a_v, b_max, c_kvs, c_krs = [], [], [], [], []
    states = None
    for l in range(DEPTH):
        j = l // 2
        fin = final_norm if l == DEPTH - 1 else None
        if l % 2 == 0:
            w_bf = _cast_rows(w_ab_t, j, MAIN_WIDTH)
            main_p, k_p, v_p, gates_p = _inproj_ab(yp, mods, gains, l, w_bf, w_ab_t, j, tp, True, True, None)
            main_s, gates_s = _inproj_ab(ys, mods, gains, l, w_bf, w_ab_t, j, ts, False, False, tables_a)
            a_k.append(k_p)
            a_v.append(v_p)

            attn_p = _attn_a_prompt(main_p, sink_a[j], bp, tp)
            ck = cache_a_k[:, j].reshape(bs, past, A_KV_WIDTH).astype(BF16)
            cv = cache_a_v[:, j].reshape(bs, past, A_KV_WIDTH).astype(BF16)
            attn_s = _attn_a_sample(main_s, sink_a[j], ck, cv, bs, ts)

            prev = None if states is None else states[:2]
            hb_p, states = _mlstm(main_p, gates_p, conv_b, gate_bias_b, norm_b, None, j, bp, tp, True, prev)
            init = (state_b_mem, state_b_norm, state_b_max)
            hb_s, _ = _mlstm(main_s, gates_s, conv_b, gate_bias_b, norm_b, init, j, bs, ts, False, None)
            b_max.append(states[2])

            yp = _outproj([(attn_p, main_p, MAIN_ZA), (hb_p, main_p, MAIN_ZB)], w_out_ab, j, yp, mods, l, None, fin)
            ys = _outproj([(attn_s, main_s, MAIN_ZA), (hb_s, main_s, MAIN_ZB)], w_out_ab, j, ys, mods, l, ts, fin)
        else:
            wq = w_qb_c[j].reshape(C_Q_RANK, C_HEADS, C_NOPE + C_ROPE)
            wq = jnp.pad(wq, ((0, 0), (0, 0), (0, C_HEAD_PAD - C_NOPE - C_ROPE)))
            wq = wq.reshape(C_Q_RANK, C_HEADS * C_HEAD_PAD).astype(BF16)
            wkv = w_kvb_c[j].astype(BF16)
            qn = q_norm_c[j].reshape(1, C_Q_RANK)
            kvn = kv_norm_c[j].reshape(1, C_KV_RANK)

            q_p, ks_p, vs_p, z_p, ckv_p, kr_p = _inproj_c(yp, mods, gains, l, w_c_t, j, qn, wq, kvn, wkv, None, None)
            q_s, ks_s, vs_s, z_s, _, _ = _inproj_c(ys, mods, gains, l, w_c_t, j, qn, wq, kvn, wkv, ts, tables_c)
            c_kvs.append(ckv_p.reshape(bp, tp, C_KV_RANK))
            c_krs.append(kr_p[:, C_NOPE:C_NOPE + C_ROPE].reshape(bp, tp, C_ROPE))

            cc = cache_c_kv[:, j].reshape(bs * past, C_KV_RANK).astype(BF16)
            kv_ctx = _matmul(cc, wkv, 512)
            kr_ctx = _pad_cols(cache_c_krope[:, j].reshape(bs * past, C_ROPE), C_NOPE, C_HEAD_PAD).astype(BF16)

            attn_p = _attn_c_dense(q_p, ks_p, vs_p, bp, tp)
            attn_s = _attn_c(q_s, ks_s, vs_s, (kv_ctx, kr_ctx), bs, ts, 512)

            yp = _outproj([(attn_p, z_p, 0)], w_out_c, j, yp, mods, l, None, fin)
            ys = _outproj([(attn_s, z_s, 0)], w_out_c, j, ys, mods, l, ts, fin)

    def cache_layout(per_layer):
        stacked = jnp.stack(per_layer, axis=1).reshape(bp, N_EVEN, A_KV_HEADS, A_HEAD_DIM, tp)
        return stacked.transpose(0, 1, 4, 2, 3)

    b_mem = states[0]
    b_nrm = states[1].reshape(bp, N_EVEN, 2, B_HEADS, B_HEAD_DIM)
    return (yp.reshape(bp, tp, D_MODEL), ys.reshape(bs, ts, D_MODEL),
            cache_layout(a_k), cache_layout(a_v), b_mem, b_nrm,
            jnp.stack(b_max, axis=1), jnp.stack(c_kvs, axis=1), jnp.stack(c_krs, axis=1))
```

```python
import functools
import math

import numpy as np
import jax
import jax.numpy as jnp
from jax import lax
from jax.experimental import pallas as pl
from jax.experimental.pallas import tpu as pltpu

F32 = jnp.float32
BF16 = jnp.bfloat16

D_MODEL = 1024
DEPTH = 4
N_EVEN = 2
EPS = 1e-6
ROPE_BASE = 10000.0
NEG_INF = -1e30
GRID_W = 64
LOG2E = math.log2(math.e)
A_HEADS = 16
A_KV_HEADS = 4
A_GROUP = A_HEADS // A_KV_HEADS
A_HEAD_DIM = 64
A_WIDTH = A_HEADS * A_HEAD_DIM
A_KV_WIDTH = A_KV_HEADS * A_HEAD_DIM
WINDOW = 128
BLOCK = 128
Q_BLOCKS = 4
B_HEADS = 4
B_HEAD_DIM = 256
B_WIDTH = B_HEADS * B_HEAD_DIM
B_CHUNK = 256
C_HEADS = 16
C_NOPE = 64
C_ROPE = 32
C_VDIM = 64
C_Q_RANK = 384
C_KV_RANK = 256
C_WIDTH = C_HEADS * C_VDIM
C_HEAD_PAD = 128

LANES = 128
SUBLANES = 8
B_CONV = 3
MAIN_QA, MAIN_KA, MAIN_VA, MAIN_ZA, MAIN_QB, MAIN_KB, MAIN_VB, MAIN_OB, MAIN_ZB = (
    0, 1024, 1280, 1536, 2560, 3584, 4608, 5632, 6656)
MAIN_WIDTH = 7680
CIN_QA, CIN_KVA, CIN_KR, CIN_Z, CIN_WIDTH = 0, 384, 640, 672, 1696
Z_BLOCK = 512
QK_TILE = 1280
A_QSCALE = A_HEAD_DIM ** -0.5 * LOG2E

MIB = 1024 * 1024
VMEM_LIMIT = 48 * MIB
VMEM_LIMIT_WIDE = 56 * MIB

_NT = (((1,), (1,)), ((), ()))
_TN = (((0,), (0,)), ((), ()))


def _cparams(n_axes, vmem=VMEM_LIMIT):
    return pltpu.CompilerParams(dimension_semantics=("arbitrary",) * n_axes,
                                vmem_limit_bytes=vmem)


def _silu(x):
    return x * jax.nn.sigmoid(x)


def _log_sigmoid(x):
    return jnp.minimum(x, 0.0) - jnp.log1p(jnp.exp(-jnp.abs(x)))


def _rms(x):
    return x * lax.rsqrt(jnp.mean(x * x, axis=-1, keepdims=True) + EPS)


def _norm_modulate(x, g, mod):
    y = _rms(x) * g
    return y * (1.0 + mod[1:2, :]) + mod[0:1, :]


def _swap_halves(x, half):
    lane = lax.broadcasted_iota(jnp.int32, x.shape, 1)
    first = (lane % (2 * half)) < half
    return jnp.where(first, pltpu.roll(x, LANES - half, 1), pltpu.roll(x, half, 1))


def _rope(x, cos, sin, half):
    return x * cos + _swap_halves(x, half) * sin


def _softmax_pv(scores, values, sink):
    tiles = [s[:, t * LANES:(t + 1) * LANES] for s in scores for t in range(s.shape[1] // LANES)]
    m = jnp.max(functools.reduce(jnp.maximum, tiles), axis=-1, keepdims=True)
    if sink is not None:
        m = jnp.maximum(m, sink)
    res = functools.reduce(jnp.add, [
        jnp.dot(jnp.exp2(s - m).astype(BF16), v, preferred_element_type=F32)
        for s, v in zip(scores, values)])
    den = pltpu.roll(res, LANES // 2, 1)
    if sink is not None:
        den = den + jnp.exp2(sink - m)
    return res / den


def _adaln_kernel(c_ref, w_ref, b_ref, o_ref):
    a = _silu(c_ref[...]).astype(BF16)
    w = w_ref[0].astype(BF16)
    o_ref[0] = jnp.dot(a, w, preferred_element_type=F32) + b_ref[0]


def _adaln(cond, w_mod, b_mod):
    tn = 1024
    n = 3 * D_MODEL
    return pl.pallas_call(
        _adaln_kernel,
        grid=(DEPTH, n // tn),
        in_specs=[pl.BlockSpec((16, D_MODEL), lambda l, j: (0, 0)),
                  pl.BlockSpec((1, D_MODEL, tn), lambda l, j: (l, 0, j)),
                  pl.BlockSpec((1, 1, tn), lambda l, j: (l, 0, j))],
        out_specs=pl.BlockSpec((1, 16, tn), lambda l, j: (l, 0, j)),
        out_shape=jax.ShapeDtypeStruct((DEPTH, 16, n), F32),
        compiler_params=_cparams(2),
        name="adaln",
    )(cond, w_mod, b_mod.reshape(DEPTH, 1, n))


def _cast_kernel(w_ref, o_ref):
    o_ref[...] = w_ref[0].astype(o_ref.dtype)


def _cast_rows(w_t, layer, rows):
    tr = QK_TILE
    d = w_t.shape[2]
    return pl.pallas_call(
        _cast_kernel,
        grid=(rows // tr,),
        in_specs=[pl.BlockSpec((1, tr, d), lambda i: (layer, i, 0))],
        out_specs=pl.BlockSpec((tr, d), lambda i: (i, 0)),
        out_shape=jax.ShapeDtypeStruct((rows, d), BF16),
        compiler_params=_cparams(1),
        name="cast_rows",
    )(w_t)


def _main_chunk_kind(col):
    bounds = ((MAIN_KA, "q"), (MAIN_VA, "k"), (MAIN_ZA, "plain"), (MAIN_QB, "silu"), (MAIN_OB, "plain"),
              (MAIN_ZB, "sigmoid"), (MAIN_WIDTH, "silu"))
    return next(kind for end, kind in bounds if col < end)


def _inproj_ab_kernel(*refs, emit_kv, rope):
    it = iter(refs)
    x_ref, mod_ref, g_ref, wb_ref, wg_ref = (next(it) for _ in range(5))
    cos_ref, sin_ref = (next(it), next(it)) if rope else (None, None)
    main_ref = next(it)
    k_ref, v_ref = (next(it), next(it)) if emit_kv else (None, None)
    gate_ref = next(it)

    mod = mod_ref[0, 0]
    g = g_ref[0]
    hn = _norm_modulate(x_ref[...], g, mod).astype(BF16)
    gate_ref[...] = lax.dot_general(wg_ref[0].astype(BF16), hn, _NT, preferred_element_type=F32)
    if emit_kv:
        kv_t = lax.dot_general(wb_ref[MAIN_KA:MAIN_ZA, :], hn, _NT, preferred_element_type=F32)
        seq = k_ref.shape[2]
        for n in range(k_ref.shape[0]):
            k_ref[n] = kv_t[:A_KV_WIDTH, n * seq:(n + 1) * seq]
            v_ref[n] = kv_t[A_KV_WIDTH:, n * seq:(n + 1) * seq]
    for t in range(MAIN_WIDTH // QK_TILE):
        res = lax.dot_general(hn, wb_ref[t * QK_TILE:(t + 1) * QK_TILE, :], _NT, preferred_element_type=F32)
        for c in range(QK_TILE // LANES):
            col = t * QK_TILE + c * LANES
            kind = _main_chunk_kind(col)
            chunk = res[:, c * LANES:(c + 1) * LANES]
            if kind in ("q", "k"):
                if rope:
                    chunk = _rope(chunk, cos_ref[...], sin_ref[...], A_HEAD_DIM // 4)
                if kind == "q":
                    chunk = chunk * A_QSCALE
            elif kind == "silu":
                chunk = _silu(chunk)
            elif kind == "sigmoid":
                chunk = jax.nn.sigmoid(chunk)
            main_ref[:, col:col + LANES] = chunk.astype(BF16)


def _inproj_ab(x, mods, norm_g, depth_idx, w_bf, w_t, layer, seq, shared_mod, emit_kv, tables):
    m = x.shape[0]
    tm = 512
    rope = tables is not None
    if shared_mod:
        mod_map = lambda i: (depth_idx, 0, 0, 0)
    else:
        mod_map = lambda i: (depth_idx, 1 + (i * tm) // seq, 0, 0)
    n_gate = 4 * B_HEADS
    tiles_per_table = max(seq, tm) // tm
    table_spec = pl.BlockSpec((tm, LANES), lambda i: (i % tiles_per_table, 0))
    in_specs = [pl.BlockSpec((tm, D_MODEL), lambda i: (i, 0)),
                pl.BlockSpec((1, 1, 3, D_MODEL), mod_map),
                pl.BlockSpec((1, 1, D_MODEL), lambda i: (depth_idx, 0, 0)),
                pl.BlockSpec((MAIN_WIDTH, D_MODEL), lambda i: (0, 0), pipeline_mode=pl.Buffered(1)),
                pl.BlockSpec((1, n_gate, D_MODEL), lambda i: (layer, MAIN_WIDTH // n_gate, 0))]
    args = [x, mods, norm_g, w_bf, w_t]
    if rope:
        in_specs += [table_spec] * 2
        args += list(tables)
    out_specs = [pl.BlockSpec((tm, MAIN_WIDTH), lambda i: (i, 0))]
    out_shape = [jax.ShapeDtypeStruct((m, MAIN_WIDTH), BF16)]
    if emit_kv:
        out_specs += [pl.BlockSpec((tm // seq, A_KV_WIDTH, seq), lambda i: (i, 0, 0))] * 2
        out_shape += [jax.ShapeDtypeStruct((m // seq, A_KV_WIDTH, seq), F32)] * 2
    out_specs.append(pl.BlockSpec((n_gate, tm), lambda i: (0, i)))
    out_shape.append(jax.ShapeDtypeStruct((n_gate, m), F32))
    return pl.pallas_call(
        functools.partial(_inproj_ab_kernel, emit_kv=emit_kv, rope=rope),
        grid=(m // tm,),
        in_specs=in_specs,
        out_specs=out_specs,
        out_shape=out_shape,
        compiler_params=_cparams(1, VMEM_LIMIT_WIDE),
        name="inproj_ab",
    )(*args)


def _value_slabs(v_ref, slab_ref):
    keys = v_ref.shape[0]
    ones = jnp.ones((keys, LANES - A_HEAD_DIM), BF16)
    for g in range(A_KV_HEADS):
        slab_ref[:, g * LANES:g * LANES + A_HEAD_DIM] = v_ref[:, g * A_HEAD_DIM:(g + 1) * A_HEAD_DIM]
        slab_ref[:, g * LANES + A_HEAD_DIM:(g + 1) * LANES] = ones


def _group_queries(q_ref, g):
    return jnp.concatenate([q_ref[:, (g * A_GROUP + hh) * A_HEAD_DIM:(g * A_GROUP + hh + 1) * A_HEAD_DIM]
                            for hh in range(A_GROUP)], axis=0)


def _sink_softmax(s_ref, e_ref, t_ref, sink_ref):
    s = s_ref[...]
    sink = sink_ref[...] * LOG2E
    m = jnp.maximum(jnp.max(s, axis=-1, keepdims=True), sink)
    e_ref[...] = jnp.exp2(s - m).astype(BF16)
    t_ref[...] = jnp.exp2(sink - m)


def _attn_a_prompt_kernel(sink_ref, q_ref, k_ref, v_ref, o_ref, s_ref, e_ref, t_ref, vs_ref):
    _value_slabs(v_ref, vs_ref)
    seq = q_ref.shape[0]
    rows = A_GROUP * seq
    for g in range(A_KV_HEADS):
        heads = slice(g * A_GROUP, (g + 1) * A_GROUP)
        ks = slice(g * A_HEAD_DIM, (g + 1) * A_HEAD_DIM)
        s = lax.dot_general(_group_queries(q_ref, g), k_ref[:, ks], _NT, preferred_element_type=F32)
        s_ref[heads] = s.reshape(A_GROUP, seq, seq)
    _sink_softmax(s_ref, e_ref, t_ref, sink_ref)
    for g in range(A_KV_HEADS):
        heads = slice(g * A_GROUP, (g + 1) * A_GROUP)
        e = e_ref[heads].reshape(rows, seq)
        res = jnp.dot(e, vs_ref[:, g * LANES:(g + 1) * LANES], preferred_element_type=F32)
        den = jnp.dot(e, jnp.ones((seq, LANES), BF16), preferred_element_type=F32)
        o = res / (den + t_ref[heads].reshape(rows, 1))
        for hh in range(A_GROUP):
            h = g * A_GROUP + hh
            o_ref[:, h * A_HEAD_DIM:(h + 1) * A_HEAD_DIM] = (
                o[hh * seq:(hh + 1) * seq, :A_HEAD_DIM].astype(o_ref.dtype))


def _attn_a_prompt(main, sink, batch, seq):
    kb = MAIN_KA // A_KV_WIDTH
    vb = MAIN_VA // A_KV_WIDTH
    return pl.pallas_call(
        _attn_a_prompt_kernel,
        grid=(batch,),
        in_specs=[pl.BlockSpec((A_HEADS, 1, 1), lambda b: (0, 0, 0)),
                  pl.BlockSpec((seq, A_WIDTH), lambda b: (b, MAIN_QA // A_WIDTH)),
                  pl.BlockSpec((seq, A_KV_WIDTH), lambda b: (b, kb)),
                  pl.BlockSpec((seq, A_KV_WIDTH), lambda b: (b, vb))],
        out_specs=pl.BlockSpec((seq, A_WIDTH), lambda b: (b, 0)),
        out_shape=jax.ShapeDtypeStruct((batch * seq, A_WIDTH), BF16),
        scratch_shapes=[pltpu.VMEM((A_HEADS, seq, seq), F32),
                        pltpu.VMEM((A_HEADS, seq, seq), BF16),
                        pltpu.VMEM((A_HEADS, seq, 1), F32),
                        pltpu.VMEM((seq, A_KV_HEADS * LANES), BF16)],
        compiler_params=_cparams(1),
        name="attn_a_prompt",
    )(sink.reshape(A_HEADS, 1, 1), main, main, main)


def _attn_a_sample_kernel(sink_ref, q_ref, k_ref, v_ref, ck_ref, cv_ref, bias_ref, o_ref, vs_ref, cvs_ref):
    i = pl.program_id(1)
    seq = k_ref.shape[0]
    span = 3 * BLOCK

    @pl.when(i == 0)
    def _():
        _value_slabs(v_ref, vs_ref)
        _value_slabs(cv_ref.at[0], cvs_ref)

    ck = ck_ref[0]
    rows = A_GROUP * BLOCK
    head_of_row = lax.broadcasted_iota(jnp.int32, (rows, 1), 0) // BLOCK
    for blk in range(Q_BLOCKS):
        qi = i * Q_BLOCKS + blk
        qrows = slice(blk * BLOCK, (blk + 1) * BLOCK)
        start = pl.multiple_of(jnp.clip((qi - 1) * BLOCK, 0, seq - span), BLOCK)
        kw = k_ref[pl.ds(start, span), :]
        vw = vs_ref[pl.ds(start, span), :]
        bias = bias_ref[(qi * BLOCK - start) // BLOCK]
        bias = jnp.concatenate([bias] * A_GROUP, axis=0)
        for g in range(A_KV_HEADS):
            ks = slice(g * A_HEAD_DIM, (g + 1) * A_HEAD_DIM)
            gs = slice(g * LANES, (g + 1) * LANES)
            sink = jnp.zeros((rows, 1), F32)
            for hh in range(A_GROUP):
                sink = jnp.where(head_of_row == hh, sink_ref[g * A_GROUP + hh] * LOG2E, sink)
            qg = _group_queries(q_ref.at[qrows], g)
            s_loc = lax.dot_general(qg, kw[:, ks], _NT, preferred_element_type=F32) + bias
            s_ctx = lax.dot_general(qg, ck[:, ks], _NT, preferred_element_type=F32)
            o = _softmax_pv([s_loc, s_ctx], [vw[:, gs], cvs_ref[:, gs]], sink)
            for hh in range(A_GROUP):
                h = g * A_GROUP + hh
                o_ref[qrows, h * A_HEAD_DIM:(h + 1) * A_HEAD_DIM] = (
                    o[hh * BLOCK:(hh + 1) * BLOCK, :A_HEAD_DIM].astype(o_ref.dtype))


def _window_bias():
    r = np.arange(BLOCK)[:, None]
    c = np.arange(3 * BLOCK)[None, :]
    masks = [np.where(np.abs(off + r - c) <= WINDOW, 0.0, NEG_INF) for off in (0, BLOCK, 2 * BLOCK)]
    return jnp.asarray(np.stack(masks), dtype=F32)


def _attn_a_sample(main, sink, ck, cv, batch, seq):
    tq = BLOCK * Q_BLOCKS
    nb = seq // tq
    kb = MAIN_KA // A_KV_WIDTH
    vb = MAIN_VA // A_KV_WIDTH
    ctx = ck.shape[1]
    slab_w = A_KV_HEADS * LANES
    return pl.pallas_call(
        _attn_a_sample_kernel,
        grid=(batch, nb),
        in_specs=[pl.BlockSpec(memory_space=pltpu.SMEM),
                  pl.BlockSpec((tq, A_WIDTH), lambda b, i: (b * nb + i, MAIN_QA // A_WIDTH)),
                  pl.BlockSpec((seq, A_KV_WIDTH), lambda b, i: (b, kb)),
                  pl.BlockSpec((seq, A_KV_WIDTH), lambda b, i: (b, vb)),
                  pl.BlockSpec((1, ctx, A_KV_WIDTH), lambda b, i: (b, 0, 0)),
                  pl.BlockSpec((1, ctx, A_KV_WIDTH), lambda b, i: (b, 0, 0)),
                  pl.BlockSpec((3, BLOCK, 3 * BLOCK), lambda b, i: (0, 0, 0))],
        out_specs=pl.BlockSpec((tq, A_WIDTH), lambda b, i: (b * nb + i, 0)),
        out_shape=jax.ShapeDtypeStruct((batch * seq, A_WIDTH), BF16),
        scratch_shapes=[pltpu.VMEM((seq, slab_w), BF16),
                        pltpu.VMEM((ctx, slab_w), BF16)],
        compiler_params=_cparams(2),
        name="attn_a_sample",
    )(sink, main, main, main, ck, cv, _window_bias())


def _mlstm_kernel(*refs, seq, chunk, layer, hps, has_init, emit_state, has_prev):
    it = iter(refs)
    bias_ref = next(it)
    m0_ref = next(it) if has_init else None
    take = lambda: [next(it) for _ in range(hps)]
    q_refs, k_refs, v_refs, o_refs = take(), take(), take(), take()
    g_ref = next(it)
    cwq_refs, cwk_refs = take(), take()
    nw_ref = next(it)
    c0_ref, n0_ref = (next(it), next(it)) if has_init else (None, None)
    cprev_ref, nprev_ref = (next(it), next(it)) if has_prev else (None, None)
    h_ref = next(it)
    cst_out, nst_out, mst_out = (next(it), next(it), next(it)) if emit_state else (None, None, None)
    qs_ref, ks_ref, hf_ref, hb_ref, cst_ref, nst_ref = (next(it) for _ in range(6))

    b = pl.program_id(0)
    head0 = pl.program_id(1) * hps
    nc = seq // chunk
    hd = B_HEAD_DIM

    edge = lax.broadcasted_iota(jnp.int32, (SUBLANES, 1), 0)
    drop_first = (edge != 0).astype(F32)
    drop_last = (edge != SUBLANES - 1).astype(F32)
    rr = lax.broadcasted_iota(jnp.int32, (chunk, chunk), 0)
    cc = lax.broadcasted_iota(jnp.int32, (chunk, chunk), 1)
    diag = rr == cc

    def conv_silu(x_ref, w_ref, scale):
        x = x_ref[...].astype(F32)
        w = w_ref[0]
        x_up = jnp.concatenate([x[:seq - SUBLANES], x[seq - SUBLANES:] * drop_last], axis=0)
        x_dn = jnp.concatenate([x[:SUBLANES] * drop_first, x[SUBLANES:]], axis=0)
        y = pltpu.roll(x_up, 1, 0) * w[0:1] + x * w[1:2] + pltpu.roll(x_dn, seq - 1, 0) * w[2:3]
        return (_silu(y) * scale).astype(BF16)

    for hh in range(hps):
        h = head0 + hh
        v_ref = v_refs[hh]
        qs_ref[hh] = conv_silu(q_refs[hh], cwq_refs[hh], 1.0)
        ks_ref[hh] = conv_silu(k_refs[hh], cwk_refs[hh], hd ** -0.5)

        def gate_row(kind):
            ch = kind * B_HEADS + h
            return g_ref[pl.ds(ch, 1), :] + bias_ref[layer, ch]

        li = [gate_row(2 * d) for d in range(2)]
        lf = [_log_sigmoid(gate_row(2 * d + 1)) for d in range(2)]

        def chunk_step(c, d, m_prev, first):
            rows = pl.ds(c * chunk, chunk)
            lanes = slice(c * chunk, (c + 1) * chunk)
            qc = qs_ref[hh, rows, :]
            kc = ks_ref[hh, rows, :]
            vc = v_ref[rows, :]
            li_row = li[d][:, lanes]
            lf_row = lf[d][:, lanes]
            causal = (cc <= rr) if d == 0 else (cc >= rr)
            b_col = jnp.sum(jnp.where(causal, lf_row, 0.0), axis=1, keepdims=True)
            b_row = jnp.sum(jnp.where(diag, b_col, 0.0), axis=0, keepdims=True)
            a_row = li_row - b_row
            a_col = jnp.sum(jnp.where(diag, a_row, 0.0), axis=1, keepdims=True)
            total = jnp.sum(lf_row, axis=1, keepdims=True)
            log_d = jnp.where(causal, b_col + a_row, NEG_INF)
            log_init = b_col + m_prev
            m_t = jnp.maximum(log_init, jnp.max(log_d, axis=1, keepdims=True))
            d_mat = jnp.exp(log_d - m_t)
            s = lax.dot_general(qc, kc, _NT, preferred_element_type=F32) * d_mat
            num = jnp.dot(s.astype(BF16), vc, preferred_element_type=F32)
            den = jnp.sum(s, axis=1, keepdims=True)
            if not first:
                w_init = jnp.exp(log_init - m_t)
                num = num + w_init * jnp.dot(qc, cst_ref[hh, d].astype(BF16), preferred_element_type=F32)
                n_rows = jnp.broadcast_to(nst_ref[hh, d], (SUBLANES, hd)).astype(BF16)
                den = den + w_init * lax.dot_general(qc, n_rows, _NT, preferred_element_type=F32)[:, 0:1]
            hc = num / jnp.maximum(jnp.abs(den), jnp.exp(-m_t))
            (hf_ref if d == 0 else hb_ref)[hh, rows, :] = hc
            log_w = total + a_col
            m_new = jnp.maximum(total + m_prev, jnp.max(log_w, axis=0, keepdims=True))
            kw = kc.astype(F32) * jnp.exp(log_w - m_new)
            c_add = lax.dot_general(kw.astype(BF16), vc, _TN, preferred_element_type=F32)
            n_add = jnp.sum(kw, axis=0, keepdims=True)
            if first:
                cst_ref[hh, d] = c_add
                nst_ref[hh, d] = n_add
            else:
                w_0 = jnp.exp(total + m_prev - m_new)
                cst_ref[hh, d] = w_0 * cst_ref[hh, d] + c_add
                nst_ref[hh, d] = w_0 * nst_ref[hh, d] + n_add
            return m_new

        if has_init:
            for d in range(2):
                cst_ref[hh, d] = c0_ref[0, 0, d, hh]
                nst_ref[hh, d] = n0_ref[0, 0, d, hh]
            m = [jnp.full((1, 1), m0_ref[b, layer, d, h], F32) for d in range(2)]
        else:
            m = [jnp.zeros((1, 1), F32) for _ in range(2)]
        for step in range(nc):
            first = (step == 0) and not has_init
            m[0] = chunk_step(step, 0, m[0], first)
            m[1] = chunk_step(nc - 1 - step, 1, m[1], first)

        hsl = slice(hh * hd, (hh + 1) * hd)
        hhat = _rms(o_refs[hh][...].astype(F32) * (hf_ref[hh] + hb_ref[hh]))
        h_ref[:, hsl] = (hhat * nw_ref[0, :, hsl]).astype(h_ref.dtype)

        if emit_state:
            row = lax.broadcasted_iota(jnp.int32, (8, LANES), 0)
            mst_out[0, hh] = jnp.where(row == 0, m[0], m[1])
            if has_prev:
                for d in range(2):
                    cst_out[0, 0, d, hh] = cprev_ref[0, d, hh]
                    nst_out[0, 0, d, hh] = nprev_ref[0, d, hh]
                    cst_out[0, 1, d, hh] = cst_ref[hh, d]
                    nst_out[0, 1, d, hh] = nst_ref[hh, d]
            else:
                for d in range(2):
                    cst_out[0, d, hh] = cst_ref[hh, d]
                    nst_out[0, d, hh] = nst_ref[hh, d]


def _mlstm(main, gates, conv_w, gate_bias, norm_w, init, layer, batch, seq, emit_state, prev, hps):
    chunk = B_CHUNK
    hd = B_HEAD_DIM
    has_init = init is not None
    has_prev = prev is not None

    def head_cols(off):
        return [pl.BlockSpec((seq, hd), lambda b, g, hh=hh: (b, off // hd + g * hps + hh)) for hh in range(hps)]

    def conv_cols(off):
        return [pl.BlockSpec((1, B_CONV, hd), lambda b, g, hh=hh: (layer, 0, off + g * hps + hh))
                for hh in range(hps)]

    in_specs = [pl.BlockSpec(memory_space=pltpu.SMEM)]
    args = [gate_bias]
    if has_init:
        c0, n0, m0 = init
        in_specs.append(pl.BlockSpec(memory_space=pltpu.SMEM))
        args.append(m0)
    in_specs += (head_cols(MAIN_QB) + head_cols(MAIN_KB) + head_cols(MAIN_VB) + head_cols(MAIN_OB)
                 + [pl.BlockSpec((4 * B_HEADS, seq), lambda b, g: (0, b))]
                 + conv_cols(0) + conv_cols(B_HEADS)
                 + [pl.BlockSpec((1, 1, hps * hd), lambda b, g: (layer, 0, g))])
    args += [main] * (4 * hps) + [gates] + [conv_w] * (2 * hps) + [norm_w.reshape(N_EVEN, 1, B_WIDTH)]
    if has_init:
        in_specs += [pl.BlockSpec((1, 1, 2, hps, hd, hd), lambda b, g: (b, layer, 0, g, 0, 0)),
                     pl.BlockSpec((1, 1, 2, hps, 1, hd), lambda b, g: (b, layer, 0, g, 0, 0))]
        args += [c0, n0.reshape(n0.shape[:4] + (1, hd))]
    if has_prev:
        in_specs += [pl.BlockSpec((1, 2, hps, hd, hd), lambda b, g: (b, 0, g, 0, 0)),
                     pl.BlockSpec((1, 2, hps, 1, hd), lambda b, g: (b, 0, g, 0, 0))]
        args += list(prev)
    out_specs = [pl.BlockSpec((seq, hps * hd), lambda b, g: (b, g))]
    out_shape = [jax.ShapeDtypeStruct((batch * seq, B_WIDTH), BF16)]
    if emit_state:
        if has_prev:
            out_specs += [pl.BlockSpec((1, N_EVEN, 2, hps, hd, hd), lambda b, g: (b, 0, 0, g, 0, 0)),
                          pl.BlockSpec((1, N_EVEN, 2, hps, 1, hd), lambda b, g: (b, 0, 0, g, 0, 0))]
            out_shape += [jax.ShapeDtypeStruct((batch, N_EVEN, 2, B_HEADS, hd, hd), F32),
                          jax.ShapeDtypeStruct((batch, N_EVEN, 2, B_HEADS, 1, hd), F32)]
        else:
            out_specs += [pl.BlockSpec((1, 2, hps, hd, hd), lambda b, g: (b, 0, g, 0, 0)),
                          pl.BlockSpec((1, 2, hps, 1, hd), lambda b, g: (b, 0, g, 0, 0))]
            out_shape += [jax.ShapeDtypeStruct((batch, 2, B_HEADS, hd, hd), F32),
                          jax.ShapeDtypeStruct((batch, 2, B_HEADS, 1, hd), F32)]
        out_specs.append(pl.BlockSpec((1, hps, 8, LANES), lambda b, g: (b, g, 0, 0)))
        out_shape.append(jax.ShapeDtypeStruct((batch, B_HEADS, 8, LANES), F32))
    outs = pl.pallas_call(
        functools.partial(_mlstm_kernel, seq=seq, chunk=chunk, layer=layer, hps=hps, has_init=has_init,
                          emit_state=emit_state, has_prev=has_prev),
        grid=(batch, B_HEADS // hps),
        in_specs=in_specs,
        out_specs=out_specs,
        out_shape=out_shape,
        scratch_shapes=[pltpu.VMEM((hps, seq, hd), BF16), pltpu.VMEM((hps, seq, hd), BF16),
                        pltpu.VMEM((hps, seq, hd), F32), pltpu.VMEM((hps, seq, hd), F32),
                        pltpu.VMEM((hps, 2, hd, hd), F32), pltpu.VMEM((hps, 2, 1, hd), F32)],
        compiler_params=_cparams(2),
        name="mlstm",
    )(*args)
    if not emit_state:
        return outs[0], None
    hb, cst, nst, mst = outs
    mstate = mst[:, :, 0:2, 0].transpose(0, 2, 1)
    return hb, (cst, nst, mstate)


def _outproj_kernel(*refs, n_branch, final):
    it = iter(refs)
    branches = [(next(it), next(it), next(it)) for _ in range(n_branch)]
    w_ref, x_ref, mod_ref = next(it), next(it), next(it)
    fn_ref = next(it) if final else None
    y_ref, wb_ref = next(it), next(it)

    @pl.when(pl.program_id(0) == 0)
    def _():
        wb_ref[...] = w_ref[0].astype(BF16)

    out = None
    for n, (a_ref, zlo_ref, zhi_ref) in enumerate(branches):
        base = n * 2 * Z_BLOCK
        for part, z_ref in enumerate((zlo_ref, zhi_ref)):
            cols = slice(part * Z_BLOCK, (part + 1) * Z_BLOCK)
            gated = a_ref[:, cols] * z_ref[...]
            w = wb_ref[base + part * Z_BLOCK:base + (part + 1) * Z_BLOCK, :]
            p = jnp.dot(gated, w, preferred_element_type=F32)
            out = p if out is None else out + p
    y = x_ref[...] + mod_ref[0, 0, 2:3, :] * out
    if final:
        y = _rms(y) * fn_ref[...]
    y_ref[...] = y


def _outproj(branches, w_out, layer, x, mods, depth_idx, rows_per_mod, final_norm):
    m = x.shape[0]
    tm = 1024
    if rows_per_mod is None:
        mod_map = lambda i: (depth_idx, 0, 0, 0)
    else:
        mod_map = lambda i: (depth_idx, 1 + (i * tm) // rows_per_mod, 0, 0)
    in_specs, args = [], []
    for a, z, z_off in branches:
        zb = z_off // Z_BLOCK
        in_specs += [pl.BlockSpec((tm, 2 * Z_BLOCK), lambda i: (i, 0)),
                     pl.BlockSpec((tm, Z_BLOCK), lambda i, zb=zb: (i, zb)),
                     pl.BlockSpec((tm, Z_BLOCK), lambda i, zb=zb: (i, zb + 1))]
        args += [a, z, z]
    wk = w_out.shape[1]
    in_specs += [pl.BlockSpec((1, wk, D_MODEL), lambda i: (layer, 0, 0), pipeline_mode=pl.Buffered(1)),
                 pl.BlockSpec((tm, D_MODEL), lambda i: (i, 0)),
                 pl.BlockSpec((1, 1, 3, D_MODEL), mod_map)]
    args += [w_out, x, mods]
    final = final_norm is not None
    if final:
        in_specs.append(pl.BlockSpec((1, D_MODEL), lambda i: (0, 0)))
        args.append(final_norm.reshape(1, D_MODEL))
    return pl.pallas_call(
        functools.partial(_outproj_kernel, n_branch=len(branches), final=final),
        grid=(m // tm,),
        in_specs=in_specs,
        out_specs=pl.BlockSpec((tm, D_MODEL), lambda i: (i, 0)),
        out_shape=jax.ShapeDtypeStruct((m, D_MODEL), F32),
        scratch_shapes=[pltpu.VMEM((wk, D_MODEL), BF16)],
        compiler_params=_cparams(1, VMEM_LIMIT_WIDE),
        name="outproj",
    )(*args)


def _mla_slabs(kv, kr_slab):
    lane = lax.broadcasted_iota(jnp.int32, kv.shape, 1)
    nope = lane < C_NOPE
    keys = jnp.where(nope, kv, kr_slab).astype(BF16)
    values = jnp.where(nope, 1.0, kv).astype(BF16)
    return keys, values


def _inproj_c_kernel(*refs, rope):
    it = iter(refs)
    x_ref, mod_ref, g_ref, w_ref, qn_ref, wqb_ref, kvn_ref, wkvb_ref = (next(it) for _ in range(8))
    cos_ref, sin_ref = (next(it), next(it)) if rope else (None, None)
    q_ref, ks_ref, vs_ref, z_ref, ckv_ref, kr_ref, wb_ref = (next(it) for _ in range(7))

    @pl.when(pl.program_id(0) == 0)
    def _():
        wb_ref[...] = w_ref[0].astype(BF16)

    hn = _norm_modulate(x_ref[...], g_ref[0], mod_ref[0, 0]).astype(BF16)

    def project(lo, hi):
        return lax.dot_general(hn, wb_ref[lo:hi, :], _NT, preferred_element_type=F32)

    qa = _rms(project(CIN_QA, CIN_KVA)) * qn_ref[...]
    ckv = _rms(project(CIN_KVA, CIN_KR)) * kvn_ref[...]
    z_ref[...] = _silu(project(CIN_Z, CIN_WIDTH)).astype(BF16)
    tm = hn.shape[0]
    kr = jnp.concatenate([jnp.zeros((tm, C_NOPE), F32), project(CIN_KR, CIN_Z),
                          jnp.zeros((tm, C_HEAD_PAD - C_NOPE - C_ROPE), F32)], axis=1)
    ckv_ref[...] = ckv
    kr_ref[...] = kr
    kv = jnp.dot(ckv.astype(BF16), wkvb_ref[...], preferred_element_type=F32)
    scale = (C_NOPE + C_ROPE) ** -0.5 * LOG2E
    q = jnp.dot(qa.astype(BF16), wqb_ref[...], preferred_element_type=F32) * scale
    half = C_ROPE // 4
    if rope:
        cos = cos_ref[...]
        sin = sin_ref[...]
        kr = _rope(kr, cos, sin, half)
        for hd in range(C_HEADS):
            hs = slice(hd * C_HEAD_PAD, (hd + 1) * C_HEAD_PAD)
            q_ref[:, hs] = _rope(q[:, hs], cos, sin, half).astype(BF16)
    else:
        q_ref[...] = q.astype(BF16)
    for hd in range(C_HEADS):
        hs = slice(hd * C_HEAD_PAD, (hd + 1) * C_HEAD_PAD)
        ks_ref[:, hs], vs_ref[:, hs] = _mla_slabs(kv[:, hs], kr)


def _inproj_c(x, mods, norm_g, depth_idx, w_t, layer, q_norm, w_qb, kv_norm, w_kvb, rows_per_mod, tables):
    m = x.shape[0]
    tm = 512
    rope = tables is not None
    if rows_per_mod is None:
        mod_map = lambda i: (depth_idx, 0, 0, 0)
    else:
        mod_map = lambda i: (depth_idx, 1 + (i * tm) // rows_per_mod, 0, 0)
    const = lambda i: (0, 0)
    qw = C_HEADS * C_HEAD_PAD
    kvw = C_HEADS * (C_NOPE + C_VDIM)
    in_specs = [pl.BlockSpec((tm, D_MODEL), lambda i: (i, 0)),
                pl.BlockSpec((1, 1, 3, D_MODEL), mod_map),
                pl.BlockSpec((1, 1, D_MODEL), lambda i: (depth_idx, 0, 0)),
                pl.BlockSpec((1, CIN_WIDTH, D_MODEL), lambda i: (layer, 0, 0), pipeline_mode=pl.Buffered(1)),
                pl.BlockSpec((1, C_Q_RANK), const),
                pl.BlockSpec((C_Q_RANK, qw), const),
                pl.BlockSpec((1, C_KV_RANK), const),
                pl.BlockSpec((C_KV_RANK, kvw), const)]
    args = [x, mods, norm_g, w_t, q_norm, w_qb, kv_norm, w_kvb]
    if rope:
        tpb = rows_per_mod // tm
        in_specs += [pl.BlockSpec((tm, LANES), lambda i: (i % tpb, 0))] * 2
        args += list(tables)
    return pl.pallas_call(
        functools.partial(_inproj_c_kernel, rope=rope),
        grid=(m // tm,),
        in_specs=in_specs,
        out_specs=[pl.BlockSpec((tm, qw), lambda i: (i, 0)),
                   pl.BlockSpec((tm, kvw), lambda i: (i, 0)),
                   pl.BlockSpec((tm, kvw), lambda i: (i, 0)),
                   pl.BlockSpec((tm, C_WIDTH), lambda i: (i, 0)),
                   pl.BlockSpec((tm, C_KV_RANK), lambda i: (i, 0)),
                   pl.BlockSpec((tm, LANES), lambda i: (i, 0))],
        out_shape=[jax.ShapeDtypeStruct((m, qw), BF16),
                   jax.ShapeDtypeStruct((m, kvw), BF16),
                   jax.ShapeDtypeStruct((m, kvw), BF16),
                   jax.ShapeDtypeStruct((m, C_WIDTH), BF16),
                   jax.ShapeDtypeStruct((m, C_KV_RANK), F32),
                   jax.ShapeDtypeStruct((m, LANES), F32)],
        scratch_shapes=[pltpu.VMEM((CIN_WIDTH, D_MODEL), BF16)],
        compiler_params=_cparams(1),
        name="inproj_c",
    )(*args)


def _matmul_kernel(x_ref, w_ref, o_ref):
    o_ref[...] = jnp.dot(x_ref[...], w_ref[...], preferred_element_type=F32).astype(o_ref.dtype)


def _matmul(x, w, tm):
    m, k = x.shape
    n = w.shape[1]
    return pl.pallas_call(
        _matmul_kernel,
        grid=(m // tm,),
        in_specs=[pl.BlockSpec((tm, k), lambda i: (i, 0)),
                  pl.BlockSpec((k, n), lambda i: (0, 0))],
        out_specs=pl.BlockSpec((tm, n), lambda i: (i, 0)),
        out_shape=jax.ShapeDtypeStruct((m, n), BF16),
        compiler_params=_cparams(1),
        name="matmul",
    )(x, w)


def _attn_c_kernel(*refs, has_ctx):
    it = iter(refs)
    q_ref, kown_ref, vown_ref = next(it), next(it), next(it)
    kvc_ref, krc_ref = (next(it), next(it)) if has_ctx else (None, None)
    o_ref = next(it)
    kctx_ref, vctx_ref = (next(it), next(it)) if has_ctx else (None, None)

    if has_ctx:
        @pl.when(pl.program_id(1) == 0)
        def _():
            for h in range(C_HEADS):
                hs = slice(h * C_HEAD_PAD, (h + 1) * C_HEAD_PAD)
                kctx_ref[:, hs], vctx_ref[:, hs] = _mla_slabs(kvc_ref[:, hs], krc_ref[...])

    for h in range(C_HEADS):
        hs = slice(h * C_HEAD_PAD, (h + 1) * C_HEAD_PAD)
        qh = q_ref[:, hs]
        scores = [lax.dot_general(qh, kown_ref[:, hs], _NT, preferred_element_type=F32)]
        values = [vown_ref[:, hs]]
        if has_ctx:
            scores.append(lax.dot_general(qh, kctx_ref[:, hs], _NT, preferred_element_type=F32))
            values.append(vctx_ref[:, hs])
        o = _softmax_pv(scores, values, None)
        o_ref[:, h * C_VDIM:(h + 1) * C_VDIM] = o[:, C_NOPE:].astype(o_ref.dtype)


def _attn_c_dense_kernel(q_ref, ks_ref, vs_ref, o_ref, s_ref, e_ref):
    for h in range(C_HEADS):
        hs = slice(h * C_HEAD_PAD, (h + 1) * C_HEAD_PAD)
        s_ref[h] = lax.dot_general(q_ref[:, hs], ks_ref[:, hs], _NT, preferred_element_type=F32)
    s = s_ref[...]
    e_ref[...] = jnp.exp2(s - jnp.max(s, axis=-1, keepdims=True)).astype(BF16)
    for h in range(C_HEADS):
        hs = slice(h * C_HEAD_PAD, (h + 1) * C_HEAD_PAD)
        res = jnp.dot(e_ref[h], vs_ref[:, hs], preferred_element_type=F32)
        o = res / pltpu.roll(res, LANES // 2, 1)
        o_ref[:, h * C_VDIM:(h + 1) * C_VDIM] = o[:, C_NOPE:].astype(o_ref.dtype)


def _attn_c_dense(q, ks, vs, batch, seq):
    w = C_HEADS * C_HEAD_PAD
    return pl.pallas_call(
        _attn_c_dense_kernel,
        grid=(batch,),
        in_specs=[pl.BlockSpec((seq, w), lambda b: (b, 0))] * 3,
        out_specs=pl.BlockSpec((seq, C_WIDTH), lambda b: (b, 0)),
        out_shape=jax.ShapeDtypeStruct((batch * seq, C_WIDTH), BF16),
        scratch_shapes=[pltpu.VMEM((C_HEADS, seq, seq), F32), pltpu.VMEM((C_HEADS, seq, seq), BF16)],
        compiler_params=_cparams(1),
        name="attn_c_dense",
    )(q, ks, vs)


def _attn_c(q, ks, vs, ctx, batch, seq, tq):
    nq = seq // tq
    w = C_HEADS * C_HEAD_PAD
    has_ctx = ctx is not None
    in_specs = [pl.BlockSpec((tq, w), lambda b, i: (b * nq + i, 0)),
                pl.BlockSpec((seq, w), lambda b, i: (b, 0)),
                pl.BlockSpec((seq, w), lambda b, i: (b, 0))]
    args = [q, ks, vs]
    scratch = []
    if has_ctx:
        kv_ctx, kr_ctx = ctx
        nctx = kv_ctx.shape[0] // batch
        in_specs += [pl.BlockSpec((nctx, w), lambda b, i: (b, 0)),
                     pl.BlockSpec((nctx, LANES), lambda b, i: (b, 0))]
        args += [kv_ctx, kr_ctx]
        scratch += [pltpu.VMEM((nctx, w), BF16), pltpu.VMEM((nctx, w), BF16)]
    return pl.pallas_call(
        functools.partial(_attn_c_kernel, has_ctx=has_ctx),
        grid=(batch, nq),
        in_specs=in_specs,
        out_specs=pl.BlockSpec((tq, C_WIDTH), lambda b, i: (b * nq + i, 0)),
        out_shape=jax.ShapeDtypeStruct((batch * seq, C_WIDTH), BF16),
        scratch_shapes=scratch,
        compiler_params=_cparams(2),
        name="attn_c",
    )(*args)


def _rope_tables(n_tokens):
    pos_r = np.repeat(np.arange(n_tokens // GRID_W), GRID_W).astype(np.float64)
    pos_c = np.tile(np.arange(GRID_W), n_tokens // GRID_W).astype(np.float64)

    def seg(d_axis):
        half = d_axis // 2
        freqs = np.power(ROPE_BASE, -np.arange(half, dtype=np.float64) / half)
        cos, sin = [], []
        for pos in (pos_r, pos_c):
            ang = pos[:, None] * freqs[None, :]
            cos += [np.cos(ang), np.cos(ang)]
            sin += [-np.sin(ang), np.sin(ang)]
        return np.concatenate(cos, axis=1), np.concatenate(sin, axis=1)

    cos_a, sin_a = seg(A_HEAD_DIM // 2)
    cos_a, sin_a = np.tile(cos_a, (1, 2)), np.tile(sin_a, (1, 2))
    cos_r, sin_r = seg(C_ROPE // 2)
    ones = np.ones((n_tokens, C_NOPE))
    pad = C_HEAD_PAD - C_NOPE - C_ROPE
    cos_c = np.concatenate([ones, cos_r, np.ones((n_tokens, pad))], axis=1)
    sin_c = np.concatenate([0 * ones, sin_r, np.zeros((n_tokens, pad))], axis=1)
    f = lambda a: jnp.asarray(a, dtype=F32)
    return (f(cos_a), f(sin_a)), (f(cos_c), f(sin_c))


def _pad_cols(w, left, total):
    return jnp.pad(w, ((0, 0), (left, total - left - w.shape[1])))


def kernel(x_prompt, x_sample, cache_a_k, cache_a_v, state_b_mem, state_b_norm, state_b_max, cache_c_kv, cache_c_krope, c, c_ctx, norm_g, w_mod, b_mod, w_in_ab, sink_a, conv_b, gate_bias_b, norm_b, w_out_ab, w_in_c, q_norm_c, w_qb_c, kv_norm_c, w_kvb_c, w_out_c, final_norm):
    bp, tp, _ = x_prompt.shape
    bs, ts, _ = x_sample.shape
    past = cache_a_k.shape[2]
    tables_a, tables_c = _rope_tables(ts)
    w_ab_t = jnp.swapaxes(w_in_ab, 1, 2)
    w_c_t = jnp.swapaxes(w_in_c, 1, 2)

    cond = jnp.zeros((16, D_MODEL), F32).at[0].set(c_ctx).at[1:1 + bs].set(c)
    mods = _adaln(cond, w_mod, b_mod).reshape(DEPTH, 16, 3, D_MODEL)
    gains = norm_g.reshape(DEPTH, 1, D_MODEL)

    yp = x_prompt.reshape(bp * tp, D_MODEL)
    ys = x_sample.reshape(bs * ts, D_MODEL)
    a_k, a_v, b_max, c_kvs, c_krs = [], [], [], [], []
    states = None
    for l in range(DEPTH):
        j = l // 2
        fin = final_norm if l == DEPTH - 1 else None
        if l % 2 == 0:
            w_bf = _cast_rows(w_ab_t, j, MAIN_WIDTH)
            main_p, k_p, v_p, gates_p = _inproj_ab(yp, mods, gains, l, w_bf, w_ab_t, j, tp, True, True, None)
            main_s, gates_s = _inproj_ab(ys, mods, gains, l, w_bf, w_ab_t, j, ts, False, False, tables_a)
            a_k.append(k_p)
            a_v.append(v_p)

            attn_p = _attn_a_prompt(main_p, sink_a[j], bp, tp)
            ck = cache_a_k[:, j].reshape(bs, past, A_KV_WIDTH).astype(BF16)
            cv = cache_a_v[:, j].reshape(bs, past, A_KV_WIDTH).astype(BF16)
            attn_s = _attn_a_sample(main_s, sink_a[j], ck, cv, bs, ts)

            prev = None if states is None else states[:2]
            hb_p, states = _mlstm(main_p, gates_p, conv_b, gate_bias_b, norm_b, None, j, bp, tp, True, prev, B_HEADS)
            init = (state_b_mem, state_b_norm, state_b_max)
            hb_s, _ = _mlstm(main_s, gates_s, conv_b, gate_bias_b, norm_b, init, j, bs, ts, False, None, 1)
            b_max.append(states[2])

            yp = _outproj([(attn_p, main_p, MAIN_ZA), (hb_p, main_p, MAIN_ZB)], w_out_ab, j, yp, mods, l, None, fin)
            ys = _outproj([(attn_s, main_s, MAIN_ZA), (hb_s, main_s, MAIN_ZB)], w_out_ab, j, ys, mods, l, ts, fin)
        else:
            wq = w_qb_c[j].reshape(C_Q_RANK, C_HEADS, C_NOPE + C_ROPE)
            wq = jnp.pad(wq, ((0, 0), (0, 0), (0, C_HEAD_PAD - C_NOPE - C_ROPE)))
            wq = wq.reshape(C_Q_RANK, C_HEADS * C_HEAD_PAD).astype(BF16)
            wkv = w_kvb_c[j].astype(BF16)
            qn = q_norm_c[j].reshape(1, C_Q_RANK)
            kvn = kv_norm_c[j].reshape(1, C_KV_RANK)

            q_p, ks_p, vs_p, z_p, ckv_p, kr_p = _inproj_c(yp, mods, gains, l, w_c_t, j, qn, wq, kvn, wkv, None, None)
            q_s, ks_s, vs_s, z_s, _, _ = _inproj_c(ys, mods, gains, l, w_c_t, j, qn, wq, kvn, wkv, ts, tables_c)
            c_kvs.append(ckv_p.reshape(bp, tp, C_KV_RANK))
            c_krs.append(kr_p[:, C_NOPE:C_NOPE + C_ROPE].reshape(bp, tp, C_ROPE))

            cc = cache_c_kv[:, j].reshape(bs * past, C_KV_RANK).astype(BF16)
            kv_ctx = _matmul(cc, wkv, 512)
            kr_ctx = _pad_cols(cache_c_krope[:, j].reshape(bs * past, C_ROPE), C_NOPE, C_HEAD_PAD).astype(BF16)

            attn_p = _attn_c_dense(q_p, ks_p, vs_p, bp, tp)
            attn_s = _attn_c(q_s, ks_s, vs_s, (kv_ctx, kr_ctx), bs, ts, 512)

            yp = _outproj([(attn_p, z_p, 0)], w_out_c, j, yp, mods, l, None, fin)
            ys = _outproj([(attn_s, z_s, 0)], w_out_c, j, ys, mods, l, ts, fin)

    def cache_layout(per_layer):
        stacked = jnp.stack(per_layer, axis=1).reshape(bp, N_EVEN, A_KV_HEADS, A_HEAD_DIM, tp)
        return stacked.transpose(0, 1, 4, 2, 3)

    b_mem = states[0]
    b_nrm = states[1].reshape(bp, N_EVEN, 2, B_HEADS, B_HEAD_DIM)
    return (yp.reshape(bp, tp, D_MODEL), ys.reshape(bs, ts, D_MODEL),
            cache_layout(a_k), cache_layout(a_v), b_mem, b_nrm,
            jnp.stack(b_max, axis=1), jnp.stack(c_kvs, axis=1), jnp.stack(c_krs, axis=1))
```

```python
import functools
import math

import numpy as np
import jax
import jax.numpy as jnp
from jax import lax
from jax.experimental import pallas as pl
from jax.experimental.pallas import tpu as pltpu

F32 = jnp.float32
BF16 = jnp.bfloat16

D_MODEL = 1024
DEPTH = 4
N_EVEN = 2
EPS = 1e-6
ROPE_BASE = 10000.0
NEG_INF = -1e30
GRID_W = 64
LOG2E = math.log2(math.e)
A_HEADS = 16
A_KV_HEADS = 4
A_GROUP = A_HEADS // A_KV_HEADS
A_HEAD_DIM = 64
A_WIDTH = A_HEADS * A_HEAD_DIM
A_KV_WIDTH = A_KV_HEADS * A_HEAD_DIM
WINDOW = 128
BLOCK = 128
Q_BLOCKS = 4
B_HEADS = 4
B_HEAD_DIM = 256
B_WIDTH = B_HEADS * B_HEAD_DIM
B_CHUNK = 256
C_HEADS = 16
C_NOPE = 64
C_ROPE = 32
C_VDIM = 64
C_Q_RANK = 384
C_KV_RANK = 256
C_WIDTH = C_HEADS * C_VDIM
C_HEAD_PAD = 128

LANES = 128
SUBLANES = 8
B_CONV = 3
MAIN_QA, MAIN_KA, MAIN_VA, MAIN_ZA, MAIN_QB, MAIN_KB, MAIN_VB, MAIN_OB, MAIN_ZB = (
    0, 1024, 1280, 1536, 2560, 3584, 4608, 5632, 6656)
MAIN_WIDTH = 7680
CIN_QA, CIN_KVA, CIN_KR, CIN_Z, CIN_WIDTH = 0, 384, 640, 672, 1696
Z_BLOCK = 512
QK_TILE = 1280
A_QSCALE = A_HEAD_DIM ** -0.5 * LOG2E

MIB = 1024 * 1024
VMEM_LIMIT = 48 * MIB
VMEM_LIMIT_WIDE = 56 * MIB

_NT = (((1,), (1,)), ((), ()))
_TN = (((0,), (0,)), ((), ()))


def _cparams(n_axes, vmem=VMEM_LIMIT):
    return pltpu.CompilerParams(dimension_semantics=("arbitrary",) * n_axes,
                                vmem_limit_bytes=vmem)


def _silu(x):
    return x * jax.nn.sigmoid(x)


def _log_sigmoid(x):
    return jnp.minimum(x, 0.0) - jnp.log1p(jnp.exp(-jnp.abs(x)))


def _rms(x):
    return x * lax.rsqrt(jnp.mean(x * x, axis=-1, keepdims=True) + EPS)


def _norm_modulate(x, g, mod):
    y = _rms(x) * g
    return y * (1.0 + mod[1:2, :]) + mod[0:1, :]


def _swap_halves(x, half):
    lane = lax.broadcasted_iota(jnp.int32, x.shape, 1)
    first = (lane % (2 * half)) < half
    return jnp.where(first, pltpu.roll(x, LANES - half, 1), pltpu.roll(x, half, 1))


def _rope(x, cos, sin, half):
    return x * cos + _swap_halves(x, half) * sin


def _softmax_pv(scores, values, sink):
    tiles = [s[:, t * LANES:(t + 1) * LANES] for s in scores for t in range(s.shape[1] // LANES)]
    m = jnp.max(functools.reduce(jnp.maximum, tiles), axis=-1, keepdims=True)
    if sink is not None:
        m = jnp.maximum(m, sink)
    res = functools.reduce(jnp.add, [
        jnp.dot(jnp.exp2(s - m).astype(BF16), v, preferred_element_type=F32)
        for s, v in zip(scores, values)])
    den = pltpu.roll(res, LANES // 2, 1)
    if sink is not None:
        den = den + jnp.exp2(sink - m)
    return res / den


def _adaln_kernel(c_ref, w_ref, b_ref, o_ref):
    a = _silu(c_ref[...]).astype(BF16)
    w = w_ref[0].astype(BF16)
    o_ref[0] = jnp.dot(a, w, preferred_element_type=F32) + b_ref[0]


def _adaln(cond, w_mod, b_mod):
    tn = 1024
    n = 3 * D_MODEL
    return pl.pallas_call(
        _adaln_kernel,
        grid=(DEPTH, n // tn),
        in_specs=[pl.BlockSpec((16, D_MODEL), lambda l, j: (0, 0)),
                  pl.BlockSpec((1, D_MODEL, tn), lambda l, j: (l, 0, j)),
                  pl.BlockSpec((1, 1, tn), lambda l, j: (l, 0, j))],
        out_specs=pl.BlockSpec((1, 16, tn), lambda l, j: (l, 0, j)),
        out_shape=jax.ShapeDtypeStruct((DEPTH, 16, n), F32),
        compiler_params=_cparams(2),
        name="adaln",
    )(cond, w_mod, b_mod.reshape(DEPTH, 1, n))


def _cast_kernel(w_ref, o_ref):
    o_ref[...] = w_ref[0].astype(o_ref.dtype)


def _cast_rows(w_t, layer, rows):
    tr = QK_TILE
    d = w_t.shape[2]
    return pl.pallas_call(
        _cast_kernel,
        grid=(rows // tr,),
        in_specs=[pl.BlockSpec((1, tr, d), lambda i: (layer, i, 0))],
        out_specs=pl.BlockSpec((tr, d), lambda i: (i, 0)),
        out_shape=jax.ShapeDtypeStruct((rows, d), BF16),
        compiler_params=_cparams(1),
        name="cast_rows",
    )(w_t)


def _main_chunk_kind(col):
    bounds = ((MAIN_KA, "q"), (MAIN_VA, "k"), (MAIN_ZA, "plain"), (MAIN_QB, "silu"), (MAIN_OB, "plain"),
              (MAIN_ZB, "sigmoid"), (MAIN_WIDTH, "silu"))
    return next(kind for end, kind in bounds if col < end)


def _inproj_ab_kernel(*refs, emit_kv, rope, sub):
    it = iter(refs)
    x_ref, mod_ref, g_ref, wb_ref, wg_ref = (next(it) for _ in range(5))
    cos_ref, sin_ref = (next(it), next(it)) if rope else (None, None)
    main_ref = next(it)
    k_ref, v_ref = (next(it), next(it)) if emit_kv else (None, None)
    gate_ref = next(it)

    mod = mod_ref[0, 0]
    g = g_ref[0]
    for r in range(x_ref.shape[0] // sub):
        rows = slice(r * sub, (r + 1) * sub)
        hn = _norm_modulate(x_ref[rows, :], g, mod).astype(BF16)
        gate_ref[:, rows] = lax.dot_general(wg_ref[0].astype(BF16), hn, _NT, preferred_element_type=F32)
        if emit_kv:
            kv_t = lax.dot_general(wb_ref[MAIN_KA:MAIN_ZA, :], hn, _NT, preferred_element_type=F32)
            k_ref[r] = kv_t[:A_KV_WIDTH, :]
            v_ref[r] = kv_t[A_KV_WIDTH:, :]
        for t in range(MAIN_WIDTH // QK_TILE):
            res = lax.dot_general(hn, wb_ref[t * QK_TILE:(t + 1) * QK_TILE, :], _NT, preferred_element_type=F32)
            for c in range(QK_TILE // LANES):
                col = t * QK_TILE + c * LANES
                kind = _main_chunk_kind(col)
                chunk = res[:, c * LANES:(c + 1) * LANES]
                if kind in ("q", "k"):
                    if rope:
                        chunk = _rope(chunk, cos_ref[rows, :], sin_ref[rows, :], A_HEAD_DIM // 4)
                    if kind == "q":
                        chunk = chunk * A_QSCALE
                elif kind == "silu":
                    chunk = _silu(chunk)
                elif kind == "sigmoid":
                    chunk = jax.nn.sigmoid(chunk)
                main_ref[rows, col:col + LANES] = chunk.astype(BF16)


def _inproj_ab(x, mods, norm_g, depth_idx, w_bf, w_t, layer, seq, shared_mod, emit_kv, tables):
    m = x.shape[0]
    tm = 512
    rope = tables is not None
    if shared_mod:
        mod_map = lambda i: (depth_idx, 0, 0, 0)
    else:
        mod_map = lambda i: (depth_idx, 1 + (i * tm) // seq, 0, 0)
    n_gate = 4 * B_HEADS
    tiles_per_table = max(seq, tm) // tm
    table_spec = pl.BlockSpec((tm, LANES), lambda i: (i % tiles_per_table, 0))
    in_specs = [pl.BlockSpec((tm, D_MODEL), lambda i: (i, 0)),
                pl.BlockSpec((1, 1, 3, D_MODEL), mod_map),
                pl.BlockSpec((1, 1, D_MODEL), lambda i: (depth_idx, 0, 0)),
                pl.BlockSpec((MAIN_WIDTH, D_MODEL), lambda i: (0, 0), pipeline_mode=pl.Buffered(1)),
                pl.BlockSpec((1, n_gate, D_MODEL), lambda i: (layer, MAIN_WIDTH // n_gate, 0))]
    args = [x, mods, norm_g, w_bf, w_t]
    if rope:
        in_specs += [table_spec] * 2
        args += list(tables)
    out_specs = [pl.BlockSpec((tm, MAIN_WIDTH), lambda i: (i, 0))]
    out_shape = [jax.ShapeDtypeStruct((m, MAIN_WIDTH), BF16)]
    if emit_kv:
        out_specs += [pl.BlockSpec((tm // seq, A_KV_WIDTH, seq), lambda i: (i, 0, 0))] * 2
        out_shape += [jax.ShapeDtypeStruct((m // seq, A_KV_WIDTH, seq), F32)] * 2
    out_specs.append(pl.BlockSpec((n_gate, tm), lambda i: (0, i)))
    out_shape.append(jax.ShapeDtypeStruct((n_gate, m), F32))
    return pl.pallas_call(
        functools.partial(_inproj_ab_kernel, emit_kv=emit_kv, rope=rope, sub=seq if emit_kv else tm // 2),
        grid=(m // tm,),
        in_specs=in_specs,
        out_specs=out_specs,
        out_shape=out_shape,
        compiler_params=_cparams(1, VMEM_LIMIT_WIDE),
        name="inproj_ab",
    )(*args)


def _value_slabs(v_ref, slab_ref):
    keys = v_ref.shape[0]
    ones = jnp.ones((keys, LANES - A_HEAD_DIM), BF16)
    for g in range(A_KV_HEADS):
        slab_ref[:, g * LANES:g * LANES + A_HEAD_DIM] = v_ref[:, g * A_HEAD_DIM:(g + 1) * A_HEAD_DIM]
        slab_ref[:, g * LANES + A_HEAD_DIM:(g + 1) * LANES] = ones


def _group_queries(q_ref, g):
    return jnp.concatenate([q_ref[:, (g * A_GROUP + hh) * A_HEAD_DIM:(g * A_GROUP + hh + 1) * A_HEAD_DIM]
                            for hh in range(A_GROUP)], axis=0)


def _sink_softmax(s_ref, e_ref, t_ref, sink_ref):
    s = s_ref[...]
    sink = sink_ref[...] * LOG2E
    m = jnp.maximum(jnp.max(s, axis=-1, keepdims=True), sink)
    e_ref[...] = jnp.exp2(s - m).astype(BF16)
    t_ref[...] = jnp.exp2(sink - m)


def _attn_a_prompt_kernel(sink_ref, q_ref, k_ref, v_ref, o_ref, s_ref, e_ref, t_ref, vs_ref):
    _value_slabs(v_ref, vs_ref)
    seq = q_ref.shape[0]
    rows = A_GROUP * seq
    for g in range(A_KV_HEADS):
        heads = slice(g * A_GROUP, (g + 1) * A_GROUP)
        ks = slice(g * A_HEAD_DIM, (g + 1) * A_HEAD_DIM)
        s = lax.dot_general(_group_queries(q_ref, g), k_ref[:, ks], _NT, preferred_element_type=F32)
        s_ref[heads] = s.reshape(A_GROUP, seq, seq)
    _sink_softmax(s_ref, e_ref, t_ref, sink_ref)
    for g in range(A_KV_HEADS):
        heads = slice(g * A_GROUP, (g + 1) * A_GROUP)
        e = e_ref[heads].reshape(rows, seq)
        res = jnp.dot(e, vs_ref[:, g * LANES:(g + 1) * LANES], preferred_element_type=F32)
        den = jnp.dot(e, jnp.ones((seq, LANES), BF16), preferred_element_type=F32)
        o = res / (den + t_ref[heads].reshape(rows, 1))
        for hh in range(A_GROUP):
            h = g * A_GROUP + hh
            o_ref[:, h * A_HEAD_DIM:(h + 1) * A_HEAD_DIM] = (
                o[hh * seq:(hh + 1) * seq, :A_HEAD_DIM].astype(o_ref.dtype))


def _attn_a_prompt(main, sink, batch, seq):
    kb = MAIN_KA // A_KV_WIDTH
    vb = MAIN_VA // A_KV_WIDTH
    return pl.pallas_call(
        _attn_a_prompt_kernel,
        grid=(batch,),
        in_specs=[pl.BlockSpec((A_HEADS, 1, 1), lambda b: (0, 0, 0)),
                  pl.BlockSpec((seq, A_WIDTH), lambda b: (b, MAIN_QA // A_WIDTH)),
                  pl.BlockSpec((seq, A_KV_WIDTH), lambda b: (b, kb)),
                  pl.BlockSpec((seq, A_KV_WIDTH), lambda b: (b, vb))],
        out_specs=pl.BlockSpec((seq, A_WIDTH), lambda b: (b, 0)),
        out_shape=jax.ShapeDtypeStruct((batch * seq, A_WIDTH), BF16),
        scratch_shapes=[pltpu.VMEM((A_HEADS, seq, seq), F32),
                        pltpu.VMEM((A_HEADS, seq, seq), BF16),
                        pltpu.VMEM((A_HEADS, seq, 1), F32),
                        pltpu.VMEM((seq, A_KV_HEADS * LANES), BF16)],
        compiler_params=_cparams(1),
        name="attn_a_prompt",
    )(sink.reshape(A_HEADS, 1, 1), main, main, main)


def _attn_a_sample_kernel(sink_ref, q_ref, k_ref, v_ref, ck_ref, cv_ref, bias_ref, o_ref, vs_ref, cvs_ref):
    i = pl.program_id(1)
    seq = k_ref.shape[0]
    span = 3 * BLOCK

    @pl.when(i == 0)
    def _():
        _value_slabs(v_ref, vs_ref)
        _value_slabs(cv_ref.at[0], cvs_ref)

    ck = ck_ref[0]
    rows = A_GROUP * BLOCK
    head_of_row = lax.broadcasted_iota(jnp.int32, (rows, 1), 0) // BLOCK
    for blk in range(Q_BLOCKS):
        qi = i * Q_BLOCKS + blk
        qrows = slice(blk * BLOCK, (blk + 1) * BLOCK)
        start = pl.multiple_of(jnp.clip((qi - 1) * BLOCK, 0, seq - span), BLOCK)
        kw = k_ref[pl.ds(start, span), :]
        vw = vs_ref[pl.ds(start, span), :]
        bias = bias_ref[(qi * BLOCK - start) // BLOCK]
        bias = jnp.concatenate([bias] * A_GROUP, axis=0)
        for g in range(A_KV_HEADS):
            ks = slice(g * A_HEAD_DIM, (g + 1) * A_HEAD_DIM)
            gs = slice(g * LANES, (g + 1) * LANES)
            sink = jnp.zeros((rows, 1), F32)
            for hh in range(A_GROUP):
                sink = jnp.where(head_of_row == hh, sink_ref[g * A_GROUP + hh] * LOG2E, sink)
            qg = _group_queries(q_ref.at[qrows], g)
            s_loc = lax.dot_general(qg, kw[:, ks], _NT, preferred_element_type=F32) + bias
            s_ctx = lax.dot_general(qg, ck[:, ks], _NT, preferred_element_type=F32)
            o = _softmax_pv([s_loc, s_ctx], [vw[:, gs], cvs_ref[:, gs]], sink)
            for hh in range(A_GROUP):
                h = g * A_GROUP + hh
                o_ref[qrows, h * A_HEAD_DIM:(h + 1) * A_HEAD_DIM] = (
                    o[hh * BLOCK:(hh + 1) * BLOCK, :A_HEAD_DIM].astype(o_ref.dtype))


def _window_bias():
    r = np.arange(BLOCK)[:, None]
    c = np.arange(3 * BLOCK)[None, :]
    masks = [np.where(np.abs(off + r - c) <= WINDOW, 0.0, NEG_INF) for off in (0, BLOCK, 2 * BLOCK)]
    return jnp.asarray(np.stack(masks), dtype=F32)


def _attn_a_sample(main, sink, ck, cv, batch, seq):
    tq = BLOCK * Q_BLOCKS
    nb = seq // tq
    kb = MAIN_KA // A_KV_WIDTH
    vb = MAIN_VA // A_KV_WIDTH
    ctx = ck.shape[1]
    slab_w = A_KV_HEADS * LANES
    return pl.pallas_call(
        _attn_a_sample_kernel,
        grid=(batch, nb),
        in_specs=[pl.BlockSpec(memory_space=pltpu.SMEM),
                  pl.BlockSpec((tq, A_WIDTH), lambda b, i: (b * nb + i, MAIN_QA // A_WIDTH)),
                  pl.BlockSpec((seq, A_KV_WIDTH), lambda b, i: (b, kb)),
                  pl.BlockSpec((seq, A_KV_WIDTH), lambda b, i: (b, vb)),
                  pl.BlockSpec((1, ctx, A_KV_WIDTH), lambda b, i: (b, 0, 0)),
                  pl.BlockSpec((1, ctx, A_KV_WIDTH), lambda b, i: (b, 0, 0)),
                  pl.BlockSpec((3, BLOCK, 3 * BLOCK), lambda b, i: (0, 0, 0))],
        out_specs=pl.BlockSpec((tq, A_WIDTH), lambda b, i: (b * nb + i, 0)),
        out_shape=jax.ShapeDtypeStruct((batch * seq, A_WIDTH), BF16),
        scratch_shapes=[pltpu.VMEM((seq, slab_w), BF16),
                        pltpu.VMEM((ctx, slab_w), BF16)],
        compiler_params=_cparams(2),
        name="attn_a_sample",
    )(sink, main, main, main, ck, cv, _window_bias())


def _mlstm_kernel(*refs, seq, chunk, layer, hps, has_init, emit_state, has_prev):
    it = iter(refs)
    bias_ref = next(it)
    m0_ref = next(it) if has_init else None
    take = lambda: [next(it) for _ in range(hps)]
    q_refs, k_refs, v_refs, o_refs = take(), take(), take(), take()
    g_ref = next(it)
    cwq_refs, cwk_refs = take(), take()
    nw_ref = next(it)
    c0_ref, n0_ref = (next(it), next(it)) if has_init else (None, None)
    cprev_ref, nprev_ref = (next(it), next(it)) if has_prev else (None, None)
    h_ref = next(it)
    cst_out, nst_out, mst_out = (next(it), next(it), next(it)) if emit_state else (None, None, None)
    qs_ref, ks_ref, hf_ref, hb_ref, cst_ref, nst_ref = (next(it) for _ in range(6))

    b = pl.program_id(0)
    head0 = pl.program_id(1) * hps
    nc = seq // chunk
    hd = B_HEAD_DIM

    edge = lax.broadcasted_iota(jnp.int32, (SUBLANES, 1), 0)
    drop_first = (edge != 0).astype(F32)
    drop_last = (edge != SUBLANES - 1).astype(F32)
    rr = lax.broadcasted_iota(jnp.int32, (chunk, chunk), 0)
    cc = lax.broadcasted_iota(jnp.int32, (chunk, chunk), 1)
    diag = rr == cc

    def conv_silu(x_ref, w_ref, scale):
        x = x_ref[...].astype(F32)
        w = w_ref[0]
        x_up = jnp.concatenate([x[:seq - SUBLANES], x[seq - SUBLANES:] * drop_last], axis=0)
        x_dn = jnp.concatenate([x[:SUBLANES] * drop_first, x[SUBLANES:]], axis=0)
        y = pltpu.roll(x_up, 1, 0) * w[0:1] + x * w[1:2] + pltpu.roll(x_dn, seq - 1, 0) * w[2:3]
        return (_silu(y) * scale).astype(BF16)

    for hh in range(hps):
        h = head0 + hh
        v_ref = v_refs[hh]
        qs_ref[hh] = conv_silu(q_refs[hh], cwq_refs[hh], 1.0)
        ks_ref[hh] = conv_silu(k_refs[hh], cwk_refs[hh], hd ** -0.5)

        def gate_row(kind):
            ch = kind * B_HEADS + h
            return g_ref[pl.ds(ch, 1), :] + bias_ref[layer, ch]

        li = [gate_row(2 * d) for d in range(2)]
        lf = [_log_sigmoid(gate_row(2 * d + 1)) for d in range(2)]

        def chunk_step(c, d, m_prev, first):
            rows = pl.ds(c * chunk, chunk)
            lanes = slice(c * chunk, (c + 1) * chunk)
            qc = qs_ref[hh, rows, :]
            kc = ks_ref[hh, rows, :]
            vc = v_ref[rows, :]
            li_row = li[d][:, lanes]
            lf_row = lf[d][:, lanes]
            causal = (cc <= rr) if d == 0 else (cc >= rr)
            b_col = jnp.sum(jnp.where(causal, lf_row, 0.0), axis=1, keepdims=True)
            b_row = jnp.sum(jnp.where(diag, b_col, 0.0), axis=0, keepdims=True)
            a_row = li_row - b_row
            a_col = jnp.sum(jnp.where(diag, a_row, 0.0), axis=1, keepdims=True)
            total = jnp.sum(lf_row, axis=1, keepdims=True)
            log_d = jnp.where(causal, b_col + a_row, NEG_INF)
            log_init = b_col + m_prev
            m_t = jnp.maximum(log_init, jnp.max(log_d, axis=1, keepdims=True))
            d_mat = jnp.exp(log_d - m_t)
            s = lax.dot_general(qc, kc, _NT, preferred_element_type=F32) * d_mat
            num = jnp.dot(s.astype(BF16), vc, preferred_element_type=F32)
            den = jnp.sum(s, axis=1, keepdims=True)
            if not first:
                w_init = jnp.exp(log_init - m_t)
                num = num + w_init * jnp.dot(qc, cst_ref[hh, d].astype(BF16), preferred_element_type=F32)
                n_rows = jnp.broadcast_to(nst_ref[hh, d], (SUBLANES, hd)).astype(BF16)
                den = den + w_init * lax.dot_general(qc, n_rows, _NT, preferred_element_type=F32)[:, 0:1]
            hc = num / jnp.maximum(jnp.abs(den), jnp.exp(-m_t))
            (hf_ref if d == 0 else hb_ref)[hh, rows, :] = hc
            log_w = total + a_col
            m_new = jnp.maximum(total + m_prev, jnp.max(log_w, axis=0, keepdims=True))
            kw = kc.astype(F32) * jnp.exp(log_w - m_new)
            c_add = lax.dot_general(kw.astype(BF16), vc, _TN, preferred_element_type=F32)
            n_add = jnp.sum(kw, axis=0, keepdims=True)
            if first:
                cst_ref[hh, d] = c_add
                nst_ref[hh, d] = n_add
            else:
                w_0 = jnp.exp(total + m_prev - m_new)
                cst_ref[hh, d] = w_0 * cst_ref[hh, d] + c_add
                nst_ref[hh, d] = w_0 * nst_ref[hh, d] + n_add
            return m_new

        if has_init:
            for d in range(2):
                cst_ref[hh, d] = c0_ref[0, 0, d, hh]
                nst_ref[hh, d] = n0_ref[0, 0, d, hh]
            m = [jnp.full((1, 1), m0_ref[b, layer, d, h], F32) for d in range(2)]
        else:
            m = [jnp.zeros((1, 1), F32) for _ in range(2)]
        for step in range(nc):
            first = (step == 0) and not has_init
            m[0] = chunk_step(step, 0, m[0], first)
            m[1] = chunk_step(nc - 1 - step, 1, m[1], first)

        hsl = slice(hh * hd, (hh + 1) * hd)
        hhat = _rms(o_refs[hh][...].astype(F32) * (hf_ref[hh] + hb_ref[hh]))
        h_ref[:, hsl] = (hhat * nw_ref[0, :, hsl]).astype(h_ref.dtype)

        if emit_state:
            row = lax.broadcasted_iota(jnp.int32, (8, LANES), 0)
            mst_out[0, hh] = jnp.where(row == 0, m[0], m[1])
            if has_prev:
                for d in range(2):
                    cst_out[0, 0, d, hh] = cprev_ref[0, d, hh]
                    nst_out[0, 0, d, hh] = nprev_ref[0, d, hh]
                    cst_out[0, 1, d, hh] = cst_ref[hh, d]
                    nst_out[0, 1, d, hh] = nst_ref[hh, d]
            else:
                for d in range(2):
                    cst_out[0, d, hh] = cst_ref[hh, d]
                    nst_out[0, d, hh] = nst_ref[hh, d]


def _mlstm(main, gates, conv_w, gate_bias, norm_w, init, layer, batch, seq, emit_state, prev, hps):
    chunk = B_CHUNK
    hd = B_HEAD_DIM
    has_init = init is not None
    has_prev = prev is not None

    def head_cols(off):
        return [pl.BlockSpec((seq, hd), lambda b, g, hh=hh: (b, off // hd + g * hps + hh)) for hh in range(hps)]

    def conv_cols(off):
        return [pl.BlockSpec((1, B_CONV, hd), lambda b, g, hh=hh: (layer, 0, off + g * hps + hh))
                for hh in range(hps)]

    in_specs = [pl.BlockSpec(memory_space=pltpu.SMEM)]
    args = [gate_bias]
    if has_init:
        c0, n0, m0 = init
        in_specs.append(pl.BlockSpec(memory_space=pltpu.SMEM))
        args.append(m0)
    in_specs += (head_cols(MAIN_QB) + head_cols(MAIN_KB) + head_cols(MAIN_VB) + head_cols(MAIN_OB)
                 + [pl.BlockSpec((4 * B_HEADS, seq), lambda b, g: (0, b))]
                 + conv_cols(0) + conv_cols(B_HEADS)
                 + [pl.BlockSpec((1, 1, hps * hd), lambda b, g: (layer, 0, g))])
    args += [main] * (4 * hps) + [gates] + [conv_w] * (2 * hps) + [norm_w.reshape(N_EVEN, 1, B_WIDTH)]
    if has_init:
        in_specs += [pl.BlockSpec((1, 1, 2, hps, hd, hd), lambda b, g: (b, layer, 0, g, 0, 0)),
                     pl.BlockSpec((1, 1, 2, hps, 1, hd), lambda b, g: (b, layer, 0, g, 0, 0))]
        args += [c0, n0.reshape(n0.shape[:4] + (1, hd))]
    if has_prev:
        in_specs += [pl.BlockSpec((1, 2, hps, hd, hd), lambda b, g: (b, 0, g, 0, 0)),
                     pl.BlockSpec((1, 2, hps, 1, hd), lambda b, g: (b, 0, g, 0, 0))]
        args += list(prev)
    out_specs = [pl.BlockSpec((seq, hps * hd), lambda b, g: (b, g))]
    out_shape = [jax.ShapeDtypeStruct((batch * seq, B_WIDTH), BF16)]
    if emit_state:
        if has_prev:
            out_specs += [pl.BlockSpec((1, N_EVEN, 2, hps, hd, hd), lambda b, g: (b, 0, 0, g, 0, 0)),
                          pl.BlockSpec((1, N_EVEN, 2, hps, 1, hd), lambda b, g: (b, 0, 0, g, 0, 0))]
            out_shape += [jax.ShapeDtypeStruct((batch, N_EVEN, 2, B_HEADS, hd, hd), F32),
                          jax.ShapeDtypeStruct((batch, N_EVEN, 2, B_HEADS, 1, hd), F32)]
        else:
            out_specs += [pl.BlockSpec((1, 2, hps, hd, hd), lambda b, g: (b, 0, g, 0, 0)),
                          pl.BlockSpec((1, 2, hps, 1, hd), lambda b, g: (b, 0, g, 0, 0))]
            out_shape += [jax.ShapeDtypeStruct((batch, 2, B_HEADS, hd, hd), F32),
                          jax.ShapeDtypeStruct((batch, 2, B_HEADS, 1, hd), F32)]
        out_specs.append(pl.BlockSpec((1, hps, 8, LANES), lambda b, g: (b, g, 0, 0)))
        out_shape.append(jax.ShapeDtypeStruct((batch, B_HEADS, 8, LANES), F32))
    outs = pl.pallas_call(
        functools.partial(_mlstm_kernel, seq=seq, chunk=chunk, layer=layer, hps=hps, has_init=has_init,
                          emit_state=emit_state, has_prev=has_prev),
        grid=(batch, B_HEADS // hps),
        in_specs=in_specs,
        out_specs=out_specs,
        out_shape=out_shape,
        scratch_shapes=[pltpu.VMEM((hps, seq, hd), BF16), pltpu.VMEM((hps, seq, hd), BF16),
                        pltpu.VMEM((hps, seq, hd), F32), pltpu.VMEM((hps, seq, hd), F32),
                        pltpu.VMEM((hps, 2, hd, hd), F32), pltpu.VMEM((hps, 2, 1, hd), F32)],
        compiler_params=_cparams(2),
        name="mlstm",
    )(*args)
    if not emit_state:
        return outs[0], None
    hb, cst, nst, mst = outs
    mstate = mst[:, :, 0:2, 0].transpose(0, 2, 1)
    return hb, (cst, nst, mstate)


def _outproj_kernel(*refs, n_branch, final):
    it = iter(refs)
    branches = [(next(it), next(it), next(it)) for _ in range(n_branch)]
    w_ref, x_ref, mod_ref = next(it), next(it), next(it)
    fn_ref = next(it) if final else None
    y_ref, wb_ref = next(it), next(it)

    @pl.when(pl.program_id(0) == 0)
    def _():
        wb_ref[...] = w_ref[0].astype(BF16)

    out = None
    for n, (a_ref, zlo_ref, zhi_ref) in enumerate(branches):
        base = n * 2 * Z_BLOCK
        for part, z_ref in enumerate((zlo_ref, zhi_ref)):
            cols = slice(part * Z_BLOCK, (part + 1) * Z_BLOCK)
            gated = a_ref[:, cols] * z_ref[...]
            w = wb_ref[base + part * Z_BLOCK:base + (part + 1) * Z_BLOCK, :]
            p = jnp.dot(gated, w, preferred_element_type=F32)
            out = p if out is None else out + p
    y = x_ref[...] + mod_ref[0, 0, 2:3, :] * out
    if final:
        y = _rms(y) * fn_ref[...]
    y_ref[...] = y


def _outproj(branches, w_out, layer, x, mods, depth_idx, rows_per_mod, final_norm):
    m = x.shape[0]
    tm = 1024
    if rows_per_mod is None:
        mod_map = lambda i: (depth_idx, 0, 0, 0)
    else:
        mod_map = lambda i: (depth_idx, 1 + (i * tm) // rows_per_mod, 0, 0)
    in_specs, args = [], []
    for a, z, z_off in branches:
        zb = z_off // Z_BLOCK
        in_specs += [pl.BlockSpec((tm, 2 * Z_BLOCK), lambda i: (i, 0)),
                     pl.BlockSpec((tm, Z_BLOCK), lambda i, zb=zb: (i, zb)),
                     pl.BlockSpec((tm, Z_BLOCK), lambda i, zb=zb: (i, zb + 1))]
        args += [a, z, z]
    wk = w_out.shape[1]
    in_specs += [pl.BlockSpec((1, wk, D_MODEL), lambda i: (layer, 0, 0), pipeline_mode=pl.Buffered(1)),
                 pl.BlockSpec((tm, D_MODEL), lambda i: (i, 0)),
                 pl.BlockSpec((1, 1, 3, D_MODEL), mod_map)]
    args += [w_out, x, mods]
    final = final_norm is not None
    if final:
        in_specs.append(pl.BlockSpec((1, D_MODEL), lambda i: (0, 0)))
        args.append(final_norm.reshape(1, D_MODEL))
    return pl.pallas_call(
        functools.partial(_outproj_kernel, n_branch=len(branches), final=final),
        grid=(m // tm,),
        in_specs=in_specs,
        out_specs=pl.BlockSpec((tm, D_MODEL), lambda i: (i, 0)),
        out_shape=jax.ShapeDtypeStruct((m, D_MODEL), F32),
        scratch_shapes=[pltpu.VMEM((wk, D_MODEL), BF16)],
        compiler_params=_cparams(1, VMEM_LIMIT_WIDE),
        name="outproj",
    )(*args)


def _mla_slabs(kv, kr_slab):
    lane = lax.broadcasted_iota(jnp.int32, kv.shape, 1)
    nope = lane < C_NOPE
    keys = jnp.where(nope, kv, kr_slab).astype(BF16)
    values = jnp.where(nope, 1.0, kv).astype(BF16)
    return keys, values


def _inproj_c_kernel(*refs, rope, sub):
    it = iter(refs)
    x_ref, mod_ref, g_ref, w_ref, qn_ref, wqb_ref, kvn_ref, wkvb_ref = (next(it) for _ in range(8))
    cos_ref, sin_ref = (next(it), next(it)) if rope else (None, None)
    q_ref, ks_ref, vs_ref, z_ref, ckv_ref, kr_ref, wb_ref = (next(it) for _ in range(7))

    @pl.when(pl.program_id(0) == 0)
    def _():
        wb_ref[...] = w_ref[0].astype(BF16)

    for r in range(x_ref.shape[0] // sub):
        rows = slice(r * sub, (r + 1) * sub)
        hn = _norm_modulate(x_ref[rows, :], g_ref[0], mod_ref[0, 0]).astype(BF16)

        def project(lo, hi):
            return lax.dot_general(hn, wb_ref[lo:hi, :], _NT, preferred_element_type=F32)

        qa = _rms(project(CIN_QA, CIN_KVA)) * qn_ref[...]
        ckv = _rms(project(CIN_KVA, CIN_KR)) * kvn_ref[...]
        z_ref[rows, :] = _silu(project(CIN_Z, CIN_WIDTH)).astype(BF16)
        kr = jnp.concatenate([jnp.zeros((sub, C_NOPE), F32), project(CIN_KR, CIN_Z),
                              jnp.zeros((sub, C_HEAD_PAD - C_NOPE - C_ROPE), F32)], axis=1)
        ckv_ref[rows, :] = ckv
        kr_ref[rows, :] = kr
        kv = jnp.dot(ckv.astype(BF16), wkvb_ref[...], preferred_element_type=F32)
        scale = (C_NOPE + C_ROPE) ** -0.5 * LOG2E
        q = jnp.dot(qa.astype(BF16), wqb_ref[...], preferred_element_type=F32) * scale
        half = C_ROPE // 4
        if rope:
            cos = cos_ref[rows, :]
            sin = sin_ref[rows, :]
            kr = _rope(kr, cos, sin, half)
            for hd in range(C_HEADS):
                hs = slice(hd * C_HEAD_PAD, (hd + 1) * C_HEAD_PAD)
                q_ref[rows, hs] = _rope(q[:, hs], cos, sin, half).astype(BF16)
        else:
            q_ref[rows, :] = q.astype(BF16)
        for hd in range(C_HEADS):
            hs = slice(hd * C_HEAD_PAD, (hd + 1) * C_HEAD_PAD)
            ks_ref[rows, hs], vs_ref[rows, hs] = _mla_slabs(kv[:, hs], kr)


def _inproj_c(x, mods, norm_g, depth_idx, w_t, layer, q_norm, w_qb, kv_norm, w_kvb, rows_per_mod, tables):
    m = x.shape[0]
    tm = 512
    rope = tables is not None
    if rows_per_mod is None:
        mod_map = lambda i: (depth_idx, 0, 0, 0)
    else:
        mod_map = lambda i: (depth_idx, 1 + (i * tm) // rows_per_mod, 0, 0)
    const = lambda i: (0, 0)
    qw = C_HEADS * C_HEAD_PAD
    kvw = C_HEADS * (C_NOPE + C_VDIM)
    in_specs = [pl.BlockSpec((tm, D_MODEL), lambda i: (i, 0)),
                pl.BlockSpec((1, 1, 3, D_MODEL), mod_map),
                pl.BlockSpec((1, 1, D_MODEL), lambda i: (depth_idx, 0, 0)),
                pl.BlockSpec((1, CIN_WIDTH, D_MODEL), lambda i: (layer, 0, 0), pipeline_mode=pl.Buffered(1)),
                pl.BlockSpec((1, C_Q_RANK), const),
                pl.BlockSpec((C_Q_RANK, qw), const),
                pl.BlockSpec((1, C_KV_RANK), const),
                pl.BlockSpec((C_KV_RANK, kvw), const)]
    args = [x, mods, norm_g, w_t, q_norm, w_qb, kv_norm, w_kvb]
    if rope:
        tpb = rows_per_mod // tm
        in_specs += [pl.BlockSpec((tm, LANES), lambda i: (i % tpb, 0))] * 2
        args += list(tables)
    return pl.pallas_call(
        functools.partial(_inproj_c_kernel, rope=rope, sub=tm // 2),
        grid=(m // tm,),
        in_specs=in_specs,
        out_specs=[pl.BlockSpec((tm, qw), lambda i: (i, 0)),
                   pl.BlockSpec((tm, kvw), lambda i: (i, 0)),
                   pl.BlockSpec((tm, kvw), lambda i: (i, 0)),
                   pl.BlockSpec((tm, C_WIDTH), lambda i: (i, 0)),
                   pl.BlockSpec((tm, C_KV_RANK), lambda i: (i, 0)),
                   pl.BlockSpec((tm, LANES), lambda i: (i, 0))],
        out_shape=[jax.ShapeDtypeStruct((m, qw), BF16),
                   jax.ShapeDtypeStruct((m, kvw), BF16),
                   jax.ShapeDtypeStruct((m, kvw), BF16),
                   jax.ShapeDtypeStruct((m, C_WIDTH), BF16),
                   jax.ShapeDtypeStruct((m, C_KV_RANK), F32),
                   jax.ShapeDtypeStruct((m, LANES), F32)],
        scratch_shapes=[pltpu.VMEM((CIN_WIDTH, D_MODEL), BF16)],
        compiler_params=_cparams(1),
        name="inproj_c",
    )(*args)


def _matmul_kernel(x_ref, w_ref, o_ref):
    o_ref[...] = jnp.dot(x_ref[...], w_ref[...], preferred_element_type=F32).astype(o_ref.dtype)


def _matmul(x, w, tm):
    m, k = x.shape
    n = w.shape[1]
    return pl.pallas_call(
        _matmul_kernel,
        grid=(m // tm,),
        in_specs=[pl.BlockSpec((tm, k), lambda i: (i, 0)),
                  pl.BlockSpec((k, n), lambda i: (0, 0))],
        out_specs=pl.BlockSpec((tm, n), lambda i: (i, 0)),
        out_shape=jax.ShapeDtypeStruct((m, n), BF16),
        compiler_params=_cparams(1),
        name="matmul",
    )(x, w)


def _attn_c_kernel(*refs, has_ctx):
    it = iter(refs)
    q_ref, kown_ref, vown_ref = next(it), next(it), next(it)
    kvc_ref, krc_ref = (next(it), next(it)) if has_ctx else (None, None)
    o_ref = next(it)
    kctx_ref, vctx_ref = (next(it), next(it)) if has_ctx else (None, None)

    if has_ctx:
        @pl.when(pl.program_id(1) == 0)
        def _():
            for h in range(C_HEADS):
                hs = slice(h * C_HEAD_PAD, (h + 1) * C_HEAD_PAD)
                kctx_ref[:, hs], vctx_ref[:, hs] = _mla_slabs(kvc_ref[:, hs], krc_ref[...])

    for h in range(C_HEADS):
        hs = slice(h * C_HEAD_PAD, (h + 1) * C_HEAD_PAD)
        qh = q_ref[:, hs]
        scores = [lax.dot_general(qh, kown_ref[:, hs], _NT, preferred_element_type=F32)]
        values = [vown_ref[:, hs]]
        if has_ctx:
            scores.append(lax.dot_general(qh, kctx_ref[:, hs], _NT, preferred_element_type=F32))
            values.append(vctx_ref[:, hs])
        o = _softmax_pv(scores, values, None)
        o_ref[:, h * C_VDIM:(h + 1) * C_VDIM] = o[:, C_NOPE:].astype(o_ref.dtype)


def _attn_c_dense_kernel(q_ref, ks_ref, vs_ref, o_ref, s_ref, e_ref):
    for h in range(C_HEADS):
        hs = slice(h * C_HEAD_PAD, (h + 1) * C_HEAD_PAD)
        s_ref[h] = lax.dot_general(q_ref[:, hs], ks_ref[:, hs], _NT, preferred_element_type=F32)
    s = s_ref[...]
    e_ref[...] = jnp.exp2(s - jnp.max(s, axis=-1, keepdims=True)).astype(BF16)
    for h in range(C_HEADS):
        hs = slice(h * C_HEAD_PAD, (h + 1) * C_HEAD_PAD)
        res = jnp.dot(e_ref[h], vs_ref[:, hs], preferred_element_type=F32)
        o = res / pltpu.roll(res, LANES // 2, 1)
        o_ref[:, h * C_VDIM:(h + 1) * C_VDIM] = o[:, C_NOPE:].astype(o_ref.dtype)


def _attn_c_dense(q, ks, vs, batch, seq):
    w = C_HEADS * C_HEAD_PAD
    return pl.pallas_call(
        _attn_c_dense_kernel,
        grid=(batch,),
        in_specs=[pl.BlockSpec((seq, w), lambda b: (b, 0))] * 3,
        out_specs=pl.BlockSpec((seq, C_WIDTH), lambda b: (b, 0)),
        out_shape=jax.ShapeDtypeStruct((batch * seq, C_WIDTH), BF16),
        scratch_shapes=[pltpu.VMEM((C_HEADS, seq, seq), F32), pltpu.VMEM((C_HEADS, seq, seq), BF16)],
        compiler_params=_cparams(1),
        name="attn_c_dense",
    )(q, ks, vs)


def _attn_c(q, ks, vs, ctx, batch, seq, tq):
    nq = seq // tq
    w = C_HEADS * C_HEAD_PAD
    has_ctx = ctx is not None
    in_specs = [pl.BlockSpec((tq, w), lambda b, i: (b * nq + i, 0)),
                pl.BlockSpec((seq, w), lambda b, i: (b, 0)),
                pl.BlockSpec((seq, w), lambda b, i: (b, 0))]
    args = [q, ks, vs]
    scratch = []
    if has_ctx:
        kv_ctx, kr_ctx = ctx
        nctx = kv_ctx.shape[0] // batch
        in_specs += [pl.BlockSpec((nctx, w), lambda b, i: (b, 0)),
                     pl.BlockSpec((nctx, LANES), lambda b, i: (b, 0))]
        args += [kv_ctx, kr_ctx]
        scratch += [pltpu.VMEM((nctx, w), BF16), pltpu.VMEM((nctx, w), BF16)]
    return pl.pallas_call(
        functools.partial(_attn_c_kernel, has_ctx=has_ctx),
        grid=(batch, nq),
        in_specs=in_specs,
        out_specs=pl.BlockSpec((tq, C_WIDTH), lambda b, i: (b * nq + i, 0)),
        out_shape=jax.ShapeDtypeStruct((batch * seq, C_WIDTH), BF16),
        scratch_shapes=scratch,
        compiler_params=_cparams(2),
        name="attn_c",
    )(*args)


def _rope_tables(n_tokens):
    pos_r = np.repeat(np.arange(n_tokens // GRID_W), GRID_W).astype(np.float64)
    pos_c = np.tile(np.arange(GRID_W), n_tokens // GRID_W).astype(np.float64)

    def seg(d_axis):
        half = d_axis // 2
        freqs = np.power(ROPE_BASE, -np.arange(half, dtype=np.float64) / half)
        cos, sin = [], []
        for pos in (pos_r, pos_c):
            ang = pos[:, None] * freqs[None, :]
            cos += [np.cos(ang), np.cos(ang)]
            sin += [-np.sin(ang), np.sin(ang)]
        return np.concatenate(cos, axis=1), np.concatenate(sin, axis=1)

    cos_a, sin_a = seg(A_HEAD_DIM // 2)
    cos_a, sin_a = np.tile(cos_a, (1, 2)), np.tile(sin_a, (1, 2))
    cos_r, sin_r = seg(C_ROPE // 2)
    ones = np.ones((n_tokens, C_NOPE))
    pad = C_HEAD_PAD - C_NOPE - C_ROPE
    cos_c = np.concatenate([ones, cos_r, np.ones((n_tokens, pad))], axis=1)
    sin_c = np.concatenate([0 * ones, sin_r, np.zeros((n_tokens, pad))], axis=1)
    f = lambda a: jnp.asarray(a, dtype=F32)
    return (f(cos_a), f(sin_a)), (f(cos_c), f(sin_c))


def _pad_cols(w, left, total):
    return jnp.pad(w, ((0, 0), (left, total - left - w.shape[1])))


def kernel(x_prompt, x_sample, cache_a_k, cache_a_v, state_b_mem, state_b_norm, state_b_max, cache_c_kv, cache_c_krope, c, c_ctx, norm_g, w_mod, b_mod, w_in_ab, sink_a, conv_b, gate_bias_b, norm_b, w_out_ab, w_in_c, q_norm_c, w_qb_c, kv_norm_c, w_kvb_c, w_out_c, final_norm):
    bp, tp, _ = x_prompt.shape
    bs, ts, _ = x_sample.shape
    past = cache_a_k.shape[2]
    tables_a, tables_c = _rope_tables(ts)
    w_ab_t = jnp.swapaxes(w_in_ab, 1, 2)
    w_c_t = jnp.swapaxes(w_in_c, 1, 2)

    cond = jnp.zeros((16, D_MODEL), F32).at[0].set(c_ctx).at[1:1 + bs].set(c)
    mods = _adaln(cond, w_mod, b_mod).reshape(DEPTH, 16, 3, D_MODEL)
    gains = norm_g.reshape(DEPTH, 1, D_MODEL)

    yp = x_prompt.reshape(bp * tp, D_MODEL)
    ys = x_sample.reshape(bs * ts, D_MODEL)
    a_k, a_v, b_max, c_kvs, c_krs = [], [], [], [], []
    states = None
    for l in range(DEPTH):
        j = l // 2
        fin = final_norm if l == DEPTH - 1 else None
        if l % 2 == 0:
            w_bf = _cast_rows(w_ab_t, j, MAIN_WIDTH)
            main_p, k_p, v_p, gates_p = _inproj_ab(yp, mods, gains, l, w_bf, w_ab_t, j, tp, True, True, None)
            main_s, gates_s = _inproj_ab(ys, mods, gains, l, w_bf, w_ab_t, j, ts, False, False, tables_a)
            a_k.append(k_p)
            a_v.append(v_p)

            attn_p = _attn_a_prompt(main_p, sink_a[j], bp, tp)
            ck = cache_a_k[:, j].reshape(bs, past, A_KV_WIDTH).astype(BF16)
            cv = cache_a_v[:, j].reshape(bs, past, A_KV_WIDTH).astype(BF16)
            attn_s = _attn_a_sample(main_s, sink_a[j], ck, cv, bs, ts)

            prev = None if states is None else states[:2]
            hb_p, states = _mlstm(main_p, gates_p, conv_b, gate_bias_b, norm_b, None, j, bp, tp, True, prev, B_HEADS)
            init = (state_b_mem, state_b_norm, state_b_max)
            hb_s, _ = _mlstm(main_s, gates_s, conv_b, gate_bias_b, norm_b, init, j, bs, ts, False, None, 1)
            b_max.append(states[2])

            yp = _outproj([(attn_p, main_p, MAIN_ZA), (hb_p, main_p, MAIN_ZB)], w_out_ab, j, yp, mods, l, None, fin)
            ys = _outproj([(attn_s, main_s, MAIN_ZA), (hb_s, main_s, MAIN_ZB)], w_out_ab, j, ys, mods, l, ts, fin)
        else:
            wq = w_qb_c[j].reshape(C_Q_RANK, C_HEADS, C_NOPE + C_ROPE)
            wq = jnp.pad(wq, ((0, 0), (0, 0), (0, C_HEAD_PAD - C_NOPE - C_ROPE)))
            wq = wq.reshape(C_Q_RANK, C_HEADS * C_HEAD_PAD).astype(BF16)
            wkv = w_kvb_c[j].astype(BF16)
            qn = q_norm_c[j].reshape(1, C_Q_RANK)
            kvn = kv_norm_c[j].reshape(1, C_KV_RANK)

            q_p, ks_p, vs_p, z_p, ckv_p, kr_p = _inproj_c(yp, mods, gains, l, w_c_t, j, qn, wq, kvn, wkv, None, None)
            q_s, ks_s, vs_s, z_s, _, _ = _inproj_c(ys, mods, gains, l, w_c_t, j, qn, wq, kvn, wkv, ts, tables_c)
            c_kvs.append(ckv_p.reshape(bp, tp, C_KV_RANK))
            c_krs.append(kr_p[:, C_NOPE:C_NOPE + C_ROPE].reshape(bp, tp, C_ROPE))

            cc = cache_c_kv[:, j].reshape(bs * past, C_KV_RANK).astype(BF16)
            kv_ctx = _matmul(cc, wkv, 512)
            kr_ctx = _pad_cols(cache_c_krope[:, j].reshape(bs * past, C_ROPE), C_NOPE, C_HEAD_PAD).astype(BF16)

            attn_p = _attn_c_dense(q_p, ks_p, vs_p, bp, tp)
            attn_s = _attn_c(q_s, ks_s, vs_s, (kv_ctx, kr_ctx), bs, ts, 512)

            yp = _outproj([(attn_p, z_p, 0)], w_out_c, j, yp, mods, l, None, fin)
            ys = _outproj([(attn_s, z_s, 0)], w_out_c, j, ys, mods, l, ts, fin)

    def cache_layout(per_layer):
        stacked = jnp.stack(per_layer, axis=1).reshape(bp, N_EVEN, A_KV_HEADS, A_HEAD_DIM, tp)
        return stacked.transpose(0, 1, 4, 2, 3)

    b_mem = states[0]
    b_nrm = states[1].reshape(bp, N_EVEN, 2, B_HEADS, B_HEAD_DIM)
    return (yp.reshape(bp, tp, D_MODEL), ys.reshape(bs, ts, D_MODEL),
            cache_layout(a_k), cache_layout(a_v), b_mem, b_nrm,
            jnp.stack(b_max, axis=1), jnp.stack(c_kvs, axis=1), jnp.stack(c_krs, axis=1))
```

```python
import functools
import math

import numpy as np
import jax
import jax.numpy as jnp
from jax import lax
from jax.experimental import pallas as pl
from jax.experimental.pallas import tpu as pltpu

F32 = jnp.float32
BF16 = jnp.bfloat16

D_MODEL = 1024
DEPTH = 4
N_EVEN = 2
EPS = 1e-6
ROPE_BASE = 10000.0
NEG_INF = -1e30
GRID_W = 64
LOG2E = math.log2(math.e)
A_HEADS = 16
A_KV_HEADS = 4
A_GROUP = A_HEADS // A_KV_HEADS
A_HEAD_DIM = 64
A_WIDTH = A_HEADS * A_HEAD_DIM
A_KV_WIDTH = A_KV_HEADS * A_HEAD_DIM
WINDOW = 128
BLOCK = 128
Q_BLOCKS = 4
B_HEADS = 4
B_HEAD_DIM = 256
B_WIDTH = B_HEADS * B_HEAD_DIM
B_CHUNK = 256
C_HEADS = 16
C_NOPE = 64
C_ROPE = 32
C_VDIM = 64
C_Q_RANK = 384
C_KV_RANK = 256
C_WIDTH = C_HEADS * C_VDIM
C_HEAD_PAD = 128

LANES = 128
SUBLANES = 8
B_CONV = 3
MAIN_QA, MAIN_KA, MAIN_VA, MAIN_ZA, MAIN_QB, MAIN_KB, MAIN_VB, MAIN_OB, MAIN_ZB = (
    0, 1024, 1280, 1536, 2560, 3584, 4608, 5632, 6656)
MAIN_WIDTH = 7680
CIN_QA, CIN_KVA, CIN_KR, CIN_Z, CIN_WIDTH = 0, 384, 640, 672, 1696
Z_BLOCK = 512
QK_TILE = 1280
A_QSCALE = A_HEAD_DIM ** -0.5 * LOG2E

MIB = 1024 * 1024
VMEM_LIMIT = 48 * MIB
VMEM_LIMIT_WIDE = 56 * MIB

_NT = (((1,), (1,)), ((), ()))
_TN = (((0,), (0,)), ((), ()))


def _cparams(n_axes, vmem=VMEM_LIMIT):
    return pltpu.CompilerParams(dimension_semantics=("arbitrary",) * n_axes,
                                vmem_limit_bytes=vmem)


def _silu(x):
    return x * jax.nn.sigmoid(x)


def _log_sigmoid(x):
    return jnp.minimum(x, 0.0) - jnp.log1p(jnp.exp(-jnp.abs(x)))


def _rms(x):
    return x * lax.rsqrt(jnp.mean(x * x, axis=-1, keepdims=True) + EPS)


def _norm_modulate(x, g, mod):
    y = _rms(x) * g
    return y * (1.0 + mod[1:2, :]) + mod[0:1, :]


def _swap_halves(x, half):
    lane = lax.broadcasted_iota(jnp.int32, x.shape, 1)
    first = (lane % (2 * half)) < half
    return jnp.where(first, pltpu.roll(x, LANES - half, 1), pltpu.roll(x, half, 1))


def _rope(x, cos, sin, half):
    return x * cos + _swap_halves(x, half) * sin


def _softmax_pv(scores, values, sink):
    tiles = [s[:, t * LANES:(t + 1) * LANES] for s in scores for t in range(s.shape[1] // LANES)]
    m = jnp.max(functools.reduce(jnp.maximum, tiles), axis=-1, keepdims=True)
    if sink is not None:
        m = jnp.maximum(m, sink)
    res = functools.reduce(jnp.add, [
        jnp.dot(jnp.exp2(s - m).astype(BF16), v, preferred_element_type=F32)
        for s, v in zip(scores, values)])
    den = pltpu.roll(res, LANES // 2, 1)
    if sink is not None:
        den = den + jnp.exp2(sink - m)
    return res / den


def _adaln_kernel(c_ref, w_ref, b_ref, o_ref):
    a = _silu(c_ref[...]).astype(BF16)
    w = w_ref[0].astype(BF16)
    o_ref[0] = jnp.dot(a, w, preferred_element_type=F32) + b_ref[0]


def _adaln(cond, w_mod, b_mod):
    tn = 1024
    n = 3 * D_MODEL
    return pl.pallas_call(
        _adaln_kernel,
        grid=(DEPTH, n // tn),
        in_specs=[pl.BlockSpec((16, D_MODEL), lambda l, j: (0, 0)),
                  pl.BlockSpec((1, D_MODEL, tn), lambda l, j: (l, 0, j)),
                  pl.BlockSpec((1, 1, tn), lambda l, j: (l, 0, j))],
        out_specs=pl.BlockSpec((1, 16, tn), lambda l, j: (l, 0, j)),
        out_shape=jax.ShapeDtypeStruct((DEPTH, 16, n), F32),
        compiler_params=_cparams(2),
        name="adaln",
    )(cond, w_mod, b_mod.reshape(DEPTH, 1, n))


def _cast_kernel(w_ref, o_ref):
    o_ref[...] = w_ref[0].astype(o_ref.dtype)


def _cast_rows(w_t, layer, rows):
    tr = QK_TILE
    d = w_t.shape[2]
    return pl.pallas_call(
        _cast_kernel,
        grid=(rows // tr,),
        in_specs=[pl.BlockSpec((1, tr, d), lambda i: (layer, i, 0))],
        out_specs=pl.BlockSpec((tr, d), lambda i: (i, 0)),
        out_shape=jax.ShapeDtypeStruct((rows, d), BF16),
        compiler_params=_cparams(1),
        name="cast_rows",
    )(w_t)


def _main_chunk_kind(col):
    bounds = ((MAIN_KA, "q"), (MAIN_VA, "k"), (MAIN_ZA, "plain"), (MAIN_QB, "silu"), (MAIN_OB, "plain"),
              (MAIN_ZB, "sigmoid"), (MAIN_WIDTH, "silu"))
    return next(kind for end, kind in bounds if col < end)


def _inproj_ab_kernel(*refs, emit_kv, rope, sub):
    it = iter(refs)
    x_ref, mod_ref, g_ref, wb_ref, wg_ref = (next(it) for _ in range(5))
    cos_ref, sin_ref = (next(it), next(it)) if rope else (None, None)
    main_ref = next(it)
    k_ref, v_ref = (next(it), next(it)) if emit_kv else (None, None)
    gate_ref = next(it)

    mod = mod_ref[0, 0]
    g = g_ref[0]
    for r in range(x_ref.shape[0] // sub):
        rows = slice(r * sub, (r + 1) * sub)
        hn = _norm_modulate(x_ref[rows, :], g, mod).astype(BF16)
        gate_ref[:, rows] = lax.dot_general(wg_ref[0].astype(BF16), hn, _NT, preferred_element_type=F32)
        if emit_kv:
            kv_t = lax.dot_general(wb_ref[MAIN_KA:MAIN_ZA, :], hn, _NT, preferred_element_type=F32)
            k_ref[r] = kv_t[:A_KV_WIDTH, :]
            v_ref[r] = kv_t[A_KV_WIDTH:, :]
        for t in range(MAIN_WIDTH // QK_TILE):
            res = lax.dot_general(hn, wb_ref[t * QK_TILE:(t + 1) * QK_TILE, :], _NT, preferred_element_type=F32)
            for c in range(QK_TILE // LANES):
                col = t * QK_TILE + c * LANES
                kind = _main_chunk_kind(col)
                chunk = res[:, c * LANES:(c + 1) * LANES]
                if kind in ("q", "k"):
                    if rope:
                        chunk = _rope(chunk, cos_ref[rows, :], sin_ref[rows, :], A_HEAD_DIM // 4)
                    if kind == "q":
                        chunk = chunk * A_QSCALE
                elif kind == "silu":
                    chunk = _silu(chunk)
                elif kind == "sigmoid":
                    chunk = jax.nn.sigmoid(chunk)
                main_ref[rows, col:col + LANES] = chunk.astype(BF16)


def _inproj_ab(x, mods, norm_g, depth_idx, w_bf, w_t, layer, seq, shared_mod, emit_kv, tables):
    m = x.shape[0]
    tm = 512
    rope = tables is not None
    if shared_mod:
        mod_map = lambda i: (depth_idx, 0, 0, 0)
    else:
        mod_map = lambda i: (depth_idx, 1 + (i * tm) // seq, 0, 0)
    n_gate = 4 * B_HEADS
    tiles_per_table = max(seq, tm) // tm
    table_spec = pl.BlockSpec((tm, LANES), lambda i: (i % tiles_per_table, 0))
    in_specs = [pl.BlockSpec((tm, D_MODEL), lambda i: (i, 0)),
                pl.BlockSpec((1, 1, 3, D_MODEL), mod_map),
                pl.BlockSpec((1, 1, D_MODEL), lambda i: (depth_idx, 0, 0)),
                pl.BlockSpec((MAIN_WIDTH, D_MODEL), lambda i: (0, 0), pipeline_mode=pl.Buffered(1)),
                pl.BlockSpec((1, n_gate, D_MODEL), lambda i: (layer, MAIN_WIDTH // n_gate, 0))]
    args = [x, mods, norm_g, w_bf, w_t]
    if rope:
        in_specs += [table_spec] * 2
        args += list(tables)
    out_specs = [pl.BlockSpec((tm, MAIN_WIDTH), lambda i: (i, 0))]
    out_shape = [jax.ShapeDtypeStruct((m, MAIN_WIDTH), BF16)]
    if emit_kv:
        out_specs += [pl.BlockSpec((tm // seq, A_KV_WIDTH, seq), lambda i: (i, 0, 0))] * 2
        out_shape += [jax.ShapeDtypeStruct((m // seq, A_KV_WIDTH, seq), F32)] * 2
    out_specs.append(pl.BlockSpec((n_gate, tm), lambda i: (0, i)))
    out_shape.append(jax.ShapeDtypeStruct((n_gate, m), F32))
    return pl.pallas_call(
        functools.partial(_inproj_ab_kernel, emit_kv=emit_kv, rope=rope, sub=seq if emit_kv else tm // 2),
        grid=(m // tm,),
        in_specs=in_specs,
        out_specs=out_specs,
        out_shape=out_shape,
        compiler_params=_cparams(1, VMEM_LIMIT_WIDE),
        name="inproj_ab",
    )(*args)


def _value_slabs(v_ref, slab_ref):
    keys = v_ref.shape[0]
    ones = jnp.ones((keys, LANES - A_HEAD_DIM), BF16)
    for g in range(A_KV_HEADS):
        slab_ref[:, g * LANES:g * LANES + A_HEAD_DIM] = v_ref[:, g * A_HEAD_DIM:(g + 1) * A_HEAD_DIM]
        slab_ref[:, g * LANES + A_HEAD_DIM:(g + 1) * LANES] = ones


def _group_queries(q_ref, g):
    return jnp.concatenate([q_ref[:, (g * A_GROUP + hh) * A_HEAD_DIM:(g * A_GROUP + hh + 1) * A_HEAD_DIM]
                            for hh in range(A_GROUP)], axis=0)


def _sink_softmax(s_ref, e_ref, t_ref, sink_ref):
    s = s_ref[...]
    sink = sink_ref[...] * LOG2E
    m = jnp.maximum(jnp.max(s, axis=-1, keepdims=True), sink)
    e_ref[...] = jnp.exp2(s - m).astype(BF16)
    t_ref[...] = jnp.exp2(sink - m)


def _attn_a_prompt_kernel(sink_ref, q_ref, k_ref, v_ref, o_ref, s_ref, e_ref, t_ref, vs_ref):
    _value_slabs(v_ref, vs_ref)
    seq = q_ref.shape[0]
    rows = A_GROUP * seq
    for g in range(A_KV_HEADS):
        heads = slice(g * A_GROUP, (g + 1) * A_GROUP)
        ks = slice(g * A_HEAD_DIM, (g + 1) * A_HEAD_DIM)
        s = lax.dot_general(_group_queries(q_ref, g), k_ref[:, ks], _NT, preferred_element_type=F32)
        s_ref[heads] = s.reshape(A_GROUP, seq, seq)
    _sink_softmax(s_ref, e_ref, t_ref, sink_ref)
    for g in range(A_KV_HEADS):
        heads = slice(g * A_GROUP, (g + 1) * A_GROUP)
        e = e_ref[heads].reshape(rows, seq)
        res = jnp.dot(e, vs_ref[:, g * LANES:(g + 1) * LANES], preferred_element_type=F32)
        den = jnp.dot(e, jnp.ones((seq, LANES), BF16), preferred_element_type=F32)
        o = res / (den + t_ref[heads].reshape(rows, 1))
        for hh in range(A_GROUP):
            h = g * A_GROUP + hh
            o_ref[:, h * A_HEAD_DIM:(h + 1) * A_HEAD_DIM] = (
                o[hh * seq:(hh + 1) * seq, :A_HEAD_DIM].astype(o_ref.dtype))


def _attn_a_prompt(main, sink, batch, seq):
    kb = MAIN_KA // A_KV_WIDTH
    vb = MAIN_VA // A_KV_WIDTH
    return pl.pallas_call(
        _attn_a_prompt_kernel,
        grid=(batch,),
        in_specs=[pl.BlockSpec((A_HEADS, 1, 1), lambda b: (0, 0, 0)),
                  pl.BlockSpec((seq, A_WIDTH), lambda b: (b, MAIN_QA // A_WIDTH)),
                  pl.BlockSpec((seq, A_KV_WIDTH), lambda b: (b, kb)),
                  pl.BlockSpec((seq, A_KV_WIDTH), lambda b: (b, vb))],
        out_specs=pl.BlockSpec((seq, A_WIDTH), lambda b: (b, 0)),
        out_shape=jax.ShapeDtypeStruct((batch * seq, A_WIDTH), BF16),
        scratch_shapes=[pltpu.VMEM((A_HEADS, seq, seq), F32),
                        pltpu.VMEM((A_HEADS, seq, seq), BF16),
                        pltpu.VMEM((A_HEADS, seq, 1), F32),
                        pltpu.VMEM((seq, A_KV_HEADS * LANES), BF16)],
        compiler_params=_cparams(1),
        name="attn_a_prompt",
    )(sink.reshape(A_HEADS, 1, 1), main, main, main)


def _attn_a_sample_kernel(sink_ref, q_ref, k_ref, v_ref, ck_ref, cv_ref, bias_ref, o_ref, vs_ref, cvs_ref):
    i = pl.program_id(1)
    seq = k_ref.shape[0]
    span = 3 * BLOCK

    @pl.when(i == 0)
    def _():
        _value_slabs(v_ref, vs_ref)
        _value_slabs(cv_ref.at[0], cvs_ref)

    ck = ck_ref[0]
    rows = A_GROUP * BLOCK
    head_of_row = lax.broadcasted_iota(jnp.int32, (rows, 1), 0) // BLOCK
    for blk in range(Q_BLOCKS):
        qi = i * Q_BLOCKS + blk
        qrows = slice(blk * BLOCK, (blk + 1) * BLOCK)
        start = pl.multiple_of(jnp.clip((qi - 1) * BLOCK, 0, seq - span), BLOCK)
        kw = k_ref[pl.ds(start, span), :]
        vw = vs_ref[pl.ds(start, span), :]
        bias = bias_ref[(qi * BLOCK - start) // BLOCK]
        bias = jnp.concatenate([bias] * A_GROUP, axis=0)
        for g in range(A_KV_HEADS):
            ks = slice(g * A_HEAD_DIM, (g + 1) * A_HEAD_DIM)
            gs = slice(g * LANES, (g + 1) * LANES)
            sink = jnp.zeros((rows, 1), F32)
            for hh in range(A_GROUP):
                sink = jnp.where(head_of_row == hh, sink_ref[g * A_GROUP + hh] * LOG2E, sink)
            qg = _group_queries(q_ref.at[qrows], g)
            s_loc = lax.dot_general(qg, kw[:, ks], _NT, preferred_element_type=F32) + bias
            s_ctx = lax.dot_general(qg, ck[:, ks], _NT, preferred_element_type=F32)
            o = _softmax_pv([s_loc, s_ctx], [vw[:, gs], cvs_ref[:, gs]], sink)
            for hh in range(A_GROUP):
                h = g * A_GROUP + hh
                o_ref[qrows, h * A_HEAD_DIM:(h + 1) * A_HEAD_DIM] = (
                    o[hh * BLOCK:(hh + 1) * BLOCK, :A_HEAD_DIM].astype(o_ref.dtype))


def _window_bias():
    r = np.arange(BLOCK)[:, None]
    c = np.arange(3 * BLOCK)[None, :]
    masks = [np.where(np.abs(off + r - c) <= WINDOW, 0.0, NEG_INF) for off in (0, BLOCK, 2 * BLOCK)]
    return jnp.asarray(np.stack(masks), dtype=F32)


def _attn_a_sample(main, sink, ck, cv, batch, seq):
    tq = BLOCK * Q_BLOCKS
    nb = seq // tq
    kb = MAIN_KA // A_KV_WIDTH
    vb = MAIN_VA // A_KV_WIDTH
    ctx = ck.shape[1]
    slab_w = A_KV_HEADS * LANES
    return pl.pallas_call(
        _attn_a_sample_kernel,
        grid=(batch, nb),
        in_specs=[pl.BlockSpec(memory_space=pltpu.SMEM),
                  pl.BlockSpec((tq, A_WIDTH), lambda b, i: (b * nb + i, MAIN_QA // A_WIDTH)),
                  pl.BlockSpec((seq, A_KV_WIDTH), lambda b, i: (b, kb)),
                  pl.BlockSpec((seq, A_KV_WIDTH), lambda b, i: (b, vb)),
                  pl.BlockSpec((1, ctx, A_KV_WIDTH), lambda b, i: (b, 0, 0)),
                  pl.BlockSpec((1, ctx, A_KV_WIDTH), lambda b, i: (b, 0, 0)),
                  pl.BlockSpec((3, BLOCK, 3 * BLOCK), lambda b, i: (0, 0, 0))],
        out_specs=pl.BlockSpec((tq, A_WIDTH), lambda b, i: (b * nb + i, 0)),
        out_shape=jax.ShapeDtypeStruct((batch * seq, A_WIDTH), BF16),
        scratch_shapes=[pltpu.VMEM((seq, slab_w), BF16),
                        pltpu.VMEM((ctx, slab_w), BF16)],
        compiler_params=_cparams(2),
        name="attn_a_sample",
    )(sink, main, main, main, ck, cv, _window_bias())


def _mlstm_kernel(*refs, seq, chunk, layer, hps, has_init, emit_state, has_prev):
    it = iter(refs)
    bias_ref = next(it)
    m0_ref = next(it) if has_init else None
    take = lambda: [next(it) for _ in range(hps)]
    q_refs, k_refs, v_refs, o_refs = take(), take(), take(), take()
    g_ref = next(it)
    cwq_refs, cwk_refs = take(), take()
    nw_ref = next(it)
    c0_ref, n0_ref = (next(it), next(it)) if has_init else (None, None)
    cprev_ref, nprev_ref = (next(it), next(it)) if has_prev else (None, None)
    h_ref = next(it)
    cst_out, nst_out, mst_out = (next(it), next(it), next(it)) if emit_state else (None, None, None)
    qs_ref, ks_ref, hf_ref, hb_ref, cst_ref, nst_ref = (next(it) for _ in range(6))

    b = pl.program_id(0)
    head0 = pl.program_id(1) * hps
    nc = seq // chunk
    hd = B_HEAD_DIM

    edge = lax.broadcasted_iota(jnp.int32, (SUBLANES, 1), 0)
    drop_first = (edge != 0).astype(F32)
    drop_last = (edge != SUBLANES - 1).astype(F32)
    rr = lax.broadcasted_iota(jnp.int32, (chunk, chunk), 0)
    cc = lax.broadcasted_iota(jnp.int32, (chunk, chunk), 1)
    diag = rr == cc

    def conv_silu(x_ref, w_ref, scale):
        x = x_ref[...].astype(F32)
        w = w_ref[0]
        if nc == 1:
            up = jnp.dot((cc == rr - 1).astype(BF16), x_ref[...], preferred_element_type=F32)
            dn = jnp.dot((cc == rr + 1).astype(BF16), x_ref[...], preferred_element_type=F32)
        else:
            x_up = jnp.concatenate([x[:seq - SUBLANES], x[seq - SUBLANES:] * drop_last], axis=0)
            x_dn = jnp.concatenate([x[:SUBLANES] * drop_first, x[SUBLANES:]], axis=0)
            up = pltpu.roll(x_up, 1, 0)
            dn = pltpu.roll(x_dn, seq - 1, 0)
        y = up * w[0:1] + x * w[1:2] + dn * w[2:3]
        return (_silu(y) * scale).astype(BF16)

    for hh in range(hps):
        h = head0 + hh
        v_ref = v_refs[hh]
        qs_ref[hh] = conv_silu(q_refs[hh], cwq_refs[hh], 1.0)
        ks_ref[hh] = conv_silu(k_refs[hh], cwk_refs[hh], hd ** -0.5)

        def gate_row(kind):
            ch = kind * B_HEADS + h
            return g_ref[pl.ds(ch, 1), :] + bias_ref[layer, ch]

        li = [gate_row(2 * d) for d in range(2)]
        lf = [_log_sigmoid(gate_row(2 * d + 1)) for d in range(2)]

        def chunk_step(c, d, m_prev, first):
            rows = pl.ds(c * chunk, chunk)
            lanes = slice(c * chunk, (c + 1) * chunk)
            qc = qs_ref[hh, rows, :]
            kc = ks_ref[hh, rows, :]
            vc = v_ref[rows, :]
            li_row = li[d][:, lanes]
            lf_row = lf[d][:, lanes]
            causal = (cc <= rr) if d == 0 else (cc >= rr)
            b_col = jnp.sum(jnp.where(causal, lf_row, 0.0), axis=1, keepdims=True)
            b_row = jnp.sum(jnp.where(diag, b_col, 0.0), axis=0, keepdims=True)
            a_row = li_row - b_row
            a_col = jnp.sum(jnp.where(diag, a_row, 0.0), axis=1, keepdims=True)
            total = jnp.sum(lf_row, axis=1, keepdims=True)
            log_d = jnp.where(causal, b_col + a_row, NEG_INF)
            log_init = b_col + m_prev
            m_t = jnp.maximum(log_init, jnp.max(log_d, axis=1, keepdims=True))
            d_mat = jnp.exp(log_d - m_t)
            s = lax.dot_general(qc, kc, _NT, preferred_element_type=F32) * d_mat
            num = jnp.dot(s.astype(BF16), vc, preferred_element_type=F32)
            den = jnp.sum(s, axis=1, keepdims=True)
            if not first:
                w_init = jnp.exp(log_init - m_t)
                num = num + w_init * jnp.dot(qc, cst_ref[hh, d].astype(BF16), preferred_element_type=F32)
                n_rows = jnp.broadcast_to(nst_ref[hh, d], (SUBLANES, hd)).astype(BF16)
                den = den + w_init * lax.dot_general(qc, n_rows, _NT, preferred_element_type=F32)[:, 0:1]
            hc = num / jnp.maximum(jnp.abs(den), jnp.exp(-m_t))
            (hf_ref if d == 0 else hb_ref)[hh, rows, :] = hc
            log_w = total + a_col
            m_new = jnp.maximum(total + m_prev, jnp.max(log_w, axis=0, keepdims=True))
            kw = kc.astype(F32) * jnp.exp(log_w - m_new)
            c_add = lax.dot_general(kw.astype(BF16), vc, _TN, preferred_element_type=F32)
            n_add = jnp.sum(kw, axis=0, keepdims=True)
            if first:
                cst_ref[hh, d] = c_add
                nst_ref[hh, d] = n_add
            else:
                w_0 = jnp.exp(total + m_prev - m_new)
                cst_ref[hh, d] = w_0 * cst_ref[hh, d] + c_add
                nst_ref[hh, d] = w_0 * nst_ref[hh, d] + n_add
            return m_new

        if has_init:
            for d in range(2):
                cst_ref[hh, d] = c0_ref[0, 0, d, hh]
                nst_ref[hh, d] = n0_ref[0, 0, d, hh]
            m = [jnp.full((1, 1), m0_ref[b, layer, d, h], F32) for d in range(2)]
        else:
            m = [jnp.zeros((1, 1), F32) for _ in range(2)]
        for step in range(nc):
            first = (step == 0) and not has_init
            m[0] = chunk_step(step, 0, m[0], first)
            m[1] = chunk_step(nc - 1 - step, 1, m[1], first)

        hsl = slice(hh * hd, (hh + 1) * hd)
        hhat = _rms(o_refs[hh][...].astype(F32) * (hf_ref[hh] + hb_ref[hh]))
        h_ref[:, hsl] = (hhat * nw_ref[0, :, hsl]).astype(h_ref.dtype)

        if emit_state:
            row = lax.broadcasted_iota(jnp.int32, (8, LANES), 0)
            mst_out[0, hh] = jnp.where(row == 0, m[0], m[1])
            if has_prev:
                for d in range(2):
                    cst_out[0, 0, d, hh] = cprev_ref[0, d, hh]
                    nst_out[0, 0, d, hh] = nprev_ref[0, d, hh]
                    cst_out[0, 1, d, hh] = cst_ref[hh, d]
                    nst_out[0, 1, d, hh] = nst_ref[hh, d]
            else:
                for d in range(2):
                    cst_out[0, d, hh] = cst_ref[hh, d]
                    nst_out[0, d, hh] = nst_ref[hh, d]


def _mlstm(main, gates, conv_w, gate_bias, norm_w, init, layer, batch, seq, emit_state, prev, hps):
    chunk = B_CHUNK
    hd = B_HEAD_DIM
    has_init = init is not None
    has_prev = prev is not None

    def head_cols(off):
        return [pl.BlockSpec((seq, hd), lambda b, g, hh=hh: (b, off // hd + g * hps + hh)) for hh in range(hps)]

    def conv_cols(off):
        return [pl.BlockSpec((1, B_CONV, hd), lambda b, g, hh=hh: (layer, 0, off + g * hps + hh))
                for hh in range(hps)]

    in_specs = [pl.BlockSpec(memory_space=pltpu.SMEM)]
    args = [gate_bias]
    if has_init:
        c0, n0, m0 = init
        in_specs.append(pl.BlockSpec(memory_space=pltpu.SMEM))
        args.append(m0)
    in_specs += (head_cols(MAIN_QB) + head_cols(MAIN_KB) + head_cols(MAIN_VB) + head_cols(MAIN_OB)
                 + [pl.BlockSpec((4 * B_HEADS, seq), lambda b, g: (0, b))]
                 + conv_cols(0) + conv_cols(B_HEADS)
                 + [pl.BlockSpec((1, 1, hps * hd), lambda b, g: (layer, 0, g))])
    args += [main] * (4 * hps) + [gates] + [conv_w] * (2 * hps) + [norm_w.reshape(N_EVEN, 1, B_WIDTH)]
    if has_init:
        in_specs += [pl.BlockSpec((1, 1, 2, hps, hd, hd), lambda b, g: (b, layer, 0, g, 0, 0)),
                     pl.BlockSpec((1, 1, 2, hps, 1, hd), lambda b, g: (b, layer, 0, g, 0, 0))]
        args += [c0, n0.reshape(n0.shape[:4] + (1, hd))]
    if has_prev:
        in_specs += [pl.BlockSpec((1, 2, hps, hd, hd), lambda b, g: (b, 0, g, 0, 0)),
                     pl.BlockSpec((1, 2, hps, 1, hd), lambda b, g: (b, 0, g, 0, 0))]
        args += list(prev)
    out_specs = [pl.BlockSpec((seq, hps * hd), lambda b, g: (b, g))]
    out_shape = [jax.ShapeDtypeStruct((batch * seq, B_WIDTH), BF16)]
    if emit_state:
        if has_prev:
            out_specs += [pl.BlockSpec((1, N_EVEN, 2, hps, hd, hd), lambda b, g: (b, 0, 0, g, 0, 0)),
                          pl.BlockSpec((1, N_EVEN, 2, hps, 1, hd), lambda b, g: (b, 0, 0, g, 0, 0))]
            out_shape += [jax.ShapeDtypeStruct((batch, N_EVEN, 2, B_HEADS, hd, hd), F32),
                          jax.ShapeDtypeStruct((batch, N_EVEN, 2, B_HEADS, 1, hd), F32)]
        else:
            out_specs += [pl.BlockSpec((1, 2, hps, hd, hd), lambda b, g: (b, 0, g, 0, 0)),
                          pl.BlockSpec((1, 2, hps, 1, hd), lambda b, g: (b, 0, g, 0, 0))]
            out_shape += [jax.ShapeDtypeStruct((batch, 2, B_HEADS, hd, hd), F32),
                          jax.ShapeDtypeStruct((batch, 2, B_HEADS, 1, hd), F32)]
        out_specs.append(pl.BlockSpec((1, hps, 8, LANES), lambda b, g: (b, g, 0, 0)))
        out_shape.append(jax.ShapeDtypeStruct((batch, B_HEADS, 8, LANES), F32))
    outs = pl.pallas_call(
        functools.partial(_mlstm_kernel, seq=seq, chunk=chunk, layer=layer, hps=hps, has_init=has_init,
                          emit_state=emit_state, has_prev=has_prev),
        grid=(batch, B_HEADS // hps),
        in_specs=in_specs,
        out_specs=out_specs,
        out_shape=out_shape,
        scratch_shapes=[pltpu.VMEM((hps, seq, hd), BF16), pltpu.VMEM((hps, seq, hd), BF16),
                        pltpu.VMEM((hps, seq, hd), F32), pltpu.VMEM((hps, seq, hd), F32),
                        pltpu.VMEM((hps, 2, hd, hd), F32), pltpu.VMEM((hps, 2, 1, hd), F32)],
        compiler_params=_cparams(2),
        name="mlstm",
    )(*args)
    if not emit_state:
        return outs[0], None
    hb, cst, nst, mst = outs
    mstate = mst[:, :, 0:2, 0].transpose(0, 2, 1)
    return hb, (cst, nst, mstate)


def _outproj_kernel(*refs, n_branch, final):
    it = iter(refs)
    branches = [(next(it), next(it), next(it)) for _ in range(n_branch)]
    w_ref, x_ref, mod_ref = next(it), next(it), next(it)
    fn_ref = next(it) if final else None
    y_ref, wb_ref = next(it), next(it)

    @pl.when(pl.program_id(0) == 0)
    def _():
        wb_ref[...] = w_ref[0].astype(BF16)

    out = None
    for n, (a_ref, zlo_ref, zhi_ref) in enumerate(branches):
        base = n * 2 * Z_BLOCK
        for part, z_ref in enumerate((zlo_ref, zhi_ref)):
            cols = slice(part * Z_BLOCK, (part + 1) * Z_BLOCK)
            gated = a_ref[:, cols] * z_ref[...]
            w = wb_ref[base + part * Z_BLOCK:base + (part + 1) * Z_BLOCK, :]
            p = jnp.dot(gated, w, preferred_element_type=F32)
            out = p if out is None else out + p
    y = x_ref[...] + mod_ref[0, 0, 2:3, :] * out
    if final:
        y = _rms(y) * fn_ref[...]
    y_ref[...] = y


def _outproj(branches, w_out, layer, x, mods, depth_idx, rows_per_mod, final_norm):
    m = x.shape[0]
    tm = 1024
    if rows_per_mod is None:
        mod_map = lambda i: (depth_idx, 0, 0, 0)
    else:
        mod_map = lambda i: (depth_idx, 1 + (i * tm) // rows_per_mod, 0, 0)
    in_specs, args = [], []
    for a, z, z_off in branches:
        zb = z_off // Z_BLOCK
        in_specs += [pl.BlockSpec((tm, 2 * Z_BLOCK), lambda i: (i, 0)),
                     pl.BlockSpec((tm, Z_BLOCK), lambda i, zb=zb: (i, zb)),
                     pl.BlockSpec((tm, Z_BLOCK), lambda i, zb=zb: (i, zb + 1))]
        args += [a, z, z]
    wk = w_out.shape[1]
    in_specs += [pl.BlockSpec((1, wk, D_MODEL), lambda i: (layer, 0, 0), pipeline_mode=pl.Buffered(1)),
                 pl.BlockSpec((tm, D_MODEL), lambda i: (i, 0)),
                 pl.BlockSpec((1, 1, 3, D_MODEL), mod_map)]
    args += [w_out, x, mods]
    final = final_norm is not None
    if final:
        in_specs.append(pl.BlockSpec((1, D_MODEL), lambda i: (0, 0)))
        args.append(final_norm.reshape(1, D_MODEL))
    return pl.pallas_call(
        functools.partial(_outproj_kernel, n_branch=len(branches), final=final),
        grid=(m // tm,),
        in_specs=in_specs,
        out_specs=pl.BlockSpec((tm, D_MODEL), lambda i: (i, 0)),
        out_shape=jax.ShapeDtypeStruct((m, D_MODEL), F32),
        scratch_shapes=[pltpu.VMEM((wk, D_MODEL), BF16)],
        compiler_params=_cparams(1, VMEM_LIMIT_WIDE),
        name="outproj",
    )(*args)


def _mla_slabs(kv, kr_slab):
    lane = lax.broadcasted_iota(jnp.int32, kv.shape, 1)
    nope = lane < C_NOPE
    keys = jnp.where(nope, kv, kr_slab).astype(BF16)
    values = jnp.where(nope, 1.0, kv).astype(BF16)
    return keys, values


def _inproj_c_kernel(*refs, rope, sub):
    it = iter(refs)
    x_ref, mod_ref, g_ref, w_ref, qn_ref, wqb_ref, kvn_ref, wkvb_ref = (next(it) for _ in range(8))
    cos_ref, sin_ref = (next(it), next(it)) if rope else (None, None)
    q_ref, ks_ref, vs_ref, z_ref, ckv_ref, kr_ref, wb_ref = (next(it) for _ in range(7))

    @pl.when(pl.program_id(0) == 0)
    def _():
        wb_ref[...] = w_ref[0].astype(BF16)

    for r in range(x_ref.shape[0] // sub):
        rows = slice(r * sub, (r + 1) * sub)
        hn = _norm_modulate(x_ref[rows, :], g_ref[0], mod_ref[0, 0]).astype(BF16)

        def project(lo, hi):
            return lax.dot_general(hn, wb_ref[lo:hi, :], _NT, preferred_element_type=F32)

        qa = _rms(project(CIN_QA, CIN_KVA)) * qn_ref[...]
        ckv = _rms(project(CIN_KVA, CIN_KR)) * kvn_ref[...]
        z_ref[rows, :] = _silu(project(CIN_Z, CIN_WIDTH)).astype(BF16)
        kr = jnp.concatenate([jnp.zeros((sub, C_NOPE), F32), project(CIN_KR, CIN_Z),
                              jnp.zeros((sub, C_HEAD_PAD - C_NOPE - C_ROPE), F32)], axis=1)
        ckv_ref[rows, :] = ckv
        kr_ref[rows, :] = kr
        kv = jnp.dot(ckv.astype(BF16), wkvb_ref[...], preferred_element_type=F32)
        scale = (C_NOPE + C_ROPE) ** -0.5 * LOG2E
        q = jnp.dot(qa.astype(BF16), wqb_ref[...], preferred_element_type=F32) * scale
        half = C_ROPE // 4
        if rope:
            cos = cos_ref[rows, :]
            sin = sin_ref[rows, :]
            kr = _rope(kr, cos, sin, half)
            for hd in range(C_HEADS):
                hs = slice(hd * C_HEAD_PAD, (hd + 1) * C_HEAD_PAD)
                q_ref[rows, hs] = _rope(q[:, hs], cos, sin, half).astype(BF16)
        else:
            q_ref[rows, :] = q.astype(BF16)
        for hd in range(C_HEADS):
            hs = slice(hd * C_HEAD_PAD, (hd + 1) * C_HEAD_PAD)
            ks_ref[rows, hs], vs_ref[rows, hs] = _mla_slabs(kv[:, hs], kr)


def _inproj_c(x, mods, norm_g, depth_idx, w_t, layer, q_norm, w_qb, kv_norm, w_kvb, rows_per_mod, tables):
    m = x.shape[0]
    tm = 512
    rope = tables is not None
    if rows_per_mod is None:
        mod_map = lambda i: (depth_idx, 0, 0, 0)
    else:
        mod_map = lambda i: (depth_idx, 1 + (i * tm) // rows_per_mod, 0, 0)
    const = lambda i: (0, 0)
    qw = C_HEADS * C_HEAD_PAD
    kvw = C_HEADS * (C_NOPE + C_VDIM)
    in_specs = [pl.BlockSpec((tm, D_MODEL), lambda i: (i, 0)),
                pl.BlockSpec((1, 1, 3, D_MODEL), mod_map),
                pl.BlockSpec((1, 1, D_MODEL), lambda i: (depth_idx, 0, 0)),
                pl.BlockSpec((1, CIN_WIDTH, D_MODEL), lambda i: (layer, 0, 0), pipeline_mode=pl.Buffered(1)),
                pl.BlockSpec((1, C_Q_RANK), const),
                pl.BlockSpec((C_Q_RANK, qw), const),
                pl.BlockSpec((1, C_KV_RANK), const),
                pl.BlockSpec((C_KV_RANK, kvw), const)]
    args = [x, mods, norm_g, w_t, q_norm, w_qb, kv_norm, w_kvb]
    if rope:
        tpb = rows_per_mod // tm
        in_specs += [pl.BlockSpec((tm, LANES), lambda i: (i % tpb, 0))] * 2
        args += list(tables)
    return pl.pallas_call(
        functools.partial(_inproj_c_kernel, rope=rope, sub=tm // 2),
        grid=(m // tm,),
        in_specs=in_specs,
        out_specs=[pl.BlockSpec((tm, qw), lambda i: (i, 0)),
                   pl.BlockSpec((tm, kvw), lambda i: (i, 0)),
                   pl.BlockSpec((tm, kvw), lambda i: (i, 0)),
                   pl.BlockSpec((tm, C_WIDTH), lambda i: (i, 0)),
                   pl.BlockSpec((tm, C_KV_RANK), lambda i: (i, 0)),
                   pl.BlockSpec((tm, LANES), lambda i: (i, 0))],
        out_shape=[jax.ShapeDtypeStruct((m, qw), BF16),
                   jax.ShapeDtypeStruct((m, kvw), BF16),
                   jax.ShapeDtypeStruct((m, kvw), BF16),
                   jax.ShapeDtypeStruct((m, C_WIDTH), BF16),
                   jax.ShapeDtypeStruct((m, C_KV_RANK), F32),
                   jax.ShapeDtypeStruct((m, LANES), F32)],
        scratch_shapes=[pltpu.VMEM((CIN_WIDTH, D_MODEL), BF16)],
        compiler_params=_cparams(1),
        name="inproj_c",
    )(*args)


def _matmul_kernel(x_ref, w_ref, o_ref):
    o_ref[...] = jnp.dot(x_ref[...], w_ref[...], preferred_element_type=F32).astype(o_ref.dtype)


def _matmul(x, w, tm):
    m, k = x.shape
    n = w.shape[1]
    return pl.pallas_call(
        _matmul_kernel,
        grid=(m // tm,),
        in_specs=[pl.BlockSpec((tm, k), lambda i: (i, 0)),
                  pl.BlockSpec((k, n), lambda i: (0, 0))],
        out_specs=pl.BlockSpec((tm, n), lambda i: (i, 0)),
        out_shape=jax.ShapeDtypeStruct((m, n), BF16),
        compiler_params=_cparams(1),
        name="matmul",
    )(x, w)


def _attn_c_kernel(*refs, has_ctx):
    it = iter(refs)
    q_ref, kown_ref, vown_ref = next(it), next(it), next(it)
    kvc_ref, krc_ref = (next(it), next(it)) if has_ctx else (None, None)
    o_ref = next(it)
    kctx_ref, vctx_ref = (next(it), next(it)) if has_ctx else (None, None)

    if has_ctx:
        @pl.when(pl.program_id(1) == 0)
        def _():
            for h in range(C_HEADS):
                hs = slice(h * C_HEAD_PAD, (h + 1) * C_HEAD_PAD)
                kctx_ref[:, hs], vctx_ref[:, hs] = _mla_slabs(kvc_ref[:, hs], krc_ref[...])

    for h in range(C_HEADS):
        hs = slice(h * C_HEAD_PAD, (h + 1) * C_HEAD_PAD)
        qh = q_ref[:, hs]
        scores = [lax.dot_general(qh, kown_ref[:, hs], _NT, preferred_element_type=F32)]
        values = [vown_ref[:, hs]]
        if has_ctx:
            scores.append(lax.dot_general(qh, kctx_ref[:, hs], _NT, preferred_element_type=F32))
            values.append(vctx_ref[:, hs])
        o = _softmax_pv(scores, values, None)
        o_ref[:, h * C_VDIM:(h + 1) * C_VDIM] = o[:, C_NOPE:].astype(o_ref.dtype)


def _attn_c_dense_kernel(q_ref, ks_ref, vs_ref, o_ref, s_ref, e_ref):
    for h in range(C_HEADS):
        hs = slice(h * C_HEAD_PAD, (h + 1) * C_HEAD_PAD)
        s_ref[h] = lax.dot_general(q_ref[:, hs], ks_ref[:, hs], _NT, preferred_element_type=F32)
    s = s_ref[...]
    e_ref[...] = jnp.exp2(s - jnp.max(s, axis=-1, keepdims=True)).astype(BF16)
    for h in range(C_HEADS):
        hs = slice(h * C_HEAD_PAD, (h + 1) * C_HEAD_PAD)
        res = jnp.dot(e_ref[h], vs_ref[:, hs], preferred_element_type=F32)
        o = res / pltpu.roll(res, LANES // 2, 1)
        o_ref[:, h * C_VDIM:(h + 1) * C_VDIM] = o[:, C_NOPE:].astype(o_ref.dtype)


def _attn_c_dense(q, ks, vs, batch, seq):
    w = C_HEADS * C_HEAD_PAD
    return pl.pallas_call(
        _attn_c_dense_kernel,
        grid=(batch,),
        in_specs=[pl.BlockSpec((seq, w), lambda b: (b, 0))] * 3,
        out_specs=pl.BlockSpec((seq, C_WIDTH), lambda b: (b, 0)),
        out_shape=jax.ShapeDtypeStruct((batch * seq, C_WIDTH), BF16),
        scratch_shapes=[pltpu.VMEM((C_HEADS, seq, seq), F32), pltpu.VMEM((C_HEADS, seq, seq), BF16)],
        compiler_params=_cparams(1),
        name="attn_c_dense",
    )(q, ks, vs)


def _attn_c(q, ks, vs, ctx, batch, seq, tq):
    nq = seq // tq
    w = C_HEADS * C_HEAD_PAD
    has_ctx = ctx is not None
    in_specs = [pl.BlockSpec((tq, w), lambda b, i: (b * nq + i, 0)),
                pl.BlockSpec((seq, w), lambda b, i: (b, 0)),
                pl.BlockSpec((seq, w), lambda b, i: (b, 0))]
    args = [q, ks, vs]
    scratch = []
    if has_ctx:
        kv_ctx, kr_ctx = ctx
        nctx = kv_ctx.shape[0] // batch
        in_specs += [pl.BlockSpec((nctx, w), lambda b, i: (b, 0)),
                     pl.BlockSpec((nctx, LANES), lambda b, i: (b, 0))]
        args += [kv_ctx, kr_ctx]
        scratch += [pltpu.VMEM((nctx, w), BF16), pltpu.VMEM((nctx, w), BF16)]
    return pl.pallas_call(
        functools.partial(_attn_c_kernel, has_ctx=has_ctx),
        grid=(batch, nq),
        in_specs=in_specs,
        out_specs=pl.BlockSpec((tq, C_WIDTH), lambda b, i: (b * nq + i, 0)),
        out_shape=jax.ShapeDtypeStruct((batch * seq, C_WIDTH), BF16),
        scratch_shapes=scratch,
        compiler_params=_cparams(2, VMEM_LIMIT_WIDE),
        name="attn_c",
    )(*args)


def _rope_tables(n_tokens):
    pos_r = np.repeat(np.arange(n_tokens // GRID_W), GRID_W).astype(np.float64)
    pos_c = np.tile(np.arange(GRID_W), n_tokens // GRID_W).astype(np.float64)

    def seg(d_axis):
        half = d_axis // 2
        freqs = np.power(ROPE_BASE, -np.arange(half, dtype=np.float64) / half)
        cos, sin = [], []
        for pos in (pos_r, pos_c):
            ang = pos[:, None] * freqs[None, :]
            cos += [np.cos(ang), np.cos(ang)]
            sin += [-np.sin(ang), np.sin(ang)]
        return np.concatenate(cos, axis=1), np.concatenate(sin, axis=1)

    cos_a, sin_a = seg(A_HEAD_DIM // 2)
    cos_a, sin_a = np.tile(cos_a, (1, 2)), np.tile(sin_a, (1, 2))
    cos_r, sin_r = seg(C_ROPE // 2)
    ones = np.ones((n_tokens, C_NOPE))
    pad = C_HEAD_PAD - C_NOPE - C_ROPE
    cos_c = np.concatenate([ones, cos_r, np.ones((n_tokens, pad))], axis=1)
    sin_c = np.concatenate([0 * ones, sin_r, np.zeros((n_tokens, pad))], axis=1)
    f = lambda a: jnp.asarray(a, dtype=F32)
    return (f(cos_a), f(sin_a)), (f(cos_c), f(sin_c))


def _pad_cols(w, left, total):
    return jnp.pad(w, ((0, 0), (left, total - left - w.shape[1])))


def kernel(x_prompt, x_sample, cache_a_k, cache_a_v, state_b_mem, state_b_norm, state_b_max, cache_c_kv, cache_c_krope, c, c_ctx, norm_g, w_mod, b_mod, w_in_ab, sink_a, conv_b, gate_bias_b, norm_b, w_out_ab, w_in_c, q_norm_c, w_qb_c, kv_norm_c, w_kvb_c, w_out_c, final_norm):
    bp, tp, _ = x_prompt.shape
    bs, ts, _ = x_sample.shape
    past = cache_a_k.shape[2]
    tables_a, tables_c = _rope_tables(ts)
    w_ab_t = jnp.swapaxes(w_in_ab, 1, 2)
    w_c_t = jnp.swapaxes(w_in_c, 1, 2)

    cond = jnp.zeros((16, D_MODEL), F32).at[0].set(c_ctx).at[1:1 + bs].set(c)
    mods = _adaln(cond, w_mod, b_mod).reshape(DEPTH, 16, 3, D_MODEL)
    gains = norm_g.reshape(DEPTH, 1, D_MODEL)

    yp = x_prompt.reshape(bp * tp, D_MODEL)
    ys = x_sample.reshape(bs * ts, D_MODEL)
    a_k, a_v, b_max, c_kvs, c_krs = [], [], [], [], []
    states = None
    for l in range(DEPTH):
        j = l // 2
        fin = final_norm if l == DEPTH - 1 else None
        if l % 2 == 0:
            w_bf = _cast_rows(w_ab_t, j, MAIN_WIDTH)
            main_p, k_p, v_p, gates_p = _inproj_ab(yp, mods, gains, l, w_bf, w_ab_t, j, tp, True, True, None)
            main_s, gates_s = _inproj_ab(ys, mods, gains, l, w_bf, w_ab_t, j, ts, False, False, tables_a)
            a_k.append(k_p)
            a_v.append(v_p)

            attn_p = _attn_a_prompt(main_p, sink_a[j], bp, tp)
            ck = cache_a_k[:, j].reshape(bs, past, A_KV_WIDTH).astype(BF16)
            cv = cache_a_v[:, j].reshape(bs, past, A_KV_WIDTH).astype(BF16)
            attn_s = _attn_a_sample(main_s, sink_a[j], ck, cv, bs, ts)

            prev = None if states is None else states[:2]
            hb_p, states = _mlstm(main_p, gates_p, conv_b, gate_bias_b, norm_b, None, j, bp, tp, True, prev, B_HEADS)
            init = (state_b_mem, state_b_norm, state_b_max)
            hb_s, _ = _mlstm(main_s, gates_s, conv_b, gate_bias_b, norm_b, init, j, bs, ts, False, None, 1)
            b_max.append(states[2])

            yp = _outproj([(attn_p, main_p, MAIN_ZA), (hb_p, main_p, MAIN_ZB)], w_out_ab, j, yp, mods, l, None, fin)
            ys = _outproj([(attn_s, main_s, MAIN_ZA), (hb_s, main_s, MAIN_ZB)], w_out_ab, j, ys, mods, l, ts, fin)
        else:
            wq = w_qb_c[j].reshape(C_Q_RANK, C_HEADS, C_NOPE + C_ROPE)
            wq = jnp.pad(wq, ((0, 0), (0, 0), (0, C_HEAD_PAD - C_NOPE - C_ROPE)))
            wq = wq.reshape(C_Q_RANK, C_HEADS * C_HEAD_PAD).astype(BF16)
            wkv = w_kvb_c[j].astype(BF16)
            qn = q_norm_c[j].reshape(1, C_Q_RANK)
            kvn = kv_norm_c[j].reshape(1, C_KV_RANK)

            q_p, ks_p, vs_p, z_p, ckv_p, kr_p = _inproj_c(yp, mods, gains, l, w_c_t, j, qn, wq, kvn, wkv, None, None)
            q_s, ks_s, vs_s, z_s, _, _ = _inproj_c(ys, mods, gains, l, w_c_t, j, qn, wq, kvn, wkv, ts, tables_c)
            c_kvs.append(ckv_p.reshape(bp, tp, C_KV_RANK))
            c_krs.append(kr_p[:, C_NOPE:C_NOPE + C_ROPE].reshape(bp, tp, C_ROPE))

            cc = cache_c_kv[:, j].reshape(bs * past, C_KV_RANK).astype(BF16)
            kv_ctx = _matmul(cc, wkv, 512)
            kr_ctx = _pad_cols(cache_c_krope[:, j].reshape(bs * past, C_ROPE), C_NOPE, C_HEAD_PAD).astype(BF16)

            attn_p = _attn_c_dense(q_p, ks_p, vs_p, bp, tp)
            attn_s = _attn_c(q_s, ks_s, vs_s, (kv_ctx, kr_ctx), bs, ts, ts)

            yp = _outproj([(attn_p, z_p, 0)], w_out_c, j, yp, mods, l, None, fin)
            ys = _outproj([(attn_s, z_s, 0)], w_out_c, j, ys, mods, l, ts, fin)

    def cache_layout(per_layer):
        stacked = jnp.stack(per_layer, axis=1).reshape(bp, N_EVEN, A_KV_HEADS, A_HEAD_DIM, tp)
        return stacked.transpose(0, 1, 4, 2, 3)

    b_mem = states[0]
    b_nrm = states[1].reshape(bp, N_EVEN, 2, B_HEADS, B_HEAD_DIM)
    return (yp.reshape(bp, tp, D_MODEL), ys.reshape(bs, ts, D_MODEL),
            cache_layout(a_k), cache_layout(a_v), b_mem, b_nrm,
            jnp.stack(b_max, axis=1), jnp.stack(c_kvs, axis=1), jnp.stack(c_krs, axis=1))
```

```python
import functools
import math

import numpy as np
import jax
import jax.numpy as jnp
from jax import lax
from jax.experimental import pallas as pl
from jax.experimental.pallas import tpu as pltpu

F32 = jnp.float32
BF16 = jnp.bfloat16

D_MODEL = 1024
DEPTH = 4
N_EVEN = 2
EPS = 1e-6
ROPE_BASE = 10000.0
NEG_INF = -1e30
GRID_W = 64
LOG2E = math.log2(math.e)
A_HEADS = 16
A_KV_HEADS = 4
A_GROUP = A_HEADS // A_KV_HEADS
A_HEAD_DIM = 64
A_WIDTH = A_HEADS * A_HEAD_DIM
A_KV_WIDTH = A_KV_HEADS * A_HEAD_DIM
WINDOW = 128
BLOCK = 128
Q_BLOCKS = 4
B_HEADS = 4
B_HEAD_DIM = 256
B_WIDTH = B_HEADS * B_HEAD_DIM
B_CHUNK = 256
C_HEADS = 16
C_NOPE = 64
C_ROPE = 32
C_VDIM = 64
C_Q_RANK = 384
C_KV_RANK = 256
C_WIDTH = C_HEADS * C_VDIM
C_HEAD_PAD = 128

LANES = 128
SUBLANES = 8
B_CONV = 3
MAIN_QA, MAIN_KA, MAIN_VA, MAIN_ZA, MAIN_QB, MAIN_KB, MAIN_VB, MAIN_OB, MAIN_ZB = (
    0, 1024, 1280, 1536, 2560, 3584, 4608, 5632, 6656)
MAIN_WIDTH = 7680
CIN_QA, CIN_KVA, CIN_KR, CIN_Z, CIN_WIDTH = 0, 384, 640, 672, 1696
Z_BLOCK = 512
QK_TILE = 1280
A_QSCALE = A_HEAD_DIM ** -0.5 * LOG2E

MIB = 1024 * 1024
VMEM_LIMIT = 48 * MIB
VMEM_LIMIT_WIDE = 56 * MIB

_NT = (((1,), (1,)), ((), ()))
_TN = (((0,), (0,)), ((), ()))


def _cparams(n_axes, vmem=VMEM_LIMIT):
    return pltpu.CompilerParams(dimension_semantics=("arbitrary",) * n_axes,
                                vmem_limit_bytes=vmem)


def _silu(x):
    return x * jax.nn.sigmoid(x)


def _log_sigmoid(x):
    return jnp.minimum(x, 0.0) - jnp.log1p(jnp.exp(-jnp.abs(x)))


def _rms(x):
    return x * lax.rsqrt(jnp.mean(x * x, axis=-1, keepdims=True) + EPS)


def _norm_modulate(x, g, mod):
    y = _rms(x) * g
    return y * (1.0 + mod[1:2, :]) + mod[0:1, :]


def _swap_halves(x, half):
    lane = lax.broadcasted_iota(jnp.int32, x.shape, 1)
    first = (lane % (2 * half)) < half
    return jnp.where(first, pltpu.roll(x, LANES - half, 1), pltpu.roll(x, half, 1))


def _rope(x, cos, sin, half):
    return x * cos + _swap_halves(x, half) * sin


def _softmax_pv(scores, values, sink):
    tiles = [s[:, t * LANES:(t + 1) * LANES] for s in scores for t in range(s.shape[1] // LANES)]
    m = jnp.max(functools.reduce(jnp.maximum, tiles), axis=-1, keepdims=True)
    if sink is not None:
        m = jnp.maximum(m, sink)
    res = functools.reduce(jnp.add, [
        jnp.dot(jnp.exp2(s - m).astype(BF16), v, preferred_element_type=F32)
        for s, v in zip(scores, values)])
    den = pltpu.roll(res, LANES // 2, 1)
    if sink is not None:
        den = den + jnp.exp2(sink - m)
    return res / den


def _adaln_kernel(c_ref, w_ref, b_ref, o_ref):
    a = _silu(c_ref[...]).astype(BF16)
    w = w_ref[0].astype(BF16)
    o_ref[0] = jnp.dot(a, w, preferred_element_type=F32) + b_ref[0]


def _adaln(cond, w_mod, b_mod):
    tn = 1024
    n = 3 * D_MODEL
    return pl.pallas_call(
        _adaln_kernel,
        grid=(DEPTH, n // tn),
        in_specs=[pl.BlockSpec((16, D_MODEL), lambda l, j: (0, 0)),
                  pl.BlockSpec((1, D_MODEL, tn), lambda l, j: (l, 0, j)),
                  pl.BlockSpec((1, 1, tn), lambda l, j: (l, 0, j))],
        out_specs=pl.BlockSpec((1, 16, tn), lambda l, j: (l, 0, j)),
        out_shape=jax.ShapeDtypeStruct((DEPTH, 16, n), F32),
        compiler_params=_cparams(2),
        name="adaln",
    )(cond, w_mod, b_mod.reshape(DEPTH, 1, n))


def _cast_kernel(w_ref, o_ref):
    o_ref[...] = w_ref[0].astype(o_ref.dtype)


def _cast_rows(w_t, layer, rows):
    tr = QK_TILE
    d = w_t.shape[2]
    return pl.pallas_call(
        _cast_kernel,
        grid=(rows // tr,),
        in_specs=[pl.BlockSpec((1, tr, d), lambda i: (layer, i, 0))],
        out_specs=pl.BlockSpec((tr, d), lambda i: (i, 0)),
        out_shape=jax.ShapeDtypeStruct((rows, d), BF16),
        compiler_params=_cparams(1),
        name="cast_rows",
    )(w_t)


def _main_chunk_kind(col):
    bounds = ((MAIN_KA, "q"), (MAIN_VA, "k"), (MAIN_ZA, "plain"), (MAIN_QB, "silu"), (MAIN_OB, "plain"),
              (MAIN_ZB, "sigmoid"), (MAIN_WIDTH, "silu"))
    return next(kind for end, kind in bounds if col < end)


def _inproj_ab_kernel(*refs, emit_kv, rope, sub):
    it = iter(refs)
    x_ref, mod_ref, g_ref, wb_ref, wg_ref = (next(it) for _ in range(5))
    cos_ref, sin_ref = (next(it), next(it)) if rope else (None, None)
    main_ref = next(it)
    k_ref, v_ref = (next(it), next(it)) if emit_kv else (None, None)
    gate_ref = next(it)

    mod = mod_ref[0, 0]
    g = g_ref[0]
    for r in range(x_ref.shape[0] // sub):
        rows = slice(r * sub, (r + 1) * sub)
        hn = _norm_modulate(x_ref[rows, :], g, mod).astype(BF16)
        gate_ref[:, rows] = lax.dot_general(wg_ref[0].astype(BF16), hn, _NT, preferred_element_type=F32)
        if emit_kv:
            kv_t = lax.dot_general(wb_ref[MAIN_KA:MAIN_ZA, :], hn, _NT, preferred_element_type=F32)
            k_ref[r] = kv_t[:A_KV_WIDTH, :]
            v_ref[r] = kv_t[A_KV_WIDTH:, :]
        for t in range(MAIN_WIDTH // QK_TILE):
            res = lax.dot_general(hn, wb_ref[t * QK_TILE:(t + 1) * QK_TILE, :], _NT, preferred_element_type=F32)
            for c in range(QK_TILE // LANES):
                col = t * QK_TILE + c * LANES
                kind = _main_chunk_kind(col)
                chunk = res[:, c * LANES:(c + 1) * LANES]
                if kind in ("q", "k"):
                    if rope:
                        chunk = _rope(chunk, cos_ref[rows, :], sin_ref[rows, :], A_HEAD_DIM // 4)
                    if kind == "q":
                        chunk = chunk * A_QSCALE
                elif kind == "silu":
                    chunk = _silu(chunk)
                elif kind == "sigmoid":
                    chunk = jax.nn.sigmoid(chunk)
                main_ref[rows, col:col + LANES] = chunk.astype(BF16)


def _inproj_ab(x, mods, norm_g, depth_idx, w_bf, w_t, layer, seq, shared_mod, emit_kv, tables):
    m = x.shape[0]
    tm = 512
    rope = tables is not None
    if shared_mod:
        mod_map = lambda i: (depth_idx, 0, 0, 0)
    else:
        mod_map = lambda i: (depth_idx, 1 + (i * tm) // seq, 0, 0)
    n_gate = 4 * B_HEADS
    tiles_per_table = max(seq, tm) // tm
    table_spec = pl.BlockSpec((tm, LANES), lambda i: (i % tiles_per_table, 0))
    in_specs = [pl.BlockSpec((tm, D_MODEL), lambda i: (i, 0)),
                pl.BlockSpec((1, 1, 3, D_MODEL), mod_map),
                pl.BlockSpec((1, 1, D_MODEL), lambda i: (depth_idx, 0, 0)),
                pl.BlockSpec((MAIN_WIDTH, D_MODEL), lambda i: (0, 0), pipeline_mode=pl.Buffered(1)),
                pl.BlockSpec((1, n_gate, D_MODEL), lambda i: (layer, MAIN_WIDTH // n_gate, 0))]
    args = [x, mods, norm_g, w_bf, w_t]
    if rope:
        in_specs += [table_spec] * 2
        args += list(tables)
    out_specs = [pl.BlockSpec((tm, MAIN_WIDTH), lambda i: (i, 0))]
    out_shape = [jax.ShapeDtypeStruct((m, MAIN_WIDTH), BF16)]
    if emit_kv:
        out_specs += [pl.BlockSpec((tm // seq, A_KV_WIDTH, seq), lambda i: (i, 0, 0))] * 2
        out_shape += [jax.ShapeDtypeStruct((m // seq, A_KV_WIDTH, seq), F32)] * 2
    out_specs.append(pl.BlockSpec((n_gate, tm), lambda i: (0, i)))
    out_shape.append(jax.ShapeDtypeStruct((n_gate, m), F32))
    return pl.pallas_call(
        functools.partial(_inproj_ab_kernel, emit_kv=emit_kv, rope=rope, sub=seq if emit_kv else tm // 2),
        grid=(m // tm,),
        in_specs=in_specs,
        out_specs=out_specs,
        out_shape=out_shape,
        compiler_params=_cparams(1, VMEM_LIMIT_WIDE),
        name="inproj_ab",
    )(*args)


def _value_slabs(v_ref, slab_ref):
    keys = v_ref.shape[0]
    ones = jnp.ones((keys, LANES - A_HEAD_DIM), BF16)
    for g in range(A_KV_HEADS):
        slab_ref[:, g * LANES:g * LANES + A_HEAD_DIM] = v_ref[:, g * A_HEAD_DIM:(g + 1) * A_HEAD_DIM]
        slab_ref[:, g * LANES + A_HEAD_DIM:(g + 1) * LANES] = ones


def _group_queries(q_ref, g):
    return jnp.concatenate([q_ref[:, (g * A_GROUP + hh) * A_HEAD_DIM:(g * A_GROUP + hh + 1) * A_HEAD_DIM]
                            for hh in range(A_GROUP)], axis=0)


def _sink_softmax(s_ref, e_ref, t_ref, sink_ref):
    s = s_ref[...]
    sink = sink_ref[...] * LOG2E
    m = jnp.maximum(jnp.max(s, axis=-1, keepdims=True), sink)
    e_ref[...] = jnp.exp2(s - m).astype(BF16)
    t_ref[...] = jnp.exp2(sink - m)


def _attn_a_prompt_kernel(sink_ref, q_ref, k_ref, v_ref, o_ref, s_ref, e_ref, t_ref, vs_ref):
    _value_slabs(v_ref, vs_ref)
    seq = q_ref.shape[0]
    rows = A_GROUP * seq
    for g in range(A_KV_HEADS):
        heads = slice(g * A_GROUP, (g + 1) * A_GROUP)
        ks = slice(g * A_HEAD_DIM, (g + 1) * A_HEAD_DIM)
        s = lax.dot_general(_group_queries(q_ref, g), k_ref[:, ks], _NT, preferred_element_type=F32)
        s_ref[heads] = s.reshape(A_GROUP, seq, seq)
    _sink_softmax(s_ref, e_ref, t_ref, sink_ref)
    for g in range(A_KV_HEADS):
        heads = slice(g * A_GROUP, (g + 1) * A_GROUP)
        e = e_ref[heads].reshape(rows, seq)
        res = jnp.dot(e, vs_ref[:, g * LANES:(g + 1) * LANES], preferred_element_type=F32)
        den = jnp.dot(e, jnp.ones((seq, LANES), BF16), preferred_element_type=F32)
        o = res / (den + t_ref[heads].reshape(rows, 1))
        for hh in range(A_GROUP):
            h = g * A_GROUP + hh
            o_ref[:, h * A_HEAD_DIM:(h + 1) * A_HEAD_DIM] = (
                o[hh * seq:(hh + 1) * seq, :A_HEAD_DIM].astype(o_ref.dtype))


def _attn_a_prompt(main, sink, batch, seq):
    kb = MAIN_KA // A_KV_WIDTH
    vb = MAIN_VA // A_KV_WIDTH
    return pl.pallas_call(
        _attn_a_prompt_kernel,
        grid=(batch,),
        in_specs=[pl.BlockSpec((A_HEADS, 1, 1), lambda b: (0, 0, 0)),
                  pl.BlockSpec((seq, A_WIDTH), lambda b: (b, MAIN_QA // A_WIDTH)),
                  pl.BlockSpec((seq, A_KV_WIDTH), lambda b: (b, kb)),
                  pl.BlockSpec((seq, A_KV_WIDTH), lambda b: (b, vb))],
        out_specs=pl.BlockSpec((seq, A_WIDTH), lambda b: (b, 0)),
        out_shape=jax.ShapeDtypeStruct((batch * seq, A_WIDTH), BF16),
        scratch_shapes=[pltpu.VMEM((A_HEADS, seq, seq), F32),
                        pltpu.VMEM((A_HEADS, seq, seq), BF16),
                        pltpu.VMEM((A_HEADS, seq, 1), F32),
                        pltpu.VMEM((seq, A_KV_HEADS * LANES), BF16)],
        compiler_params=_cparams(1),
        name="attn_a_prompt",
    )(sink.reshape(A_HEADS, 1, 1), main, main, main)


def _attn_a_sample_kernel(sink_ref, q_ref, k_ref, v_ref, ck_ref, cv_ref, bias_ref, o_ref, vs_ref, cvs_ref):
    i = pl.program_id(1)
    seq = k_ref.shape[0]
    span = 3 * BLOCK

    @pl.when(i == 0)
    def _():
        _value_slabs(v_ref, vs_ref)
        _value_slabs(cv_ref.at[0], cvs_ref)

    ck = ck_ref[0]
    rows = A_GROUP * BLOCK
    head_of_row = lax.broadcasted_iota(jnp.int32, (rows, 1), 0) // BLOCK
    for blk in range(Q_BLOCKS):
        qi = i * Q_BLOCKS + blk
        qrows = slice(blk * BLOCK, (blk + 1) * BLOCK)
        start = pl.multiple_of(jnp.clip((qi - 1) * BLOCK, 0, seq - span), BLOCK)
        kw = k_ref[pl.ds(start, span), :]
        vw = vs_ref[pl.ds(start, span), :]
        bias = bias_ref[(qi * BLOCK - start) // BLOCK]
        bias = jnp.concatenate([bias] * A_GROUP, axis=0)
        for g in range(A_KV_HEADS):
            ks = slice(g * A_HEAD_DIM, (g + 1) * A_HEAD_DIM)
            gs = slice(g * LANES, (g + 1) * LANES)
            sink = jnp.zeros((rows, 1), F32)
            for hh in range(A_GROUP):
                sink = jnp.where(head_of_row == hh, sink_ref[g * A_GROUP + hh] * LOG2E, sink)
            qg = _group_queries(q_ref.at[qrows], g)
            s_loc = lax.dot_general(qg, kw[:, ks], _NT, preferred_element_type=F32) + bias
            s_ctx = lax.dot_general(qg, ck[:, ks], _NT, preferred_element_type=F32)
            o = _softmax_pv([s_loc, s_ctx], [vw[:, gs], cvs_ref[:, gs]], sink)
            for hh in range(A_GROUP):
                h = g * A_GROUP + hh
                o_ref[qrows, h * A_HEAD_DIM:(h + 1) * A_HEAD_DIM] = (
                    o[hh * BLOCK:(hh + 1) * BLOCK, :A_HEAD_DIM].astype(o_ref.dtype))


def _window_bias():
    r = np.arange(BLOCK)[:, None]
    c = np.arange(3 * BLOCK)[None, :]
    masks = [np.where(np.abs(off + r - c) <= WINDOW, 0.0, NEG_INF) for off in (0, BLOCK, 2 * BLOCK)]
    return jnp.asarray(np.stack(masks), dtype=F32)


def _attn_a_sample(main, sink, ck, cv, batch, seq):
    tq = BLOCK * Q_BLOCKS
    nb = seq // tq
    kb = MAIN_KA // A_KV_WIDTH
    vb = MAIN_VA // A_KV_WIDTH
    ctx = ck.shape[1]
    slab_w = A_KV_HEADS * LANES
    return pl.pallas_call(
        _attn_a_sample_kernel,
        grid=(batch, nb),
        in_specs=[pl.BlockSpec(memory_space=pltpu.SMEM),
                  pl.BlockSpec((tq, A_WIDTH), lambda b, i: (b * nb + i, MAIN_QA // A_WIDTH)),
                  pl.BlockSpec((seq, A_KV_WIDTH), lambda b, i: (b, kb)),
                  pl.BlockSpec((seq, A_KV_WIDTH), lambda b, i: (b, vb)),
                  pl.BlockSpec((1, ctx, A_KV_WIDTH), lambda b, i: (b, 0, 0)),
                  pl.BlockSpec((1, ctx, A_KV_WIDTH), lambda b, i: (b, 0, 0)),
                  pl.BlockSpec((3, BLOCK, 3 * BLOCK), lambda b, i: (0, 0, 0))],
        out_specs=pl.BlockSpec((tq, A_WIDTH), lambda b, i: (b * nb + i, 0)),
        out_shape=jax.ShapeDtypeStruct((batch * seq, A_WIDTH), BF16),
        scratch_shapes=[pltpu.VMEM((seq, slab_w), BF16),
                        pltpu.VMEM((ctx, slab_w), BF16)],
        compiler_params=_cparams(2),
        name="attn_a_sample",
    )(sink, main, main, main, ck, cv, _window_bias())


def _mlstm_kernel(*refs, seq, chunk, layer, hps, has_init, emit_state, has_prev):
    it = iter(refs)
    bias_ref = next(it)
    m0_ref = next(it) if has_init else None
    take = lambda: [next(it) for _ in range(hps)]
    q_refs, k_refs, v_refs, o_refs = take(), take(), take(), take()
    g_ref = next(it)
    cwq_refs, cwk_refs = take(), take()
    nw_ref = next(it)
    c0_ref, n0_ref = (next(it), next(it)) if has_init else (None, None)
    cprev_ref, nprev_ref = (next(it), next(it)) if has_prev else (None, None)
    h_ref = next(it)
    cst_out, nst_out, mst_out = (next(it), next(it), next(it)) if emit_state else (None, None, None)
    qs_ref, ks_ref, hf_ref, hb_ref, cst_ref, nst_ref = (next(it) for _ in range(6))

    b = pl.program_id(0)
    head0 = pl.program_id(1) * hps
    nc = seq // chunk
    hd = B_HEAD_DIM

    edge = lax.broadcasted_iota(jnp.int32, (SUBLANES, 1), 0)
    drop_first = (edge != 0).astype(F32)
    drop_last = (edge != SUBLANES - 1).astype(F32)
    rr = lax.broadcasted_iota(jnp.int32, (chunk, chunk), 0)
    cc = lax.broadcasted_iota(jnp.int32, (chunk, chunk), 1)
    diag = rr == cc

    def conv_silu(x_ref, w_ref, scale):
        x = x_ref[...].astype(F32)
        w = w_ref[0]
        if nc == 1:
            up = jnp.dot((cc == rr - 1).astype(BF16), x_ref[...], preferred_element_type=F32)
            dn = jnp.dot((cc == rr + 1).astype(BF16), x_ref[...], preferred_element_type=F32)
        else:
            x_up = jnp.concatenate([x[:seq - SUBLANES], x[seq - SUBLANES:] * drop_last], axis=0)
            x_dn = jnp.concatenate([x[:SUBLANES] * drop_first, x[SUBLANES:]], axis=0)
            up = pltpu.roll(x_up, 1, 0)
            dn = pltpu.roll(x_dn, seq - 1, 0)
        y = up * w[0:1] + x * w[1:2] + dn * w[2:3]
        return (_silu(y) * scale).astype(BF16)

    for hh in range(hps):
        h = head0 + hh
        v_ref = v_refs[hh]
        qs_ref[hh] = conv_silu(q_refs[hh], cwq_refs[hh], 1.0)
        ks_ref[hh] = conv_silu(k_refs[hh], cwk_refs[hh], hd ** -0.5)

        def gate_row(kind):
            ch = kind * B_HEADS + h
            return g_ref[pl.ds(ch, 1), :] + bias_ref[layer, ch]

        li = [gate_row(2 * d) for d in range(2)]
        lf = [_log_sigmoid(gate_row(2 * d + 1)) for d in range(2)]

        def chunk_step(c, d, m_prev, first):
            rows = pl.ds(c * chunk, chunk)
            lanes = slice(c * chunk, (c + 1) * chunk)
            qc = qs_ref[hh, rows, :]
            kc = ks_ref[hh, rows, :]
            vc = v_ref[rows, :]
            li_row = li[d][:, lanes]
            lf_row = lf[d][:, lanes]
            causal = (cc <= rr) if d == 0 else (cc >= rr)
            b_col = jnp.sum(jnp.where(causal, lf_row, 0.0), axis=1, keepdims=True)
            b_row = jnp.sum(jnp.where(diag, b_col, 0.0), axis=0, keepdims=True)
            a_row = li_row - b_row
            a_col = jnp.sum(jnp.where(diag, a_row, 0.0), axis=1, keepdims=True)
            total = jnp.sum(lf_row, axis=1, keepdims=True)
            log_d = jnp.where(causal, b_col + a_row, NEG_INF)
            log_init = b_col + m_prev
            m_t = jnp.maximum(log_init, jnp.max(log_d, axis=1, keepdims=True))
            d_mat = jnp.exp(log_d - m_t)
            s = lax.dot_general(qc, kc, _NT, preferred_element_type=F32) * d_mat
            num = jnp.dot(s.astype(BF16), vc, preferred_element_type=F32)
            den = jnp.sum(s, axis=1, keepdims=True)
            if not first:
                w_init = jnp.exp(log_init - m_t)
                num = num + w_init * jnp.dot(qc, cst_ref[hh, d].astype(BF16), preferred_element_type=F32)
                n_rows = jnp.broadcast_to(nst_ref[hh, d], (SUBLANES, hd)).astype(BF16)
                den = den + w_init * lax.dot_general(qc, n_rows, _NT, preferred_element_type=F32)[:, 0:1]
            hc = num / jnp.maximum(jnp.abs(den), jnp.exp(-m_t))
            (hf_ref if d == 0 else hb_ref)[hh, rows, :] = hc
            log_w = total + a_col
            m_new = jnp.maximum(total + m_prev, jnp.max(log_w, axis=0, keepdims=True))
            kw = kc.astype(F32) * jnp.exp(log_w - m_new)
            c_add = lax.dot_general(kw.astype(BF16), vc, _TN, preferred_element_type=F32)
            n_add = jnp.sum(kw, axis=0, keepdims=True)
            if first:
                cst_ref[hh, d] = c_add
                nst_ref[hh, d] = n_add
            else:
                w_0 = jnp.exp(total + m_prev - m_new)
                cst_ref[hh, d] = w_0 * cst_ref[hh, d] + c_add
                nst_ref[hh, d] = w_0 * nst_ref[hh, d] + n_add
            return m_new

        if has_init:
            for d in range(2):
                cst_ref[hh, d] = c0_ref[0, 0, d, hh]
                nst_ref[hh, d] = n0_ref[0, 0, d, hh]
            m = [jnp.full((1, 1), m0_ref[b, layer, d, h], F32) for d in range(2)]
        else:
            m = [jnp.zeros((1, 1), F32) for _ in range(2)]
        for step in range(nc):
            first = (step == 0) and not has_init
            m[0] = chunk_step(step, 0, m[0], first)
            m[1] = chunk_step(nc - 1 - step, 1, m[1], first)

        hsl = slice(hh * hd, (hh + 1) * hd)
        hhat = _rms(o_refs[hh][...].astype(F32) * (hf_ref[hh] + hb_ref[hh]))
        h_ref[:, hsl] = (hhat * nw_ref[0, :, hsl]).astype(h_ref.dtype)

        if emit_state:
            row = lax.broadcasted_iota(jnp.int32, (8, LANES), 0)
            mst_out[0, hh] = jnp.where(row == 0, m[0], m[1])
            if has_prev:
                for d in range(2):
                    cst_out[0, 0, d, hh] = cprev_ref[0, d, hh]
                    nst_out[0, 0, d, hh] = nprev_ref[0, d, hh]
                    cst_out[0, 1, d, hh] = cst_ref[hh, d]
                    nst_out[0, 1, d, hh] = nst_ref[hh, d]
            else:
                for d in range(2):
                    cst_out[0, d, hh] = cst_ref[hh, d]
                    nst_out[0, d, hh] = nst_ref[hh, d]


def _mlstm(main, gates, conv_w, gate_bias, norm_w, init, layer, batch, seq, emit_state, prev, hps):
    chunk = B_CHUNK
    hd = B_HEAD_DIM
    has_init = init is not None
    has_prev = prev is not None

    def head_cols(off):
        return [pl.BlockSpec((seq, hd), lambda b, g, hh=hh: (b, off // hd + g * hps + hh)) for hh in range(hps)]

    def conv_cols(off):
        return [pl.BlockSpec((1, B_CONV, hd), lambda b, g, hh=hh: (layer, 0, off + g * hps + hh))
                for hh in range(hps)]

    in_specs = [pl.BlockSpec(memory_space=pltpu.SMEM)]
    args = [gate_bias]
    if has_init:
        c0, n0, m0 = init
        in_specs.append(pl.BlockSpec(memory_space=pltpu.SMEM))
        args.append(m0)
    in_specs += (head_cols(MAIN_QB) + head_cols(MAIN_KB) + head_cols(MAIN_VB) + head_cols(MAIN_OB)
                 + [pl.BlockSpec((4 * B_HEADS, seq), lambda b, g: (0, b))]
                 + conv_cols(0) + conv_cols(B_HEADS)
                 + [pl.BlockSpec((1, 1, hps * hd), lambda b, g: (layer, 0, g))])
    args += [main] * (4 * hps) + [gates] + [conv_w] * (2 * hps) + [norm_w.reshape(N_EVEN, 1, B_WIDTH)]
    if has_init:
        in_specs += [pl.BlockSpec((1, 1, 2, hps, hd, hd), lambda b, g: (b, layer, 0, g, 0, 0)),
                     pl.BlockSpec((1, 1, 2, hps, 1, hd), lambda b, g: (b, layer, 0, g, 0, 0))]
        args += [c0, n0.reshape(n0.shape[:4] + (1, hd))]
    if has_prev:
        in_specs += [pl.BlockSpec((1, 2, hps, hd, hd), lambda b, g: (b, 0, g, 0, 0)),
                     pl.BlockSpec((1, 2, hps, 1, hd), lambda b, g: (b, 0, g, 0, 0))]
        args += list(prev)
    out_specs = [pl.BlockSpec((seq, hps * hd), lambda b, g: (b, g))]
    out_shape = [jax.ShapeDtypeStruct((batch * seq, B_WIDTH), BF16)]
    if emit_state:
        if has_prev:
            out_specs += [pl.BlockSpec((1, N_EVEN, 2, hps, hd, hd), lambda b, g: (b, 0, 0, g, 0, 0)),
                          pl.BlockSpec((1, N_EVEN, 2, hps, 1, hd), lambda b, g: (b, 0, 0, g, 0, 0))]
            out_shape += [jax.ShapeDtypeStruct((batch, N_EVEN, 2, B_HEADS, hd, hd), F32),
                          jax.ShapeDtypeStruct((batch, N_EVEN, 2, B_HEADS, 1, hd), F32)]
        else:
            out_specs += [pl.BlockSpec((1, 2, hps, hd, hd), lambda b, g: (b, 0, g, 0, 0)),
                          pl.BlockSpec((1, 2, hps, 1, hd), lambda b, g: (b, 0, g, 0, 0))]
            out_shape += [jax.ShapeDtypeStruct((batch, 2, B_HEADS, hd, hd), F32),
                          jax.ShapeDtypeStruct((batch, 2, B_HEADS, 1, hd), F32)]
        out_specs.append(pl.BlockSpec((1, hps, 8, LANES), lambda b, g: (b, g, 0, 0)))
        out_shape.append(jax.ShapeDtypeStruct((batch, B_HEADS, 8, LANES), F32))
    outs = pl.pallas_call(
        functools.partial(_mlstm_kernel, seq=seq, chunk=chunk, layer=layer, hps=hps, has_init=has_init,
                          emit_state=emit_state, has_prev=has_prev),
        grid=(batch, B_HEADS // hps),
        in_specs=in_specs,
        out_specs=out_specs,
        out_shape=out_shape,
        scratch_shapes=[pltpu.VMEM((hps, seq, hd), BF16), pltpu.VMEM((hps, seq, hd), BF16),
                        pltpu.VMEM((hps, seq, hd), F32), pltpu.VMEM((hps, seq, hd), F32),
                        pltpu.VMEM((hps, 2, hd, hd), F32), pltpu.VMEM((hps, 2, 1, hd), F32)],
        compiler_params=_cparams(2),
        name="mlstm",
    )(*args)
    if not emit_state:
        return outs[0], None
    hb, cst, nst, mst = outs
    mstate = mst[:, :, 0:2, 0].transpose(0, 2, 1)
    return hb, (cst, nst, mstate)


def _outproj_kernel(*refs, n_branch, final):
    it = iter(refs)
    branches = [(next(it), next(it), next(it)) for _ in range(n_branch)]
    w_ref, x_ref, mod_ref = next(it), next(it), next(it)
    fn_ref = next(it) if final else None
    y_ref, wb_ref = next(it), next(it)

    @pl.when(pl.program_id(0) == 0)
    def _():
        wb_ref[...] = w_ref[0].astype(BF16)

    out = None
    for n, (a_ref, zlo_ref, zhi_ref) in enumerate(branches):
        base = n * 2 * Z_BLOCK
        for part, z_ref in enumerate((zlo_ref, zhi_ref)):
            cols = slice(part * Z_BLOCK, (part + 1) * Z_BLOCK)
            gated = a_ref[:, cols] * z_ref[...]
            w = wb_ref[base + part * Z_BLOCK:base + (part + 1) * Z_BLOCK, :]
            p = jnp.dot(gated, w, preferred_element_type=F32)
            out = p if out is None else out + p
    y = x_ref[...] + mod_ref[0, 0, 2:3, :] * out
    if final:
        y = _rms(y) * fn_ref[...]
    y_ref[...] = y


def _outproj(branches, w_out, layer, x, mods, depth_idx, rows_per_mod, final_norm):
    m = x.shape[0]
    tm = 1024
    if rows_per_mod is None:
        mod_map = lambda i: (depth_idx, 0, 0, 0)
    else:
        mod_map = lambda i: (depth_idx, 1 + (i * tm) // rows_per_mod, 0, 0)
    in_specs, args = [], []
    for a, z, z_off in branches:
        zb = z_off // Z_BLOCK
        in_specs += [pl.BlockSpec((tm, 2 * Z_BLOCK), lambda i: (i, 0)),
                     pl.BlockSpec((tm, Z_BLOCK), lambda i, zb=zb: (i, zb)),
                     pl.BlockSpec((tm, Z_BLOCK), lambda i, zb=zb: (i, zb + 1))]
        args += [a, z, z]
    wk = w_out.shape[1]
    in_specs += [pl.BlockSpec((1, wk, D_MODEL), lambda i: (layer, 0, 0), pipeline_mode=pl.Buffered(1)),
                 pl.BlockSpec((tm, D_MODEL), lambda i: (i, 0)),
                 pl.BlockSpec((1, 1, 3, D_MODEL), mod_map)]
    args += [w_out, x, mods]
    final = final_norm is not None
    if final:
        in_specs.append(pl.BlockSpec((1, D_MODEL), lambda i: (0, 0)))
        args.append(final_norm.reshape(1, D_MODEL))
    return pl.pallas_call(
        functools.partial(_outproj_kernel, n_branch=len(branches), final=final),
        grid=(m // tm,),
        in_specs=in_specs,
        out_specs=pl.BlockSpec((tm, D_MODEL), lambda i: (i, 0)),
        out_shape=jax.ShapeDtypeStruct((m, D_MODEL), F32),
        scratch_shapes=[pltpu.VMEM((wk, D_MODEL), BF16)],
        compiler_params=_cparams(1, VMEM_LIMIT_WIDE),
        name="outproj",
    )(*args)


def _mla_slabs(kv, kr_slab):
    lane = lax.broadcasted_iota(jnp.int32, kv.shape, 1)
    nope = lane < C_NOPE
    keys = jnp.where(nope, kv, kr_slab).astype(BF16)
    values = jnp.where(nope, 1.0, kv).astype(BF16)
    return keys, values


def _inproj_c_kernel(*refs, rope, sub):
    it = iter(refs)
    x_ref, mod_ref, g_ref, w_ref, qn_ref, wqb_ref, kvn_ref, wkvb_ref = (next(it) for _ in range(8))
    cos_ref, sin_ref = (next(it), next(it)) if rope else (None, None)
    q_ref, ks_ref, vs_ref, z_ref, ckv_ref, kr_ref, wb_ref = (next(it) for _ in range(7))

    @pl.when(pl.program_id(0) == 0)
    def _():
        wb_ref[...] = w_ref[0].astype(BF16)

    for r in range(x_ref.shape[0] // sub):
        rows = slice(r * sub, (r + 1) * sub)
        hn = _norm_modulate(x_ref[rows, :], g_ref[0], mod_ref[0, 0]).astype(BF16)

        def project(lo, hi):
            return lax.dot_general(hn, wb_ref[lo:hi, :], _NT, preferred_element_type=F32)

        qa = _rms(project(CIN_QA, CIN_KVA)) * qn_ref[...]
        ckv = _rms(project(CIN_KVA, CIN_KR)) * kvn_ref[...]
        z_ref[rows, :] = _silu(project(CIN_Z, CIN_WIDTH)).astype(BF16)
        kr = jnp.concatenate([jnp.zeros((sub, C_NOPE), F32), project(CIN_KR, CIN_Z),
                              jnp.zeros((sub, C_HEAD_PAD - C_NOPE - C_ROPE), F32)], axis=1)
        ckv_ref[rows, :] = ckv
        kr_ref[rows, :] = kr
        kv = jnp.dot(ckv.astype(BF16), wkvb_ref[...], preferred_element_type=F32)
        scale = (C_NOPE + C_ROPE) ** -0.5 * LOG2E
        q = jnp.dot(qa.astype(BF16), wqb_ref[...], preferred_element_type=F32) * scale
        half = C_ROPE // 4
        if rope:
            cos = cos_ref[rows, :]
            sin = sin_ref[rows, :]
            kr = _rope(kr, cos, sin, half)
            for hd in range(C_HEADS):
                hs = slice(hd * C_HEAD_PAD, (hd + 1) * C_HEAD_PAD)
                q_ref[rows, hs] = _rope(q[:, hs], cos, sin, half).astype(BF16)
        else:
            q_ref[rows, :] = q.astype(BF16)
        for hd in range(C_HEADS):
            hs = slice(hd * C_HEAD_PAD, (hd + 1) * C_HEAD_PAD)
            ks_ref[rows, hs], vs_ref[rows, hs] = _mla_slabs(kv[:, hs], kr)


def _inproj_c(x, mods, norm_g, depth_idx, w_t, layer, q_norm, w_qb, kv_norm, w_kvb, rows_per_mod, tables):
    m = x.shape[0]
    tm = 512
    rope = tables is not None
    if rows_per_mod is None:
        mod_map = lambda i: (depth_idx, 0, 0, 0)
    else:
        mod_map = lambda i: (depth_idx, 1 + (i * tm) // rows_per_mod, 0, 0)
    const = lambda i: (0, 0)
    qw = C_HEADS * C_HEAD_PAD
    kvw = C_HEADS * (C_NOPE + C_VDIM)
    in_specs = [pl.BlockSpec((tm, D_MODEL), lambda i: (i, 0)),
                pl.BlockSpec((1, 1, 3, D_MODEL), mod_map),
                pl.BlockSpec((1, 1, D_MODEL), lambda i: (depth_idx, 0, 0)),
                pl.BlockSpec((1, CIN_WIDTH, D_MODEL), lambda i: (layer, 0, 0), pipeline_mode=pl.Buffered(1)),
                pl.BlockSpec((1, C_Q_RANK), const),
                pl.BlockSpec((C_Q_RANK, qw), const),
                pl.BlockSpec((1, C_KV_RANK), const),
                pl.BlockSpec((C_KV_RANK, kvw), const)]
    args = [x, mods, norm_g, w_t, q_norm, w_qb, kv_norm, w_kvb]
    if rope:
        tpb = rows_per_mod // tm
        in_specs += [pl.BlockSpec((tm, LANES), lambda i: (i % tpb, 0))] * 2
        args += list(tables)
    return pl.pallas_call(
        functools.partial(_inproj_c_kernel, rope=rope, sub=tm // 2),
        grid=(m // tm,),
        in_specs=in_specs,
        out_specs=[pl.BlockSpec((tm, qw), lambda i: (i, 0)),
                   pl.BlockSpec((tm, kvw), lambda i: (i, 0)),
                   pl.BlockSpec((tm, kvw), lambda i: (i, 0)),
                   pl.BlockSpec((tm, C_WIDTH), lambda i: (i, 0)),
                   pl.BlockSpec((tm, C_KV_RANK), lambda i: (i, 0)),
                   pl.BlockSpec((tm, LANES), lambda i: (i, 0))],
        out_shape=[jax.ShapeDtypeStruct((m, qw), BF16),
                   jax.ShapeDtypeStruct((m, kvw), BF16),
                   jax.ShapeDtypeStruct((m, kvw), BF16),
                   jax.ShapeDtypeStruct((m, C_WIDTH), BF16),
                   jax.ShapeDtypeStruct((m, C_KV_RANK), F32),
                   jax.ShapeDtypeStruct((m, LANES), F32)],
        scratch_shapes=[pltpu.VMEM((CIN_WIDTH, D_MODEL), BF16)],
        compiler_params=_cparams(1),
        name="inproj_c",
    )(*args)


def _matmul_kernel(x_ref, w_ref, o_ref):
    o_ref[...] = jnp.dot(x_ref[...], w_ref[...], preferred_element_type=F32).astype(o_ref.dtype)


def _matmul(x, w, tm):
    m, k = x.shape
    n = w.shape[1]
    return pl.pallas_call(
        _matmul_kernel,
        grid=(m // tm,),
        in_specs=[pl.BlockSpec((tm, k), lambda i: (i, 0)),
                  pl.BlockSpec((k, n), lambda i: (0, 0))],
        out_specs=pl.BlockSpec((tm, n), lambda i: (i, 0)),
        out_shape=jax.ShapeDtypeStruct((m, n), BF16),
        compiler_params=_cparams(1),
        name="matmul",
    )(x, w)


def _attn_c_kernel(*refs, has_ctx):
    it = iter(refs)
    q_ref, kown_ref, vown_ref = next(it), next(it), next(it)
    kvc_ref, krc_ref = (next(it), next(it)) if has_ctx else (None, None)
    o_ref = next(it)
    kctx_ref, vctx_ref = (next(it), next(it)) if has_ctx else (None, None)

    if has_ctx:
        @pl.when(pl.program_id(1) == 0)
        def _():
            for h in range(C_HEADS):
                hs = slice(h * C_HEAD_PAD, (h + 1) * C_HEAD_PAD)
                kctx_ref[:, hs], vctx_ref[:, hs] = _mla_slabs(kvc_ref[:, hs], krc_ref[...])

    for h in range(C_HEADS):
        hs = slice(h * C_HEAD_PAD, (h + 1) * C_HEAD_PAD)
        qh = q_ref[:, hs]
        scores = [lax.dot_general(qh, kown_ref[:, hs], _NT, preferred_element_type=F32)]
        values = [vown_ref[:, hs]]
        if has_ctx:
            scores.append(lax.dot_general(qh, kctx_ref[:, hs], _NT, preferred_element_type=F32))
            values.append(vctx_ref[:, hs])
        o = _softmax_pv(scores, values, None)
        o_ref[:, h * C_VDIM:(h + 1) * C_VDIM] = o[:, C_NOPE:].astype(o_ref.dtype)


def _attn_c_dense_kernel(q_ref, ks_ref, vs_ref, o_ref, s_ref, e_ref):
    for h in range(C_HEADS):
        hs = slice(h * C_HEAD_PAD, (h + 1) * C_HEAD_PAD)
        s_ref[h] = lax.dot_general(q_ref[:, hs], ks_ref[:, hs], _NT, preferred_element_type=F32)
    s = s_ref[...]
    e_ref[...] = jnp.exp2(s - jnp.max(s, axis=-1, keepdims=True)).astype(BF16)
    for h in range(C_HEADS):
        hs = slice(h * C_HEAD_PAD, (h + 1) * C_HEAD_PAD)
        res = jnp.dot(e_ref[h], vs_ref[:, hs], preferred_element_type=F32)
        o = res / pltpu.roll(res, LANES // 2, 1)
        o_ref[:, h * C_VDIM:(h + 1) * C_VDIM] = o[:, C_NOPE:].astype(o_ref.dtype)


def _attn_c_dense(q, ks, vs, batch, seq):
    w = C_HEADS * C_HEAD_PAD
    return pl.pallas_call(
        _attn_c_dense_kernel,
        grid=(batch,),
        in_specs=[pl.BlockSpec((seq, w), lambda b: (b, 0))] * 3,
        out_specs=pl.BlockSpec((seq, C_WIDTH), lambda b: (b, 0)),
        out_shape=jax.ShapeDtypeStruct((batch * seq, C_WIDTH), BF16),
        scratch_shapes=[pltpu.VMEM((C_HEADS, seq, seq), F32), pltpu.VMEM((C_HEADS, seq, seq), BF16)],
        compiler_params=_cparams(1),
        name="attn_c_dense",
    )(q, ks, vs)


def _attn_c(q, ks, vs, ctx, batch, seq, tq):
    nq = seq // tq
    w = C_HEADS * C_HEAD_PAD
    has_ctx = ctx is not None
    in_specs = [pl.BlockSpec((tq, w), lambda b, i: (b * nq + i, 0)),
                pl.BlockSpec((seq, w), lambda b, i: (b, 0)),
                pl.BlockSpec((seq, w), lambda b, i: (b, 0))]
    args = [q, ks, vs]
    scratch = []
    if has_ctx:
        kv_ctx, kr_ctx = ctx
        nctx = kv_ctx.shape[0] // batch
        in_specs += [pl.BlockSpec((nctx, w), lambda b, i: (b, 0)),
                     pl.BlockSpec((nctx, LANES), lambda b, i: (b, 0))]
        args += [kv_ctx, kr_ctx]
        scratch += [pltpu.VMEM((nctx, w), BF16), pltpu.VMEM((nctx, w), BF16)]
    return pl.pallas_call(
        functools.partial(_attn_c_kernel, has_ctx=has_ctx),
        grid=(batch, nq),
        in_specs=in_specs,
        out_specs=pl.BlockSpec((tq, C_WIDTH), lambda b, i: (b * nq + i, 0)),
        out_shape=jax.ShapeDtypeStruct((batch * seq, C_WIDTH), BF16),
        scratch_shapes=scratch,
        compiler_params=_cparams(2),
        name="attn_c",
    )(*args)


def _rope_tables(n_tokens):
    pos_r = np.repeat(np.arange(n_tokens // GRID_W), GRID_W).astype(np.float64)
    pos_c = np.tile(np.arange(GRID_W), n_tokens // GRID_W).astype(np.float64)

    def seg(d_axis):
        half = d_axis // 2
        freqs = np.power(ROPE_BASE, -np.arange(half, dtype=np.float64) / half)
        cos, sin = [], []
        for pos in (pos_r, pos_c):
            ang = pos[:, None] * freqs[None, :]
            cos += [np.cos(ang), np.cos(ang)]
            sin += [-np.sin(ang), np.sin(ang)]
        return np.concatenate(cos, axis=1), np.concatenate(sin, axis=1)

    cos_a, sin_a = seg(A_HEAD_DIM // 2)
    cos_a, sin_a = np.tile(cos_a, (1, 2)), np.tile(sin_a, (1, 2))
    cos_r, sin_r = seg(C_ROPE // 2)
    ones = np.ones((n_tokens, C_NOPE))
    pad = C_HEAD_PAD - C_NOPE - C_ROPE
    cos_c = np.concatenate([ones, cos_r, np.ones((n_tokens, pad))], axis=1)
    sin_c = np.concatenate([0 * ones, sin_r, np.zeros((n_tokens, pad))], axis=1)
    f = lambda a: jnp.asarray(a, dtype=F32)
    return (f(cos_a), f(sin_a)), (f(cos_c), f(sin_c))


def _pad_cols(w, left, total):
    return jnp.pad(w, ((0, 0), (left, total - left - w.shape[1])))


def kernel(x_prompt, x_sample, cache_a_k, cache_a_v, state_b_mem, state_b_norm, state_b_max, cache_c_kv, cache_c_krope, c, c_ctx, norm_g, w_mod, b_mod, w_in_ab, sink_a, conv_b, gate_bias_b, norm_b, w_out_ab, w_in_c, q_norm_c, w_qb_c, kv_norm_c, w_kvb_c, w_out_c, final_norm):
    bp, tp, _ = x_prompt.shape
    bs, ts, _ = x_sample.shape
    past = cache_a_k.shape[2]
    tables_a, tables_c = _rope_tables(ts)
    w_ab_t = jnp.swapaxes(w_in_ab, 1, 2)
    w_c_t = jnp.swapaxes(w_in_c, 1, 2)

    cond = jnp.zeros((16, D_MODEL), F32).at[0].set(c_ctx).at[1:1 + bs].set(c)
    mods = _adaln(cond, w_mod, b_mod).reshape(DEPTH, 16, 3, D_MODEL)
    gains = norm_g.reshape(DEPTH, 1, D_MODEL)

    yp = x_prompt.reshape(bp * tp, D_MODEL)
    ys = x_sample.reshape(bs * ts, D_MODEL)
    a_k, a_v, b_max, c_kvs, c_krs = [], [], [], [], []
    states = None
    for l in range(DEPTH):
        j = l // 2
        fin = final_norm if l == DEPTH - 1 else None
        if l % 2 == 0:
            w_bf = _cast_rows(w_ab_t, j, MAIN_WIDTH)
            main_p, k_p, v_p, gates_p = _inproj_ab(yp, mods, gains, l, w_bf, w_ab_t, j, tp, True, True, None)
            main_s, gates_s = _inproj_ab(ys, mods, gains, l, w_bf, w_ab_t, j, ts, False, False, tables_a)
            a_k.append(k_p)
            a_v.append(v_p)

            attn_p = _attn_a_prompt(main_p, sink_a[j], bp, tp)
            ck = cache_a_k[:, j].reshape(bs, past, A_KV_WIDTH).astype(BF16)
            cv = cache_a_v[:, j].reshape(bs, past, A_KV_WIDTH).astype(BF16)
            attn_s = _attn_a_sample(main_s, sink_a[j], ck, cv, bs, ts)

            prev = None if states is None else states[:2]
            hb_p, states = _mlstm(main_p, gates_p, conv_b, gate_bias_b, norm_b, None, j, bp, tp, True, prev, B_HEADS)
            init = (state_b_mem, state_b_norm, state_b_max)
            hb_s, _ = _mlstm(main_s, gates_s, conv_b, gate_bias_b, norm_b, init, j, bs, ts, False, None, 1)
            b_max.append(states[2])

            yp = _outproj([(attn_p, main_p, MAIN_ZA), (hb_p, main_p, MAIN_ZB)], w_out_ab, j, yp, mods, l, None, fin)
            ys = _outproj([(attn_s, main_s, MAIN_ZA), (hb_s, main_s, MAIN_ZB)], w_out_ab, j, ys, mods, l, ts, fin)
        else:
            wq = w_qb_c[j].reshape(C_Q_RANK, C_HEADS, C_NOPE + C_ROPE)
            wq = jnp.pad(wq, ((0, 0), (0, 0), (0, C_HEAD_PAD - C_NOPE - C_ROPE)))
            wq = wq.reshape(C_Q_RANK, C_HEADS * C_HEAD_PAD).astype(BF16)
            wkv = w_kvb_c[j].astype(BF16)
            qn = q_norm_c[j].reshape(1, C_Q_RANK)
            kvn = kv_norm_c[j].reshape(1, C_KV_RANK)

            q_p, ks_p, vs_p, z_p, ckv_p, kr_p = _inproj_c(yp, mods, gains, l, w_c_t, j, qn, wq, kvn, wkv, None, None)
            q_s, ks_s, vs_s, z_s, _, _ = _inproj_c(ys, mods, gains, l, w_c_t, j, qn, wq, kvn, wkv, ts, tables_c)
            c_kvs.append(ckv_p.reshape(bp, tp, C_KV_RANK))
            c_krs.append(kr_p[:, C_NOPE:C_NOPE + C_ROPE].reshape(bp, tp, C_ROPE))

            cc = cache_c_kv[:, j].reshape(bs * past, C_KV_RANK).astype(BF16)
            kv_ctx = _matmul(cc, wkv, 512)
            kr_ctx = _pad_cols(cache_c_krope[:, j].reshape(bs * past, C_ROPE), C_NOPE, C_HEAD_PAD).astype(BF16)

            attn_p = _attn_c_dense(q_p, ks_p, vs_p, bp, tp)
            attn_s = _attn_c(q_s, ks_s, vs_s, (kv_ctx, kr_ctx), bs, ts, 512)

            yp = _outproj([(attn_p, z_p, 0)], w_out_c, j, yp, mods, l, None, fin)
            ys = _outproj([(attn_s, z_s, 0)], w_out_c, j, ys, mods, l, ts, fin)

    def cache_layout(per_layer):
        stacked = jnp.stack(per_layer, axis=1).reshape(bp, N_EVEN, A_KV_HEADS, A_HEAD_DIM, tp)
        return stacked.transpose(0, 1, 4, 2, 3)

    b_mem = states[0]
    b_nrm = states[1].reshape(bp, N_EVEN, 2, B_HEADS, B_HEAD_DIM)
    return (yp.reshape(bp, tp, D_MODEL), ys.reshape(bs, ts, D_MODEL),
            cache_layout(a_k), cache_layout(a_v), b_mem, b_nrm,
            jnp.stack(b_max, axis=1), jnp.stack(c_kvs, axis=1), jnp.stack(c_krs, axis=1))
```
